```python
import math
import jax, jax.numpy as jnp
from jax import lax
import numpy as np

D_MODEL = 1024
BATCH = 8
SEQ = 2048
DEPTH = 2

GLA_HEADS = 4
GLA_DK = 64
GLA_DV = 128
GLA_LOWRANK = 16
GLA_TAU = 16.0
GLA_CHUNK = 64
DIFF_HEADS = 4
DIFF_DQK = 64
DIFF_DV = 128
Q_BLOCK = 128
NUM_BUCKETS = 32
MAX_DISTANCE = 128
S5_GROUPS = 32
S5_GROUP_CH = 16
S5_STATE = 64
S5_WIDTH = S5_GROUPS * S5_GROUP_CH
S5_DT_MIN = 1e-3
S5_DT_MAX = 1e-1
D_FF = 2816
N_BRANCH = 3
ADA_CHUNKS = 9
EPS = 1e-6

GLA_QK_W = GLA_HEADS * GLA_DK
GLA_V_W = GLA_HEADS * GLA_DV
DIFF_QK_W = DIFF_HEADS * 2 * DIFF_DQK
DIFF_V_W = DIFF_HEADS * DIFF_DV
BRANCH_W = 512
SPLIT_WIDTHS = (GLA_QK_W, GLA_QK_W, GLA_V_W, GLA_V_W, GLA_LOWRANK,
                DIFF_QK_W, DIFF_QK_W, DIFF_V_W, S5_WIDTH, N_BRANCH * D_MODEL)
IN_W = 256 + 256 + 512 + 512 + 16 + 512 + 512 + 512 + 512 + 3 * 1024

kernel_name = 'hybrid_gla_diffattn_s5_macaron_block'


def rmsnorm(x, g):
    xf = x.astype(jnp.float32)
    y = xf * lax.rsqrt(jnp.mean(xf * xf, axis=-1, keepdims=True) + EPS)
    return (y * g.astype(jnp.float32)).astype(x.dtype)


def modulate(x, g, shift, scale):
    return rmsnorm(x, g) * (1.0 + scale[:, None, :]) + shift[:, None, :]


def swiglu(u, w_up, w_down):
    gu = u @ w_up
    gate, up = gu[..., :D_FF], gu[..., D_FF:]
    return (jax.nn.silu(gate) * up) @ w_down


def gla_mixer(q, k, v, r, a_low, w_alpha, b_alpha, norm_g):
    bsz, L, _ = q.shape
    n_chunks = L // GLA_CHUNK
    f32 = jnp.float32

    def heads(t, d):
        return t.astype(f32).reshape(bsz, n_chunks, GLA_CHUNK, GLA_HEADS, d).transpose(0, 3, 1, 2, 4)

    log_a = jax.nn.log_sigmoid((a_low @ w_alpha + b_alpha).astype(f32)) / GLA_TAU
    qh = heads(q, GLA_DK) * (GLA_DK ** -0.5)
    kh = heads(k, GLA_DK)
    vh = heads(v, GLA_DV)
    b = jnp.cumsum(heads(log_a, GLA_DK), axis=3)
    b_last = b[:, :, :, -1:, :]
    q_dec = qh * jnp.exp(b)
    k_dec = kh * jnp.exp(-b)
    causal = jnp.tril(jnp.ones((GLA_CHUNK, GLA_CHUNK), dtype=bool))
    att = jnp.where(causal, jnp.einsum('bhncd,bhnsd->bhncs', q_dec, k_dec), 0.0)
    o_intra = jnp.einsum('bhncs,bhnse->bhnce', att, vh)
    d_state = jnp.einsum('bhncd,bhnce->bhnde', kh * jnp.exp(b_last - b), vh)
    chunk_decay = jnp.exp(b_last[:, :, :, 0, :])

    def step(S, inp):
        dec, ds = inp
        return dec[..., None] * S + ds, S

    s0 = jnp.zeros((bsz, GLA_HEADS, GLA_DK, GLA_DV), f32)
    _, s_prev = lax.scan(step, s0, (jnp.moveaxis(chunk_decay, 2, 0), jnp.moveaxis(d_state, 2, 0)))
    s_prev = jnp.moveaxis(s_prev, 0, 2)
    o = o_intra + jnp.einsum('bhncd,bhnde->bhnce', q_dec, s_prev)
    o = o * lax.rsqrt(jnp.mean(o * o, axis=-1, keepdims=True) + EPS)
    o = o.transpose(0, 2, 3, 1, 4).reshape(bsz, L, GLA_V_W) * norm_g.astype(f32)
    return (o * jax.nn.silu(r.astype(f32))).astype(q.dtype)


def t5_bucket(dist):
    max_exact = NUM_BUCKETS // 2
    d = jnp.maximum(dist, 1).astype(jnp.float32)
    large = max_exact + (jnp.log(d / max_exact) / math.log(MAX_DISTANCE / max_exact)
                         * (NUM_BUCKETS - max_exact)).astype(jnp.int32)
    large = jnp.minimum(large, NUM_BUCKETS - 1)
    return jnp.where(dist < max_exact, dist, large)


def diff_attention(q, k, v, lam_vecs, lam_init, norm_g, rel_bias):
    bsz, L, _ = q.shape
    f32 = jnp.float32
    n_maps = 2 * DIFF_HEADS
    qm = q.astype(f32).reshape(bsz, L, n_maps, DIFF_DQK).transpose(0, 2, 1, 3) * (DIFF_DQK ** -0.5)
    km = k.astype(f32).reshape(bsz, L, n_maps, DIFF_DQK).transpose(0, 2, 1, 3)
    vh = v.reshape(bsz, L, DIFF_HEADS, DIFF_DV).transpose(0, 2, 1, 3)
    lv = lam_vecs.astype(f32)
    lam = jnp.exp(jnp.sum(lv[0] * lv[1])) - jnp.exp(jnp.sum(lv[2] * lv[3])) + lam_init
    outs = []
    for blk in range(L // Q_BLOCK):
        s, e = blk * Q_BLOCK, (blk + 1) * Q_BLOCK
        logits = jnp.einsum('bmqd,bmkd->bmqk', qm[:, :, s:e], km[:, :, :e])
        dist = jnp.arange(s, e)[:, None] - jnp.arange(e)[None, :]
        bias = rel_bias.astype(f32)[t5_bucket(jnp.maximum(dist, 0))].transpose(2, 0, 1)
        logits = jnp.where(dist >= 0, logits + bias, -jnp.inf)
        p = jax.nn.softmax(logits, axis=-1).reshape(bsz, DIFF_HEADS, 2, Q_BLOCK, e)
        a = p[:, :, 0] - lam * p[:, :, 1]
        outs.append(jnp.einsum('bhqk,bhkd->bhqd', a, vh[:, :, :e].astype(f32)))
    o = jnp.concatenate(outs, axis=2)
    o = o * lax.rsqrt(jnp.mean(o * o, axis=-1, keepdims=True) + EPS)
    o = o.transpose(0, 2, 1, 3).reshape(bsz, L, DIFF_V_W) * norm_g.astype(f32) * (1.0 - lam_init)
    return o.astype(q.dtype)


def s5_mixer(u, lam_re, lam_im, log_step, b_re, b_im, c_re, c_im, d_skip, w_glu, b_glu):
    bsz, L, _ = u.shape
    f32 = jnp.float32
    uf = u.astype(f32).reshape(bsz, L, S5_GROUPS, S5_GROUP_CH)
    lr, li = lam_re.astype(f32), lam_im.astype(f32)
    dt = jnp.exp(log_step.astype(f32))[:, None]
    mag = jnp.exp(lr * dt)
    abar_re, abar_im = mag * jnp.cos(li * dt), mag * jnp.sin(li * dt)
    nr, ni = abar_re - 1.0, abar_im
    den = lr * lr + li * li
    f_re = (nr * lr + ni * li) / den
    f_im = (ni * lr - nr * li) / den
    br, bi = b_re.astype(f32), b_im.astype(f32)
    bbar_re = f_re[..., None] * br - f_im[..., None] * bi
    bbar_im = f_re[..., None] * bi + f_im[..., None] * br
    bu_re = jnp.einsum('blgh,gph->blgp', uf, bbar_re)
    bu_im = jnp.einsum('blgh,gph->blgp', uf, bbar_im)
    a_re = jnp.broadcast_to(abar_re, bu_re.shape)
    a_im = jnp.broadcast_to(abar_im, bu_im.shape)

    def combine(e1, e2):
        a1r, a1i, b1r, b1i = e1
        a2r, a2i, b2r, b2i = e2
        return (a2r * a1r - a2i * a1i,
                a2r * a1i + a2i * a1r,
                a2r * b1r - a2i * b1i + b2r,
                a2r * b1i + a2i * b1r + b2i)

    _, _, xr, xi = lax.associative_scan(combine, (a_re, a_im, bu_re, bu_im), axis=1)
    y = (jnp.einsum('gnp,blgp->blgn', c_re.astype(f32), xr)
         - jnp.einsum('gnp,blgp->blgn', c_im.astype(f32), xi))
    y = y + d_skip.astype(f32) * uf
    y = jax.nn.gelu(y.reshape(bsz, L, S5_WIDTH))
    z = y @ w_glu.astype(f32) + b_glu.astype(f32)
    return (z[..., :S5_WIDTH] * jax.nn.sigmoid(z[..., S5_WIDTH:])).astype(u.dtype)


def hybrid_mixer(u, layer_idx, w_in, gla_w_alpha, gla_b_alpha, gla_norm, diff_lambda, diff_norm,
                 s5_lam_re, s5_lam_im, s5_log_step, s5_b_re, s5_b_im, s5_c_re, s5_c_im, s5_d,
                 s5_w_glu, s5_b_glu, w_branch, w_out, rel_bias):
    bsz, L, _ = u.shape
    proj = u @ w_in
    offsets = [int(o) for o in np.cumsum(SPLIT_WIDTHS)[:-1]]
    (g_q, g_k, g_v, g_r, g_a, d_q, d_k, d_v, s_u, gate_logits) = jnp.split(proj, offsets, axis=-1)
    y_gla = gla_mixer(g_q, g_k, g_v, g_r, g_a, gla_w_alpha, gla_b_alpha, gla_norm)
    lam_init = 0.8 - 0.6 * math.exp(-0.3 * layer_idx)
    y_diff = diff_attention(d_q, d_k, d_v, diff_lambda, lam_init, diff_norm, rel_bias)
    y_s5 = s5_mixer(s_u, s5_lam_re, s5_lam_im, s5_log_step, s5_b_re, s5_b_im, s5_c_re, s5_c_im,
                    s5_d, s5_w_glu, s5_b_glu)
    gates = jax.nn.sigmoid(gate_logits.astype(jnp.float32)).reshape(bsz, L, N_BRANCH, D_MODEL).astype(u.dtype)
    merged = (gates[:, :, 0] * (y_gla @ w_branch[0])
              + gates[:, :, 1] * (y_diff @ w_branch[1])
              + gates[:, :, 2] * (y_s5 @ w_branch[2]))
    return merged @ w_out


def setup_inputs(seed: int = 0) -> dict:
    key = jax.random.key(seed)
    ks = jax.random.split(key, 32)
    f32 = jnp.float32
    nrm = lambda k, shape, s: jax.random.normal(k, shape, f32) * s
    n_idx = jnp.arange(S5_STATE, dtype=f32)
    return {
        'x': nrm(ks[0], (BATCH, SEQ, D_MODEL), 1.0),
        'c': nrm(ks[1], (BATCH, D_MODEL), 1.0),
        'rel_bias': nrm(ks[2], (NUM_BUCKETS, 2 * DIFF_HEADS), 0.5),
        'w_ada': nrm(ks[3], (DEPTH, D_MODEL, ADA_CHUNKS * D_MODEL), D_MODEL ** -0.5),
        'b_ada': nrm(ks[4], (DEPTH, ADA_CHUNKS * D_MODEL), 0.02),
        'norm_pre': 1.0 + nrm(ks[5], (DEPTH, 3, D_MODEL), 0.05),
        'norm_post': 1.0 + nrm(ks[6], (DEPTH, 3, D_MODEL), 0.05),
        'ffn1_w_up': nrm(ks[7], (DEPTH, D_MODEL, 2 * D_FF), D_MODEL ** -0.5),
        'ffn1_w_down': nrm(ks[8], (DEPTH, D_FF, D_MODEL), D_FF ** -0.5),
        'ffn2_w_up': nrm(ks[9], (DEPTH, D_MODEL, 2 * D_FF), D_MODEL ** -0.5),
        'ffn2_w_down': nrm(ks[10], (DEPTH, D_FF, D_MODEL), D_FF ** -0.5),
        'w_in': nrm(ks[11], (DEPTH, D_MODEL, IN_W), D_MODEL ** -0.5),
        'gla_w_alpha': nrm(ks[12], (DEPTH, GLA_LOWRANK, GLA_QK_W), GLA_LOWRANK ** -0.5),
        'gla_b_alpha': nrm(ks[13], (DEPTH, GLA_QK_W), 0.1),
        'gla_norm': 1.0 + nrm(ks[14], (DEPTH, GLA_V_W), 0.05),
        'diff_lambda': nrm(ks[15], (DEPTH, 4, DIFF_DQK), 0.1),
        'diff_norm': 1.0 + nrm(ks[16], (DEPTH, DIFF_V_W), 0.05),
        's5_lam_re': -0.5 + nrm(ks[17], (DEPTH, S5_GROUPS, S5_STATE), 0.01),
        's5_lam_im': math.pi * n_idx + nrm(ks[18], (DEPTH, S5_GROUPS, S5_STATE), 0.01),
        's5_log_step': jax.random.uniform(ks[19], (DEPTH, S5_GROUPS), f32,
                                          math.log(S5_DT_MIN), math.log(S5_DT_MAX)),
        's5_b_re': nrm(ks[20], (DEPTH, S5_GROUPS, S5_STATE, S5_GROUP_CH), (2 * S5_GROUP_CH) ** -0.5),
        's5_b_im': nrm(ks[21], (DEPTH, S5_GROUPS, S5_STATE, S5_GROUP_CH), (2 * S5_GROUP_CH) ** -0.5),
        's5_c_re': nrm(ks[22], (DEPTH, S5_GROUPS, S5_GROUP_CH, S5_STATE), (2 * S5_STATE) ** -0.5),
        's5_c_im': nrm(ks[23], (DEPTH, S5_GROUPS, S5_GROUP_CH, S5_STATE), (2 * S5_STATE) ** -0.5),
        's5_d': nrm(ks[24], (DEPTH, S5_GROUPS, S5_GROUP_CH), 1.0),
        's5_w_glu': nrm(ks[25], (DEPTH, S5_WIDTH, 2 * S5_WIDTH), S5_WIDTH ** -0.5),
        's5_b_glu': nrm(ks[26], (DEPTH, 2 * S5_WIDTH), 0.02),
        'w_branch': nrm(ks[27], (DEPTH, N_BRANCH, BRANCH_W, D_MODEL), BRANCH_W ** -0.5),
        'w_out': nrm(ks[28], (DEPTH, D_MODEL, D_MODEL), D_MODEL ** -0.5),
    }


def reference(x, c, rel_bias, w_ada, b_ada, norm_pre, norm_post, ffn1_w_up, ffn1_w_down,
              ffn2_w_up, ffn2_w_down, w_in, gla_w_alpha, gla_b_alpha, gla_norm, diff_lambda,
              diff_norm, s5_lam_re, s5_lam_im, s5_log_step, s5_b_re, s5_b_im, s5_c_re, s5_c_im,
              s5_d, s5_w_glu, s5_b_glu, w_branch, w_out):
    h = x
    cond = jax.nn.silu(c)
    bsz = c.shape[0]
    for l in range(DEPTH):
        ada = (cond @ w_ada[l] + b_ada[l]).reshape(bsz, ADA_CHUNKS, D_MODEL)
        u = modulate(h, norm_pre[l, 0], ada[:, 0], ada[:, 1])
        y = swiglu(u, ffn1_w_up[l], ffn1_w_down[l])
        h = h + 0.5 * ada[:, 2][:, None, :] * rmsnorm(y, norm_post[l, 0])
        u = modulate(h, norm_pre[l, 1], ada[:, 3], ada[:, 4])
        y = hybrid_mixer(u, l, w_in[l], gla_w_alpha[l], gla_b_alpha[l], gla_norm[l], diff_lambda[l],
                         diff_norm[l], s5_lam_re[l], s5_lam_im[l], s5_log_step[l], s5_b_re[l],
                         s5_b_im[l], s5_c_re[l], s5_c_im[l], s5_d[l], s5_w_glu[l], s5_b_glu[l],
                         w_branch[l], w_out[l], rel_bias)
        h = h + ada[:, 5][:, None, :] * rmsnorm(y, norm_post[l, 1])
        u = modulate(h, norm_pre[l, 2], ada[:, 6], ada[:, 7])
        y = swiglu(u, ffn2_w_up[l], ffn2_w_down[l])
        h = h + 0.5 * ada[:, 8][:, None, :] * rmsnorm(y, norm_post[l, 2])
    return h
```

```python
import functools
import math

import numpy as np
import jax
import jax.numpy as jnp
from jax import lax
from jax.experimental import pallas as pl
from jax.experimental.pallas import tpu as pltpu

F32 = jnp.float32
BF16 = jnp.bfloat16

D_MODEL = 1024
DEPTH = 2
D_FF = 2816
ADA_CHUNKS = 9
EPS = 1e-6

GLA_HEADS = 4
GLA_DK = 64
GLA_DV = 128
GLA_LOWRANK = 16
GLA_TAU = 16.0
GLA_CHUNK = 64
GLA_QK_W = GLA_HEADS * GLA_DK
GLA_V_W = GLA_HEADS * GLA_DV

DIFF_HEADS = 4
DIFF_DQK = 64
DIFF_DV = 128
DIFF_QK_W = DIFF_HEADS * 2 * DIFF_DQK
DIFF_V_W = DIFF_HEADS * DIFF_DV
NUM_BUCKETS = 32
MAX_DISTANCE = 128

S5_GROUPS = 32
S5_GROUP_CH = 16
S5_STATE = 64
S5_WIDTH = S5_GROUPS * S5_GROUP_CH
S5_NSTATE = S5_GROUPS * S5_STATE
S5_GROUP_BLOCKS = 4
S5_BLOCK_CH = S5_WIDTH // S5_GROUP_BLOCKS
S5_BLOCK_STATE = S5_NSTATE // S5_GROUP_BLOCKS

N_BRANCH = 3

SUBLANES = 8
LANES = 128
VMEM_LIMIT_BYTES = 56 * 1024 * 1024

TOKEN_TILE = 512
FF_CHUNK = 256
FF_NCHUNK = D_FF // FF_CHUNK
GLA_PAIR = 2 * GLA_CHUNK
GLA_TILE = 512
ATT_TILE = 256
S5_TIME_TILE = 32
ADA_COL_TILE = 1536


def _dot(a, b):
    return jnp.dot(a, b, preferred_element_type=F32)


def _dot_nt(a, b):
    return lax.dot_general(a, b, (((1,), (1,)), ((), ())), preferred_element_type=F32)


def _rms(x):
    return x * lax.rsqrt(jnp.mean(x * x, axis=-1, keepdims=True) + EPS)


def _prenorm_mod(h, g, shift, scale):
    return (_rms(h) * g) * (1.0 + scale) + shift


def _params(*sem):
    return pltpu.CompilerParams(dimension_semantics=sem, vmem_limit_bytes=VMEM_LIMIT_BYTES)


def _resident(shape, index_map):
    return pl.BlockSpec(shape, index_map, pipeline_mode=pl.Buffered(1))


def _ada_kernel(c_ref, w_ref, b_ref, o_ref):
    cond = jax.nn.silu(c_ref[...]).astype(BF16)
    o_ref[...] = _dot(cond, w_ref[...].astype(BF16)) + b_ref[...]


def _ada_call(c, w_ada, b_ada):
    bsz = c.shape[0]
    n = ADA_CHUNKS * D_MODEL
    return pl.pallas_call(
        _ada_kernel,
        grid=(DEPTH, n // ADA_COL_TILE),
        in_specs=[
            pl.BlockSpec((bsz, D_MODEL), lambda l, j: (0, 0)),
            pl.BlockSpec((None, D_MODEL, ADA_COL_TILE), lambda l, j: (l, 0, j)),
            pl.BlockSpec((None, 1, ADA_COL_TILE), lambda l, j: (l, 0, j)),
        ],
        out_specs=pl.BlockSpec((None, bsz, ADA_COL_TILE), lambda l, j: (l, 0, j)),
        out_shape=jax.ShapeDtypeStruct((DEPTH, bsz, n), F32),
        compiler_params=_params("arbitrary", "arbitrary"),
        name="ada_proj",
    )(c, w_ada, b_ada.reshape(DEPTH, 1, n))


def _ffn_kernel(h_ref, ada_ref, npre_ref, npost_ref, wg_ref, wp_ref, wd_ref, o_ref, acc_ref, *, sub):
    h = h_ref[...]
    a0 = 3 * sub
    u = _prenorm_mod(h, npre_ref[sub:sub + 1, :], ada_ref[a0:a0 + 1, :], ada_ref[a0 + 1:a0 + 2, :])
    u = u.astype(BF16)
    acc_ref[...] = jnp.zeros_like(acc_ref)

    def chunk(j, carry):
        gate = _dot(u, wg_ref[j])
        up = _dot(u, wp_ref[j])
        mid = (jax.nn.silu(gate) * up).astype(BF16)
        acc_ref[...] += _dot(mid, wd_ref[j])
        return carry

    lax.fori_loop(0, FF_NCHUNK, chunk, 0)
    post = _rms(acc_ref[...]) * npost_ref[sub:sub + 1, :]
    o_ref[...] = h + (0.5 * ada_ref[a0 + 2:a0 + 3, :]) * post


def _ffn_call(h, ada4, norm_pre, norm_post, wg, wp, wd, layer, sub):
    bsz, seq, _ = h.shape
    tm = TOKEN_TILE
    tok = pl.BlockSpec((None, tm, D_MODEL), lambda b, i: (b, i, 0))
    return pl.pallas_call(
        functools.partial(_ffn_kernel, sub=sub),
        grid=(bsz, seq // tm),
        in_specs=[
            tok,
            pl.BlockSpec((None, None, ADA_CHUNKS, D_MODEL), lambda b, i: (layer, b, 0, 0)),
            pl.BlockSpec((None, 3, D_MODEL), lambda b, i: (layer, 0, 0)),
            pl.BlockSpec((None, 3, D_MODEL), lambda b, i: (layer, 0, 0)),
            _resident((None, FF_NCHUNK, D_MODEL, FF_CHUNK), lambda b, i: (layer, 0, 0, 0)),
            _resident((None, FF_NCHUNK, D_MODEL, FF_CHUNK), lambda b, i: (layer, 0, 0, 0)),
            _resident((None, FF_NCHUNK, FF_CHUNK, D_MODEL), lambda b, i: (layer, 0, 0, 0)),
        ],
        out_specs=tok,
        out_shape=jax.ShapeDtypeStruct(h.shape, F32),
        scratch_shapes=[pltpu.VMEM((tm, D_MODEL), F32)],
        compiler_params=_params("arbitrary", "arbitrary"),
        name=f"ffn{sub // 2 + 1}",
    )(h, ada4, norm_pre, norm_post, wg, wp, wd)


def _ffn_weights(w_up, w_down):
    def chunked(w):
        return w.astype(BF16).reshape(DEPTH, D_MODEL, FF_NCHUNK, FF_CHUNK).transpose(0, 2, 1, 3)
    wg = chunked(w_up[:, :, :D_FF])
    wp = chunked(w_up[:, :, D_FF:])
    wd = w_down.astype(BF16).reshape(DEPTH, FF_NCHUNK, FF_CHUNK, D_MODEL)
    return wg, wp, wd


def _log_sigmoid(z):
    return jnp.minimum(z, 0.0) - jnp.log1p(jnp.exp(-jnp.abs(z)))


def _inproj_kernel(h_ref, ada_ref, npre_ref, wgla_ref, wa_ref, walpha_ref, balpha_ref, wdiff_ref, ws5_ref,
                   gq_ref, gk_ref, gv_ref, gr_ref, la_ref, dq_ref, dk_ref, dv_ref, su_ref):
    u = _prenorm_mod(h_ref[...], npre_ref[1:2, :], ada_ref[3:4, :], ada_ref[4:5, :]).astype(BF16)
    pg = _dot(u, wgla_ref[...])
    gq_ref[...] = pg[:, :GLA_QK_W]
    gk_ref[...] = pg[:, GLA_QK_W:2 * GLA_QK_W]
    gv_ref[...] = pg[:, 2 * GLA_QK_W:2 * GLA_QK_W + GLA_V_W].astype(BF16)
    gr_ref[...] = pg[:, 2 * GLA_QK_W + GLA_V_W:]
    a_low = _dot(u, wa_ref[...]).astype(BF16)
    z = _dot(a_low, walpha_ref[...]) + balpha_ref[...]
    la_ref[...] = _log_sigmoid(z) / GLA_TAU
    pd = _dot(u, wdiff_ref[...])
    dq_ref[...] = pd[:, :DIFF_QK_W].astype(BF16)
    dk_ref[...] = pd[:, DIFF_QK_W:2 * DIFF_QK_W].astype(BF16)
    dv_ref[...] = pd[:, 2 * DIFF_QK_W:].astype(BF16)
    su_ref[...] = _dot(u, ws5_ref[...]).astype(BF16)


def _inproj_call(h, ada4, norm_pre, wgla, wa, walpha, balpha, wdiff, ws5, layer):
    bsz, seq, _ = h.shape
    tm = TOKEN_TILE

    def tok(w):
        return pl.BlockSpec((None, tm, w), lambda b, i: (b, i, 0))

    def sds(w, dt):
        return jax.ShapeDtypeStruct((bsz, seq, w), dt)

    def res(arr):
        nd = arr.ndim - 1
        return _resident((None,) + arr.shape[1:], lambda b, i: (layer,) + (0,) * nd)

    return pl.pallas_call(
        _inproj_kernel,
        grid=(bsz, seq // tm),
        in_specs=[
            tok(D_MODEL),
            pl.BlockSpec((None, None, ADA_CHUNKS, D_MODEL), lambda b, i: (layer, b, 0, 0)),
            pl.BlockSpec((None, 3, D_MODEL), lambda b, i: (layer, 0, 0)),
            res(wgla), res(wa), res(walpha), res(balpha), res(wdiff), res(ws5),
        ],
        out_specs=[tok(GLA_QK_W), tok(GLA_QK_W), tok(GLA_V_W), tok(GLA_V_W), tok(GLA_QK_W),
                   tok(DIFF_QK_W), tok(DIFF_QK_W), tok(DIFF_V_W),
                   pl.BlockSpec((tm, S5_WIDTH), lambda b, i: (i, b))],
        out_shape=[sds(GLA_QK_W, F32), sds(GLA_QK_W, F32), sds(GLA_V_W, BF16), sds(GLA_V_W, F32),
                   sds(GLA_QK_W, F32), sds(DIFF_QK_W, BF16), sds(DIFF_QK_W, BF16), sds(DIFF_V_W, BF16),
                   jax.ShapeDtypeStruct((seq, bsz * S5_WIDTH), BF16)],
        compiler_params=_params("arbitrary", "arbitrary"),
        name="mixer_in_proj",
    )(h, ada4, norm_pre, wgla, wa, walpha, balpha, wdiff, ws5)


def _cumsum_rows(tri, x):
    hi = x.astype(BF16)
    r1 = x - hi.astype(F32)
    mid = r1.astype(BF16)
    lo = (r1 - mid.astype(F32)).astype(BF16)
    return _dot(tri, hi) + _dot(tri, mid) + _dot(tri, lo)


def _gla_kernel(q_ref, k_ref, v_ref, r_ref, la_ref, g_ref, o_ref, s_ref, *, n_pairs):
    @pl.when(pl.program_id(1) == 0)
    def _():
        s_ref[...] = jnp.zeros_like(s_ref)

    row = lax.broadcasted_iota(jnp.int32, (GLA_PAIR, GLA_PAIR), 0)
    col = lax.broadcasted_iota(jnp.int32, (GLA_PAIR, GLA_PAIR), 1)
    causal = (col <= row) & ((row // GLA_CHUNK) == (col // GLA_CHUNK))
    tri = jnp.where(causal, 1.0, 0.0).astype(BF16)
    qlane = lax.broadcasted_iota(jnp.int32, (GLA_PAIR, GLA_QK_W), 1)
    tlane = lax.broadcasted_iota(jnp.int32, (GLA_QK_W, GLA_PAIR), 1)
    first = tlane < GLA_CHUNK

    def pair(p, carry):
        rows = pl.ds(pl.multiple_of(p * GLA_PAIR, GLA_PAIR), GLA_PAIR)
        b = _cumsum_rows(tri, la_ref[rows, :])
        b_t = b.T
        qd = (q_ref[rows, :] * (GLA_DK ** -0.5)) * jnp.exp(b)
        k_t = k_ref[rows, :].T
        kd_t = (k_t * jnp.exp(-b_t)).astype(BF16)
        bl0 = b_t[:, GLA_CHUNK - 1:GLA_CHUNK]
        bl1 = b_t[:, GLA_PAIR - 1:GLA_PAIR]
        kdec_t = k_t * jnp.exp(jnp.where(first, bl0, bl1) - b_t)
        kdec0 = jnp.where(first, kdec_t, 0.0).astype(BF16)
        kdec1 = jnp.where(first, 0.0, kdec_t).astype(BF16)
        v = v_ref[rows, :]
        r = r_ref[rows, :]

        s0 = s_ref[...]
        ds0, ds1 = [], []
        for h in range(GLA_HEADS):
            hr = slice(h * GLA_DK, (h + 1) * GLA_DK)
            vh = v[:, h * GLA_DV:(h + 1) * GLA_DV]
            ds0.append(_dot(kdec0[hr, :], vh))
            ds1.append(_dot(kdec1[hr, :], vh))
        s1 = jnp.exp(bl0) * s0 + jnp.concatenate(ds0, axis=0)
        s2 = jnp.exp(bl1) * s1 + jnp.concatenate(ds1, axis=0)
        s_ref[...] = s2
        s0b = s0.astype(BF16)
        s1b = s1.astype(BF16)

        for h in range(GLA_HEADS):
            vc = slice(h * GLA_DV, (h + 1) * GLA_DV)
            qh = jnp.where(qlane // GLA_DK == h, qd, 0.0).astype(BF16)
            att = jnp.where(causal, _dot(qh, kd_t), 0.0).astype(BF16)
            inter = jnp.concatenate([_dot(qh[:GLA_CHUNK], s0b), _dot(qh[GLA_CHUNK:], s1b)], axis=0)
            o = _rms(_dot(att, v[:, vc]) + inter) * g_ref[:, vc]
            o_ref[rows, vc] = (o * jax.nn.silu(r[:, vc])).astype(BF16)
        return carry

    lax.fori_loop(0, n_pairs, pair, 0)


def _gla_call(gq, gk, gv, gr, la, gla_norm, layer):
    bsz, seq, _ = gq.shape
    tl = GLA_TILE

    def tok(w):
        return pl.BlockSpec((None, tl, w), lambda b, i: (b, i, 0))

    return pl.pallas_call(
        functools.partial(_gla_kernel, n_pairs=tl // GLA_PAIR),
        grid=(bsz, seq // tl),
        in_specs=[tok(GLA_QK_W), tok(GLA_QK_W), tok(GLA_V_W), tok(GLA_V_W), tok(GLA_QK_W),
                  pl.BlockSpec((None, 1, GLA_V_W), lambda b, i: (layer, 0, 0))],
        out_specs=tok(GLA_V_W),
        out_shape=jax.ShapeDtypeStruct((bsz, seq, GLA_V_W), BF16),
        scratch_shapes=[pltpu.VMEM((GLA_QK_W, GLA_DV), F32)],
        compiler_params=_params("arbitrary", "arbitrary"),
        name="gla_mixer",
    )(gq, gk, gv, gr, la, gla_norm.reshape(DEPTH, 1, GLA_V_W))


def _t5_bucket_tiles():
    t = ATT_TILE
    r = np.arange(t)[:, None]
    c = np.arange(t)[None, :]
    max_exact = NUM_BUCKETS // 2
    tiles = []
    for off in range(3):
        dist = off * t + r - c
        d = np.maximum(dist, 1).astype(np.float32)
        large = max_exact + (np.log(d / np.float32(max_exact)) / np.float32(math.log(MAX_DISTANCE / max_exact))
                             * np.float32(NUM_BUCKETS - max_exact)).astype(np.int32)
        large = np.minimum(large, NUM_BUCKETS - 1)
        bucket = np.where(dist < max_exact, dist, large)
        tiles.append(np.where(dist >= 0, bucket, -1))
    far = tiles[2]
    assert (far == NUM_BUCKETS - 1).all()
    return np.stack(tiles).astype(np.int32)


def _bias_kernel(rel_ref, bucket_ref, o_ref):
    m = pl.program_id(0)
    bucket = bucket_ref[...]
    acc = jnp.full(bucket.shape, -jnp.inf, F32)
    for kk in range(NUM_BUCKETS):
        acc = jnp.where(bucket == kk, rel_ref[kk, m], acc)
    o_ref[...] = acc


def _bias_call(rel_bias):
    n_maps = 2 * DIFF_HEADS
    t = ATT_TILE
    return pl.pallas_call(
        _bias_kernel,
        grid=(n_maps,),
        in_specs=[pl.BlockSpec(memory_space=pltpu.SMEM),
                  pl.BlockSpec((3, t, t), lambda m: (0, 0, 0))],
        out_specs=pl.BlockSpec((None, 3, t, t), lambda m: (m, 0, 0, 0)),
        out_shape=jax.ShapeDtypeStruct((n_maps, 3, t, t), F32),
        compiler_params=_params("arbitrary"),
        name="t5_bias_tiles",
    )(rel_bias, jnp.asarray(_t5_bucket_tiles()))


def _diff_kernel(q_ref, k_ref, v_ref, bias_ref, lam_ref, g_ref, o_ref, *, lam_init):
    t = ATT_TILE
    qi = pl.program_id(2)
    q = q_ref[...]
    lane = lax.broadcasted_iota(jnp.int32, q.shape, 1)
    scale = DIFF_DQK ** -0.5
    zero = jnp.zeros_like(q)
    qs = (jnp.where(lane < DIFF_DQK, q, zero) * scale, jnp.where(lane < DIFF_DQK, zero, q) * scale)

    def body(kj, carry):
        ks = pl.ds(pl.multiple_of(kj * t, t), t)
        kb = k_ref[ks, :]
        vb = v_ref[ks, :]
        off = jnp.minimum(qi - kj, 2)
        out = []
        for mp in range(2):
            m_old, l_old, a_old = carry[mp]
            s = _dot_nt(qs[mp], kb) + bias_ref[mp, off]
            m_new = jnp.maximum(m_old, jnp.max(s, axis=-1, keepdims=True))
            alpha = jnp.exp(m_old - m_new)
            p = jnp.exp(s - m_new)
            l_new = alpha * l_old + jnp.sum(p, axis=-1, keepdims=True)
            a_new = alpha * a_old + _dot(p.astype(BF16), vb)
            out.append((m_new, l_new, a_new))
        return tuple(out)

    init = tuple((jnp.full((t, 1), -jnp.inf, F32), jnp.zeros((t, 1), F32), jnp.zeros((t, DIFF_DV), F32))
                 for _ in range(2))
    (_, l1, a1), (_, l2, a2) = lax.fori_loop(0, qi + 1, body, init)

    lv = lam_ref[...]
    lam = (jnp.exp(jnp.sum(lv[0:1] * lv[1:2], axis=-1, keepdims=True))
           - jnp.exp(jnp.sum(lv[2:3] * lv[3:4], axis=-1, keepdims=True)) + lam_init)
    o = a1 / l1 - lam * (a2 / l2)
    o_ref[...] = (_rms(o) * g_ref[...] * (1.0 - lam_init)).astype(BF16)


def _diff_call(dq, dk, dv, bias, diff_lambda, diff_norm, layer):
    bsz, seq, _ = dq.shape
    t = ATT_TILE
    lam_init = 0.8 - 0.6 * math.exp(-0.3 * layer)
    return pl.pallas_call(
        functools.partial(_diff_kernel, lam_init=lam_init),
        grid=(bsz, DIFF_HEADS, seq // t),
        in_specs=[
            pl.BlockSpec((None, t, 2 * DIFF_DQK), lambda b, h, i: (b, i, h)),
            pl.BlockSpec((None, seq, 2 * DIFF_DQK), lambda b, h, i: (b, 0, h)),
            pl.BlockSpec((None, seq, DIFF_DV), lambda b, h, i: (b, 0, h)),
            pl.BlockSpec((2, 3, t, t), lambda b, h, i: (h, 0, 0, 0)),
            pl.BlockSpec((None, 4, DIFF_DQK), lambda b, h, i: (layer, 0, 0)),
            pl.BlockSpec((None, 1, DIFF_DV), lambda b, h, i: (layer, 0, h)),
        ],
        out_specs=pl.BlockSpec((None, t, DIFF_DV), lambda b, h, i: (b, i, h)),
        out_shape=jax.ShapeDtypeStruct((bsz, seq, DIFF_V_W), BF16),
        compiler_params=_params("arbitrary", "arbitrary", "arbitrary"),
        name="diff_attention",
    )(dq, dk, dv, bias, diff_lambda, diff_norm.reshape(DEPTH, 1, DIFF_V_W))


def _s5_prep_kernel(lre_ref, lim_ref, lstep_ref, bre_ref, bim_ref, cre_ref, cim_ref,
                    are_ref, aim_ref, bmat_ref, cmat_ref):
    lr = lre_ref[...]
    li = lim_ref[...]
    dt = jnp.exp(lstep_ref[...])
    mag = jnp.exp(lr * dt)
    a_re = mag * jnp.cos(li * dt)
    a_im = mag * jnp.sin(li * dt)
    are_ref[...] = jnp.broadcast_to(a_re, are_ref.shape)
    aim_ref[...] = jnp.broadcast_to(a_im, aim_ref.shape)
    nr, ni = a_re - 1.0, a_im
    den = lr * lr + li * li
    f_re = (nr * lr + ni * li) / den
    f_im = (ni * lr - nr * li) / den
    ns = S5_BLOCK_STATE
    for gb in range(S5_GROUP_BLOCKS):
        fr = f_re[:, gb * ns:(gb + 1) * ns]
        fi = f_im[:, gb * ns:(gb + 1) * ns]
        br = bre_ref[gb]
        bi = bim_ref[gb]
        bmat_ref[gb, :, :ns] = (fr * br - fi * bi).astype(BF16)
        bmat_ref[gb, :, ns:] = (fr * bi + fi * br).astype(BF16)
        cmat_ref[gb, :ns, :] = cre_ref[gb].astype(BF16)
        cmat_ref[gb, ns:, :] = (-cim_ref[gb]).astype(BF16)


def _s5_prep_call(lam_re, lam_im, log_step, b_re, b_im, c_re, c_im, bsz):
    d = DEPTH
    nb, ch, ns = S5_GROUP_BLOCKS, S5_BLOCK_CH, S5_BLOCK_STATE
    gpb = S5_GROUPS // nb
    eye = jnp.eye(gpb, dtype=F32)

    def in_blocks(b):
        t = b.reshape(d, nb, gpb, S5_STATE, S5_GROUP_CH).transpose(0, 1, 2, 4, 3)
        return (t[:, :, :, :, None, :] * eye[None, None, :, None, :, None]).reshape(d, nb, ch, ns)

    def out_blocks(c):
        t = c.reshape(d, nb, gpb, S5_GROUP_CH, S5_STATE).transpose(0, 1, 4, 2, 3)
        return (t[:, :, None, :, :, :] * eye[None, None, :, None, :, None]).reshape(d, nb, ns, ch)

    def row(x):
        return x.reshape(d, 1, S5_NSTATE)

    lstep = jnp.broadcast_to(log_step[:, :, None], (d, S5_GROUPS, S5_STATE))

    def full(shape):
        nd = len(shape)
        return pl.BlockSpec((None,) + shape, lambda l: (l,) + (0,) * nd)

    return pl.pallas_call(
        _s5_prep_kernel,
        grid=(d,),
        in_specs=[full((1, S5_NSTATE))] * 3 + [full((nb, ch, ns))] * 2 + [full((nb, ns, ch))] * 2,
        out_specs=[full((bsz, S5_NSTATE)), full((bsz, S5_NSTATE)),
                   full((nb, ch, 2 * ns)), full((nb, 2 * ns, ch))],
        out_shape=[jax.ShapeDtypeStruct((d, bsz, S5_NSTATE), F32), jax.ShapeDtypeStruct((d, bsz, S5_NSTATE), F32),
                   jax.ShapeDtypeStruct((d, nb, ch, 2 * ns), BF16), jax.ShapeDtypeStruct((d, nb, 2 * ns, ch), BF16)],
        compiler_params=_params("arbitrary"),
        name="s5_discretize",
    )(row(lam_re), row(lam_im), row(lstep), in_blocks(b_re), in_blocks(b_im), out_blocks(c_re), out_blocks(c_im))


def _s5_kernel(u_ref, are_ref, aim_ref, bmat_ref, cmat_ref, d_ref, wglu_ref, bglu_ref, o_ref,
               xr_ref, xi_ref, sr_ref, si_ref, y_ref, *, bsz, n_steps):
    @pl.when(pl.program_id(0) == 0)
    def _():
        sr_ref[...] = jnp.zeros_like(sr_ref)
        si_ref[...] = jnp.zeros_like(si_ref)

    ns, ch = S5_BLOCK_STATE, S5_BLOCK_CH
    u = u_ref[...]
    for gb in range(S5_GROUP_BLOCKS):
        bu = _dot(u[:, gb * ch:(gb + 1) * ch], bmat_ref[gb])
        xr_ref[:, gb * ns:(gb + 1) * ns] = bu[:, :ns]
        xi_ref[:, gb * ns:(gb + 1) * ns] = bu[:, ns:]

    for gb in range(S5_GROUP_BLOCKS):
        cols = slice(gb * ns, (gb + 1) * ns)
        ar = are_ref[:, cols]
        ai = aim_ref[:, cols]

        def step(t, carry, cols=cols, ar=ar, ai=ai):
            pr, pi = carry
            rows = pl.ds(pl.multiple_of(t * bsz, bsz), bsz)
            nr = ar * pr - ai * pi + xr_ref[rows, cols]
            ni = ar * pi + ai * pr + xi_ref[rows, cols]
            xr_ref[rows, cols] = nr
            xi_ref[rows, cols] = ni
            return nr, ni

        fr, fi = lax.fori_loop(0, n_steps, step, (sr_ref[:, cols], si_ref[:, cols]))
        sr_ref[:, cols] = fr
        si_ref[:, cols] = fi

    for gb in range(S5_GROUP_BLOCKS):
        cols = slice(gb * ns, (gb + 1) * ns)
        y_ref[:, gb * ch:(gb + 1) * ch] = (_dot(xr_ref[:, cols].astype(BF16), cmat_ref[gb, :ns, :])
                                           + _dot(xi_ref[:, cols].astype(BF16), cmat_ref[gb, ns:, :]))
    y = jax.nn.gelu(y_ref[...] + d_ref[...] * u.astype(F32))
    z = _dot(y.astype(BF16), wglu_ref[...]) + bglu_ref[...]
    o_ref[...] = (z[:, :S5_WIDTH] * jax.nn.sigmoid(z[:, S5_WIDTH:])).astype(BF16)


def _s5_call(su_tb, a_re, a_im, bmat, cmat, d_skip, wglu, bglu, layer, bsz):
    rows_total = su_tb.shape[0]
    tr = S5_TIME_TILE * bsz
    nb, ch, ns = S5_GROUP_BLOCKS, S5_BLOCK_CH, S5_BLOCK_STATE

    def res(shape):
        nd = len(shape)
        return _resident((None,) + shape, lambda i: (layer,) + (0,) * nd)

    return pl.pallas_call(
        functools.partial(_s5_kernel, bsz=bsz, n_steps=S5_TIME_TILE),
        grid=(rows_total // tr,),
        in_specs=[pl.BlockSpec((tr, S5_WIDTH), lambda i: (i, 0)),
                  res((bsz, S5_NSTATE)), res((bsz, S5_NSTATE)),
                  res((nb, ch, 2 * ns)), res((nb, 2 * ns, ch)),
                  res((1, S5_WIDTH)), res((S5_WIDTH, 2 * S5_WIDTH)), res((1, 2 * S5_WIDTH))],
        out_specs=pl.BlockSpec((tr, S5_WIDTH), lambda i: (i, 0)),
        out_shape=jax.ShapeDtypeStruct((rows_total, S5_WIDTH), BF16),
        scratch_shapes=[pltpu.VMEM((tr, S5_NSTATE), F32), pltpu.VMEM((tr, S5_NSTATE), F32),
                        pltpu.VMEM((bsz, S5_NSTATE), F32), pltpu.VMEM((bsz, S5_NSTATE), F32),
                        pltpu.VMEM((tr, S5_WIDTH), F32)],
        compiler_params=_params("arbitrary"),
        name="s5_mixer",
    )(su_tb, a_re, a_im, bmat, cmat, d_skip, wglu, bglu)


def _merge_kernel(h_ref, ada_ref, npre_ref, npost_ref, yg_ref, yd_ref, ys_ref, wgate_ref, wbr_ref, wout_ref, o_ref):
    h = h_ref[...]
    u = _prenorm_mod(h, npre_ref[1:2, :], ada_ref[3:4, :], ada_ref[4:5, :]).astype(BF16)
    merged = None
    for i, y_ref in enumerate((yg_ref, yd_ref, ys_ref)):
        gate = jax.nn.sigmoid(_dot(u, wgate_ref[i]))
        term = gate * _dot(y_ref[...], wbr_ref[i])
        merged = term if merged is None else merged + term
    y = _dot(merged.astype(BF16), wout_ref[...])
    o_ref[...] = h + ada_ref[5:6, :] * (_rms(y) * npost_ref[1:2, :])


def _merge_call(h, ada4, norm_pre, norm_post, y_gla, y_diff, y_s5_t, wgate, wbr, wout, layer):
    bsz, seq, _ = h.shape
    tm = TOKEN_TILE

    def tok(w):
        return pl.BlockSpec((None, tm, w), lambda b, i: (b, i, 0))

    return pl.pallas_call(
        _merge_kernel,
        grid=(bsz, seq // tm),
        in_specs=[
            tok(D_MODEL),
            pl.BlockSpec((None, None, ADA_CHUNKS, D_MODEL), lambda b, i: (layer, b, 0, 0)),
            pl.BlockSpec((None, 3, D_MODEL), lambda b, i: (layer, 0, 0)),
            pl.BlockSpec((None, 3, D_MODEL), lambda b, i: (layer, 0, 0)),
            tok(GLA_V_W), tok(DIFF_V_W),
            pl.BlockSpec((tm, S5_WIDTH), lambda b, i: (i, b)),
            _resident((None, N_BRANCH, D_MODEL, D_MODEL), lambda b, i: (layer, 0, 0, 0)),
            _resident((None, N_BRANCH, S5_WIDTH, D_MODEL), lambda b, i: (layer, 0, 0, 0)),
            _resident((None, D_MODEL, D_MODEL), lambda b, i: (layer, 0, 0)),
        ],
        out_specs=tok(D_MODEL),
        out_shape=jax.ShapeDtypeStruct(h.shape, F32),
        compiler_params=_params("arbitrary", "arbitrary"),
        name="mixer_merge",
    )(h, ada4, norm_pre, norm_post, y_gla, y_diff, y_s5_t, wgate, wbr, wout)


def kernel(x, c, rel_bias, w_ada, b_ada, norm_pre, norm_post, ffn1_w_up, ffn1_w_down, ffn2_w_up, ffn2_w_down,
           w_in, gla_w_alpha, gla_b_alpha, gla_norm, diff_lambda, diff_norm, s5_lam_re, s5_lam_im, s5_log_step,
           s5_b_re, s5_b_im, s5_c_re, s5_c_im, s5_d, s5_w_glu, s5_b_glu, w_branch, w_out):
    bsz, seq, _ = x.shape
    assert bsz == SUBLANES, "the S5 scan keeps the batch on the sublane axis"

    f1 = _ffn_weights(ffn1_w_up, ffn1_w_down)
    f2 = _ffn_weights(ffn2_w_up, ffn2_w_down)
    o_alpha = 2 * GLA_QK_W + 2 * GLA_V_W
    o_diff = o_alpha + GLA_LOWRANK
    o_s5 = o_diff + 2 * DIFF_QK_W + DIFF_V_W
    o_gate = o_s5 + S5_WIDTH
    w_in_b = w_in.astype(BF16)
    wgla = w_in_b[:, :, :o_alpha]
    wa = jnp.pad(w_in_b[:, :, o_alpha:o_diff], ((0, 0), (0, 0), (0, LANES - GLA_LOWRANK)))
    walpha = jnp.pad(gla_w_alpha.astype(BF16), ((0, 0), (0, LANES - GLA_LOWRANK), (0, 0)))
    balpha = gla_b_alpha.reshape(DEPTH, 1, GLA_QK_W)
    wdiff = w_in_b[:, :, o_diff:o_s5]
    ws5 = w_in_b[:, :, o_s5:o_gate]
    wgate = w_in_b[:, :, o_gate:].reshape(DEPTH, D_MODEL, N_BRANCH, D_MODEL).transpose(0, 2, 1, 3)
    wbr = w_branch.astype(BF16)
    wout = w_out.astype(BF16)
    wglu = s5_w_glu.astype(BF16)
    bglu = s5_b_glu.reshape(DEPTH, 1, 2 * S5_WIDTH)
    d_skip = s5_d.reshape(DEPTH, 1, S5_WIDTH)

    ada4 = _ada_call(c, w_ada, b_ada).reshape(DEPTH, bsz, ADA_CHUNKS, D_MODEL)
    bias = _bias_call(rel_bias)
    a_re, a_im, bmat, cmat = _s5_prep_call(s5_lam_re, s5_lam_im, s5_log_step, s5_b_re, s5_b_im,
                                           s5_c_re, s5_c_im, bsz)

    h = x
    for layer in range(DEPTH):
        h = _ffn_call(h, ada4, norm_pre, norm_post, *f1, layer, 0)
        gq, gk, gv, gr, la, dq, dk, dv, su = _inproj_call(h, ada4, norm_pre, wgla, wa, walpha, balpha,
                                                          wdiff, ws5, layer)
        y_gla = _gla_call(gq, gk, gv, gr, la, gla_norm, layer)
        y_diff = _diff_call(dq, dk, dv, bias, diff_lambda, diff_norm, layer)
        y_s5 = _s5_call(su.reshape(seq * bsz, S5_WIDTH), a_re, a_im, bmat, cmat, d_skip, wglu, bglu, layer, bsz)
        h = _merge_call(h, ada4, norm_pre, norm_post, y_gla, y_diff, y_s5.reshape(seq, bsz * S5_WIDTH),
                        wgate, wbr, wout, layer)
        h = _ffn_call(h, ada4, norm_pre, norm_post, *f2, layer, 2)
    return h
```

```python
import functools
import math

import numpy as np
import jax
import jax.numpy as jnp
from jax import lax
from jax.experimental import pallas as pl
from jax.experimental.pallas import tpu as pltpu

F32 = jnp.float32
BF16 = jnp.bfloat16

D_MODEL = 1024
DEPTH = 2
D_FF = 2816
ADA_CHUNKS = 9
EPS = 1e-6

GLA_HEADS = 4
GLA_DK = 64
GLA_DV = 128
GLA_LOWRANK = 16
GLA_TAU = 16.0
GLA_CHUNK = 64
GLA_QK_W = GLA_HEADS * GLA_DK
GLA_V_W = GLA_HEADS * GLA_DV

DIFF_HEADS = 4
DIFF_DQK = 64
DIFF_DV = 128
DIFF_QK_W = DIFF_HEADS * 2 * DIFF_DQK
DIFF_V_W = DIFF_HEADS * DIFF_DV
NUM_BUCKETS = 32
MAX_DISTANCE = 128

S5_GROUPS = 32
S5_GROUP_CH = 16
S5_STATE = 64
S5_WIDTH = S5_GROUPS * S5_GROUP_CH
S5_NSTATE = S5_GROUPS * S5_STATE
S5_GROUP_BLOCKS = 4
S5_BLOCK_CH = S5_WIDTH // S5_GROUP_BLOCKS
S5_BLOCK_STATE = S5_NSTATE // S5_GROUP_BLOCKS

N_BRANCH = 3

SUBLANES = 8
LANES = 128
VMEM_LIMIT_BYTES = 56 * 1024 * 1024

TOKEN_TILE = 512
FF_CHUNK = 256
FF_NCHUNK = D_FF // FF_CHUNK
FFN_LOOKAHEAD = 1
GLA_PAIR = 2 * GLA_CHUNK
GLA_TILE = 512
ATT_TILE = 256
ATT_LOOKAHEAD = 4
S5_TIME_TILE = 32
ADA_COL_TILE = 1536


def _dot(a, b):
    return jnp.dot(a, b, preferred_element_type=F32)


def _dot_nt(a, b):
    return lax.dot_general(a, b, (((1,), (1,)), ((), ())), preferred_element_type=F32)


def _rms(x):
    return x * lax.rsqrt(jnp.mean(x * x, axis=-1, keepdims=True) + EPS)


def _prenorm_mod(h, g, shift, scale):
    return (_rms(h) * g) * (1.0 + scale) + shift


def _params(*sem):
    return pltpu.CompilerParams(dimension_semantics=sem, vmem_limit_bytes=VMEM_LIMIT_BYTES)


def _resident(shape, index_map):
    return pl.BlockSpec(shape, index_map, pipeline_mode=pl.Buffered(1))


def _ada_kernel(c_ref, w_ref, b_ref, o_ref):
    cond = jax.nn.silu(c_ref[...]).astype(BF16)
    o_ref[...] = _dot(cond, w_ref[...].astype(BF16)) + b_ref[...]


def _ada_call(c, w_ada, b_ada):
    bsz = c.shape[0]
    n = ADA_CHUNKS * D_MODEL
    return pl.pallas_call(
        _ada_kernel,
        grid=(DEPTH, n // ADA_COL_TILE),
        in_specs=[
            pl.BlockSpec((bsz, D_MODEL), lambda l, j: (0, 0)),
            pl.BlockSpec((None, D_MODEL, ADA_COL_TILE), lambda l, j: (l, 0, j)),
            pl.BlockSpec((None, 1, ADA_COL_TILE), lambda l, j: (l, 0, j)),
        ],
        out_specs=pl.BlockSpec((None, bsz, ADA_COL_TILE), lambda l, j: (l, 0, j)),
        out_shape=jax.ShapeDtypeStruct((DEPTH, bsz, n), F32),
        compiler_params=_params("arbitrary", "arbitrary"),
        name="ada_proj",
    )(c, w_ada, b_ada.reshape(DEPTH, 1, n))


def _ffn_kernel(h_ref, ada_ref, npre_ref, npost_ref, wup_ref, wdn_ref, o_ref, *, sub):
    h = h_ref[...]
    a0 = 3 * sub
    u = _prenorm_mod(h, npre_ref[sub:sub + 1, :], ada_ref[a0:a0 + 1, :], ada_ref[a0 + 1:a0 + 2, :])
    u = u.astype(BF16)

    def gate_up(j):
        lo = j * FF_CHUNK
        return _dot(u, wup_ref[:, lo:lo + FF_CHUNK]), _dot(u, wup_ref[:, D_FF + lo:D_FF + lo + FF_CHUNK])

    def down(j, gate, up):
        mid = (jax.nn.silu(gate) * up).astype(BF16)
        return _dot(mid, wdn_ref[j * FF_CHUNK:(j + 1) * FF_CHUNK, :])

    acc = None
    pending = []
    for j in range(FF_NCHUNK + FFN_LOOKAHEAD):
        if j < FF_NCHUNK:
            pending.append((j,) + gate_up(j))
        if j >= FFN_LOOKAHEAD:
            d = down(*pending.pop(0))
            acc = d if acc is None else acc + d
    post = _rms(acc) * npost_ref[sub:sub + 1, :]
    o_ref[...] = h + (0.5 * ada_ref[a0 + 2:a0 + 3, :]) * post


def _ffn_call(h, ada4, norm_pre, norm_post, wup, wdn, layer, sub):
    bsz, seq, _ = h.shape
    tm = TOKEN_TILE
    tok = pl.BlockSpec((None, tm, D_MODEL), lambda b, i: (b, i, 0))
    return pl.pallas_call(
        functools.partial(_ffn_kernel, sub=sub),
        grid=(bsz, seq // tm),
        in_specs=[
            tok,
            pl.BlockSpec((None, None, ADA_CHUNKS, D_MODEL), lambda b, i: (layer, b, 0, 0)),
            pl.BlockSpec((None, 3, D_MODEL), lambda b, i: (layer, 0, 0)),
            pl.BlockSpec((None, 3, D_MODEL), lambda b, i: (layer, 0, 0)),
            _resident((None, D_MODEL, 2 * D_FF), lambda b, i: (layer, 0, 0)),
            _resident((None, D_FF, D_MODEL), lambda b, i: (layer, 0, 0)),
        ],
        out_specs=tok,
        out_shape=jax.ShapeDtypeStruct(h.shape, F32),
        compiler_params=_params("arbitrary", "arbitrary"),
        name=f"ffn{sub // 2 + 1}",
    )(h, ada4, norm_pre, norm_post, wup, wdn)


def _log_sigmoid(z):
    return jnp.minimum(z, 0.0) - jnp.log1p(jnp.exp(-jnp.abs(z)))


def _inproj_kernel(h_ref, ada_ref, npre_ref, wgla_ref, wa_ref, walpha_ref, balpha_ref, wdiff_ref, ws5_ref,
                   gq_ref, gk_ref, gv_ref, gr_ref, la_ref, dq_ref, dk_ref, dv_ref, su_ref):
    u = _prenorm_mod(h_ref[...], npre_ref[1:2, :], ada_ref[3:4, :], ada_ref[4:5, :]).astype(BF16)
    pg = _dot(u, wgla_ref[...])
    gq_ref[...] = pg[:, :GLA_QK_W]
    gk_ref[...] = pg[:, GLA_QK_W:2 * GLA_QK_W]
    gv_ref[...] = pg[:, 2 * GLA_QK_W:2 * GLA_QK_W + GLA_V_W].astype(BF16)
    gr_ref[...] = pg[:, 2 * GLA_QK_W + GLA_V_W:]
    a_low = _dot(u, wa_ref[...]).astype(BF16)
    z = _dot(a_low, walpha_ref[...]) + balpha_ref[...]
    la_ref[...] = _log_sigmoid(z) / GLA_TAU
    pd = _dot(u, wdiff_ref[...])
    dq_ref[...] = pd[:, :DIFF_QK_W].astype(BF16)
    dk_ref[...] = pd[:, DIFF_QK_W:2 * DIFF_QK_W].astype(BF16)
    dv_ref[...] = pd[:, 2 * DIFF_QK_W:].astype(BF16)
    su_ref[...] = _dot(u, ws5_ref[...]).astype(BF16)


def _inproj_call(h, ada4, norm_pre, wgla, wa, walpha, balpha, wdiff, ws5, layer):
    bsz, seq, _ = h.shape
    tm = TOKEN_TILE

    def tok(w):
        return pl.BlockSpec((None, tm, w), lambda b, i: (b, i, 0))

    def sds(w, dt):
        return jax.ShapeDtypeStruct((bsz, seq, w), dt)

    def res(arr):
        nd = arr.ndim - 1
        return _resident((None,) + arr.shape[1:], lambda b, i: (layer,) + (0,) * nd)

    return pl.pallas_call(
        _inproj_kernel,
        grid=(bsz, seq // tm),
        in_specs=[
            tok(D_MODEL),
            pl.BlockSpec((None, None, ADA_CHUNKS, D_MODEL), lambda b, i: (layer, b, 0, 0)),
            pl.BlockSpec((None, 3, D_MODEL), lambda b, i: (layer, 0, 0)),
            res(wgla), res(wa), res(walpha), res(balpha), res(wdiff), res(ws5),
        ],
        out_specs=[tok(GLA_QK_W), tok(GLA_QK_W), tok(GLA_V_W), tok(GLA_V_W), tok(GLA_QK_W),
                   tok(DIFF_QK_W), tok(DIFF_QK_W), tok(DIFF_V_W),
                   pl.BlockSpec((tm, S5_WIDTH), lambda b, i: (i, b))],
        out_shape=[sds(GLA_QK_W, F32), sds(GLA_QK_W, F32), sds(GLA_V_W, BF16), sds(GLA_V_W, F32),
                   sds(GLA_QK_W, F32), sds(DIFF_QK_W, BF16), sds(DIFF_QK_W, BF16), sds(DIFF_V_W, BF16),
                   jax.ShapeDtypeStruct((seq, bsz * S5_WIDTH), BF16)],
        compiler_params=_params("arbitrary", "arbitrary"),
        name="mixer_in_proj",
    )(h, ada4, norm_pre, wgla, wa, walpha, balpha, wdiff, ws5)


def _cumsum_rows(tri, x):
    hi = x.astype(BF16)
    r1 = x - hi.astype(F32)
    mid = r1.astype(BF16)
    lo = (r1 - mid.astype(F32)).astype(BF16)
    return _dot(tri, hi) + _dot(tri, mid) + _dot(tri, lo)


def _gla_kernel(q_ref, k_ref, v_ref, r_ref, la_ref, g_ref, o_ref, s_ref, *, n_pairs):
    @pl.when(pl.program_id(1) == 0)
    def _():
        s_ref[...] = jnp.zeros_like(s_ref)

    row = lax.broadcasted_iota(jnp.int32, (GLA_PAIR, GLA_PAIR), 0)
    col = lax.broadcasted_iota(jnp.int32, (GLA_PAIR, GLA_PAIR), 1)
    causal = (col <= row) & ((row // GLA_CHUNK) == (col // GLA_CHUNK))
    tri = jnp.where(causal, 1.0, 0.0).astype(BF16)
    qlane = lax.broadcasted_iota(jnp.int32, (GLA_PAIR, GLA_QK_W), 1)
    tlane = lax.broadcasted_iota(jnp.int32, (GLA_QK_W, GLA_PAIR), 1)
    first = tlane < GLA_CHUNK

    def pair(p, carry):
        rows = pl.ds(pl.multiple_of(p * GLA_PAIR, GLA_PAIR), GLA_PAIR)
        b = _cumsum_rows(tri, la_ref[rows, :])
        b_t = b.T
        qd = (q_ref[rows, :] * (GLA_DK ** -0.5)) * jnp.exp(b)
        k_t = k_ref[rows, :].T
        kd_t = (k_t * jnp.exp(-b_t)).astype(BF16)
        bl0 = b_t[:, GLA_CHUNK - 1:GLA_CHUNK]
        bl1 = b_t[:, GLA_PAIR - 1:GLA_PAIR]
        kdec_t = k_t * jnp.exp(jnp.where(first, bl0, bl1) - b_t)
        kdec0 = jnp.where(first, kdec_t, 0.0).astype(BF16)
        kdec1 = jnp.where(first, 0.0, kdec_t).astype(BF16)
        v = v_ref[rows, :]
        r = r_ref[rows, :]

        s0 = s_ref[...]
        ds0, ds1 = [], []
        for h in range(GLA_HEADS):
            hr = slice(h * GLA_DK, (h + 1) * GLA_DK)
            vh = v[:, h * GLA_DV:(h + 1) * GLA_DV]
            ds0.append(_dot(kdec0[hr, :], vh))
            ds1.append(_dot(kdec1[hr, :], vh))
        s1 = jnp.exp(bl0) * s0 + jnp.concatenate(ds0, axis=0)
        s2 = jnp.exp(bl1) * s1 + jnp.concatenate(ds1, axis=0)
        s_ref[...] = s2
        s0b = s0.astype(BF16)
        s1b = s1.astype(BF16)

        for h in range(GLA_HEADS):
            vc = slice(h * GLA_DV, (h + 1) * GLA_DV)
            qh = jnp.where(qlane // GLA_DK == h, qd, 0.0).astype(BF16)
            att = jnp.where(causal, _dot(qh, kd_t), 0.0).astype(BF16)
            inter = jnp.concatenate([_dot(qh[:GLA_CHUNK], s0b), _dot(qh[GLA_CHUNK:], s1b)], axis=0)
            o = _rms(_dot(att, v[:, vc]) + inter) * g_ref[:, vc]
            o_ref[rows, vc] = (o * jax.nn.silu(r[:, vc])).astype(BF16)
        return carry

    lax.fori_loop(0, n_pairs, pair, 0)


def _gla_call(gq, gk, gv, gr, la, gla_norm, layer):
    bsz, seq, _ = gq.shape
    tl = GLA_TILE

    def tok(w):
        return pl.BlockSpec((None, tl, w), lambda b, i: (b, i, 0))

    return pl.pallas_call(
        functools.partial(_gla_kernel, n_pairs=tl // GLA_PAIR),
        grid=(bsz, seq // tl),
        in_specs=[tok(GLA_QK_W), tok(GLA_QK_W), tok(GLA_V_W), tok(GLA_V_W), tok(GLA_QK_W),
                  pl.BlockSpec((None, 1, GLA_V_W), lambda b, i: (layer, 0, 0))],
        out_specs=tok(GLA_V_W),
        out_shape=jax.ShapeDtypeStruct((bsz, seq, GLA_V_W), BF16),
        scratch_shapes=[pltpu.VMEM((GLA_QK_W, GLA_DV), F32)],
        compiler_params=_params("arbitrary", "arbitrary"),
        name="gla_mixer",
    )(gq, gk, gv, gr, la, gla_norm.reshape(DEPTH, 1, GLA_V_W))


def _t5_bucket_tiles():
    t = ATT_TILE
    r = np.arange(t)[:, None]
    c = np.arange(t)[None, :]
    max_exact = NUM_BUCKETS // 2
    tiles = []
    for off in range(3):
        dist = off * t + r - c
        d = np.maximum(dist, 1).astype(np.float32)
        large = max_exact + (np.log(d / np.float32(max_exact)) / np.float32(math.log(MAX_DISTANCE / max_exact))
                             * np.float32(NUM_BUCKETS - max_exact)).astype(np.int32)
        large = np.minimum(large, NUM_BUCKETS - 1)
        bucket = np.where(dist < max_exact, dist, large)
        tiles.append(np.where(dist >= 0, bucket, -1).T)
    far = tiles[2]
    assert (far == NUM_BUCKETS - 1).all()
    return np.stack(tiles).astype(np.int32)


def _bias_kernel(rel_ref, bucket_ref, o_ref):
    m = pl.program_id(0)
    bucket = bucket_ref[...]
    acc = jnp.full(bucket.shape, -jnp.inf, F32)
    for kk in range(NUM_BUCKETS):
        acc = jnp.where(bucket == kk, rel_ref[kk, m], acc)
    o_ref[...] = acc


def _bias_call(rel_bias):
    n_maps = 2 * DIFF_HEADS
    t = ATT_TILE
    return pl.pallas_call(
        _bias_kernel,
        grid=(n_maps,),
        in_specs=[pl.BlockSpec(memory_space=pltpu.SMEM),
                  pl.BlockSpec((3, t, t), lambda m: (0, 0, 0))],
        out_specs=pl.BlockSpec((None, 3, t, t), lambda m: (m, 0, 0, 0)),
        out_shape=jax.ShapeDtypeStruct((n_maps, 3, t, t), F32),
        compiler_params=_params("arbitrary"),
        name="t5_bias_tiles",
    )(rel_bias, jnp.asarray(_t5_bucket_tiles()))


def _diff_kernel(q_ref, k_ref, v_ref, bias_ref, lam_ref, g_ref, o_ref, *, lam_init, n_tiles):
    t = ATT_TILE
    scale = DIFF_DQK ** -0.5
    lv = lam_ref[...]
    lam = (jnp.exp(jnp.sum(lv[0:1] * lv[1:2], axis=-1, keepdims=True))
           - jnp.exp(jnp.sum(lv[2:3] * lv[3:4], axis=-1, keepdims=True)) + lam_init)
    v_t = jnp.concatenate([v_ref[...].astype(F32).T, jnp.ones((16, v_ref.shape[0]), F32)], axis=0).astype(BF16)
    lane = lax.broadcasted_iota(jnp.int32, (t, 2 * DIFF_DQK), 1)
    far = [bias_ref[mp, 2, 0:1, 0:1] for mp in range(2)]

    tasks = [(qi, kj, mp) for qi in range(n_tiles) for kj in range(qi + 1) for mp in range(2)]
    masked_q = {}
    state = {}
    done = {}

    def logits(qi, kj, mp):
        if qi not in masked_q:
            q = q_ref[qi * t:(qi + 1) * t, :]
            zero = jnp.zeros_like(q)
            masked_q[qi] = (jnp.where(lane < DIFF_DQK, q, zero) * scale,
                            jnp.where(lane < DIFF_DQK, zero, q) * scale)
        s = _dot_nt(k_ref[kj * t:(kj + 1) * t, :], masked_q[qi][mp])
        return s + bias_ref[mp, qi - kj] if qi - kj < 2 else s

    def accumulate(qi, kj, mp, s):
        shift = 0.0 if qi - kj < 2 else far[mp]
        m_blk = jnp.max(s, axis=0, keepdims=True) + shift
        prev = state.get((qi, mp))
        m_new = m_blk if prev is None else jnp.maximum(prev[0], m_blk)
        p = jnp.exp((s - (m_new - shift)).astype(BF16))
        pv = _dot(v_t[:, kj * t:(kj + 1) * t], p)
        if prev is None:
            state[(qi, mp)] = (m_new, pv)
        else:
            state[(qi, mp)] = (m_new, jnp.exp(prev[0] - m_new) * prev[1] + pv)
        if kj == qi:
            _, acc = state.pop((qi, mp))
            done.setdefault(qi, {})[mp] = acc[:DIFF_DV] / acc[DIFF_DV:DIFF_DV + 1]
            if len(done[qi]) == 2:
                o = done[qi][0] - lam * done[qi][1]
                o = o * lax.rsqrt(jnp.mean(o * o, axis=0, keepdims=True) + EPS)
                o_ref[qi * t:(qi + 1) * t, :] = (o.T * g_ref[...] * (1.0 - lam_init)).astype(BF16)
                del done[qi]

    pending = []
    for task in tasks:
        pending.append((task, logits(*task)))
        if len(pending) > ATT_LOOKAHEAD:
            ready, s = pending.pop(0)
            accumulate(*ready, s)
    for ready, s in pending:
        accumulate(*ready, s)


def _diff_call(dq, dk, dv, bias, diff_lambda, diff_norm, layer):
    bsz, seq, _ = dq.shape
    t = ATT_TILE
    lam_init = 0.8 - 0.6 * math.exp(-0.3 * layer)
    return pl.pallas_call(
        functools.partial(_diff_kernel, lam_init=lam_init, n_tiles=seq // t),
        grid=(bsz, DIFF_HEADS),
        in_specs=[
            pl.BlockSpec((None, seq, 2 * DIFF_DQK), lambda b, h: (b, 0, h)),
            pl.BlockSpec((None, seq, 2 * DIFF_DQK), lambda b, h: (b, 0, h)),
            pl.BlockSpec((None, seq, DIFF_DV), lambda b, h: (b, 0, h)),
            pl.BlockSpec((2, 3, t, t), lambda b, h: (h, 0, 0, 0)),
            pl.BlockSpec((None, 4, DIFF_DQK), lambda b, h: (layer, 0, 0)),
            pl.BlockSpec((None, 1, DIFF_DV), lambda b, h: (layer, 0, h)),
        ],
        out_specs=pl.BlockSpec((None, seq, DIFF_DV), lambda b, h: (b, 0, h)),
        out_shape=jax.ShapeDtypeStruct((bsz, seq, DIFF_V_W), BF16),
        compiler_params=_params("arbitrary", "arbitrary"),
        name="diff_attention",
    )(dq, dk, dv, bias, diff_lambda, diff_norm.reshape(DEPTH, 1, DIFF_V_W))


def _s5_prep_kernel(lre_ref, lim_ref, lstep_ref, bre_ref, bim_ref, cre_ref, cim_ref,
                    are_ref, aim_ref, bmat_ref, cmat_ref):
    lr = lre_ref[...]
    li = lim_ref[...]
    dt = jnp.exp(lstep_ref[...])
    mag = jnp.exp(lr * dt)
    a_re = mag * jnp.cos(li * dt)
    a_im = mag * jnp.sin(li * dt)
    are_ref[...] = jnp.broadcast_to(a_re, are_ref.shape)
    aim_ref[...] = jnp.broadcast_to(a_im, aim_ref.shape)
    nr, ni = a_re - 1.0, a_im
    den = lr * lr + li * li
    f_re = (nr * lr + ni * li) / den
    f_im = (ni * lr - nr * li) / den
    ns = S5_BLOCK_STATE
    for gb in range(S5_GROUP_BLOCKS):
        fr = f_re[:, gb * ns:(gb + 1) * ns]
        fi = f_im[:, gb * ns:(gb + 1) * ns]
        br = bre_ref[gb]
        bi = bim_ref[gb]
        bmat_ref[gb, :, :ns] = (fr * br - fi * bi).astype(BF16)
        bmat_ref[gb, :, ns:] = (fr * bi + fi * br).astype(BF16)
        cmat_ref[gb, :ns, :] = cre_ref[gb].astype(BF16)
        cmat_ref[gb, ns:, :] = (-cim_ref[gb]).astype(BF16)


def _s5_prep_call(lam_re, lam_im, log_step, b_re, b_im, c_re, c_im, bsz):
    d = DEPTH
    nb, ch, ns = S5_GROUP_BLOCKS, S5_BLOCK_CH, S5_BLOCK_STATE
    gpb = S5_GROUPS // nb
    eye = jnp.eye(gpb, dtype=F32)

    def in_blocks(b):
        t = b.reshape(d, nb, gpb, S5_STATE, S5_GROUP_CH).transpose(0, 1, 2, 4, 3)
        return (t[:, :, :, :, None, :] * eye[None, None, :, None, :, None]).reshape(d, nb, ch, ns)

    def out_blocks(c):
        t = c.reshape(d, nb, gpb, S5_GROUP_CH, S5_STATE).transpose(0, 1, 4, 2, 3)
        return (t[:, :, None, :, :, :] * eye[None, None, :, None, :, None]).reshape(d, nb, ns, ch)

    def row(x):
        return x.reshape(d, 1, S5_NSTATE)

    lstep = jnp.broadcast_to(log_step[:, :, None], (d, S5_GROUPS, S5_STATE))

    def full(shape):
        nd = len(shape)
        return pl.BlockSpec((None,) + shape, lambda l: (l,) + (0,) * nd)

    return pl.pallas_call(
        _s5_prep_kernel,
        grid=(d,),
        in_specs=[full((1, S5_NSTATE))] * 3 + [full((nb, ch, ns))] * 2 + [full((nb, ns, ch))] * 2,
        out_specs=[full((bsz, S5_NSTATE)), full((bsz, S5_NSTATE)),
                   full((nb, ch, 2 * ns)), full((nb, 2 * ns, ch))],
        out_shape=[jax.ShapeDtypeStruct((d, bsz, S5_NSTATE), F32), jax.ShapeDtypeStruct((d, bsz, S5_NSTATE), F32),
                   jax.ShapeDtypeStruct((d, nb, ch, 2 * ns), BF16), jax.ShapeDtypeStruct((d, nb, 2 * ns, ch), BF16)],
        compiler_params=_params("arbitrary"),
        name="s5_discretize",
    )(row(lam_re), row(lam_im), row(lstep), in_blocks(b_re), in_blocks(b_im), out_blocks(c_re), out_blocks(c_im))


def _s5_kernel(u_ref, are_ref, aim_ref, bmat_ref, cmat_ref, d_ref, wglu_ref, bglu_ref, o_ref,
               xr_ref, xi_ref, sr_ref, si_ref, y_ref, *, bsz, n_steps):
    @pl.when(pl.program_id(0) == 0)
    def _():
        sr_ref[...] = jnp.zeros_like(sr_ref)
        si_ref[...] = jnp.zeros_like(si_ref)

    ns, ch = S5_BLOCK_STATE, S5_BLOCK_CH
    u = u_ref[...]
    for gb in range(S5_GROUP_BLOCKS):
        bu = _dot(u[:, gb * ch:(gb + 1) * ch], bmat_ref[gb])
        xr_ref[:, gb * ns:(gb + 1) * ns] = bu[:, :ns]
        xi_ref[:, gb * ns:(gb + 1) * ns] = bu[:, ns:]

    for gb in range(S5_GROUP_BLOCKS):
        cols = slice(gb * ns, (gb + 1) * ns)
        ar = are_ref[:, cols]
        ai = aim_ref[:, cols]

        def step(t, carry, cols=cols, ar=ar, ai=ai):
            pr, pi = carry
            rows = pl.ds(pl.multiple_of(t * bsz, bsz), bsz)
            nr = ar * pr - ai * pi + xr_ref[rows, cols]
            ni = ar * pi + ai * pr + xi_ref[rows, cols]
            xr_ref[rows, cols] = nr
            xi_ref[rows, cols] = ni
            return nr, ni

        fr, fi = lax.fori_loop(0, n_steps, step, (sr_ref[:, cols], si_ref[:, cols]))
        sr_ref[:, cols] = fr
        si_ref[:, cols] = fi

    for gb in range(S5_GROUP_BLOCKS):
        cols = slice(gb * ns, (gb + 1) * ns)
        y_ref[:, gb * ch:(gb + 1) * ch] = (_dot(xr_ref[:, cols].astype(BF16), cmat_ref[gb, :ns, :])
                                           + _dot(xi_ref[:, cols].astype(BF16), cmat_ref[gb, ns:, :]))
    y = jax.nn.gelu(y_ref[...] + d_ref[...] * u.astype(F32))
    z = _dot(y.astype(BF16), wglu_ref[...]) + bglu_ref[...]
    o_ref[...] = (z[:, :S5_WIDTH] * jax.nn.sigmoid(z[:, S5_WIDTH:])).astype(BF16)


def _s5_call(su_tb, a_re, a_im, bmat, cmat, d_skip, wglu, bglu, layer, bsz):
    rows_total = su_tb.shape[0]
    tr = S5_TIME_TILE * bsz
    nb, ch, ns = S5_GROUP_BLOCKS, S5_BLOCK_CH, S5_BLOCK_STATE

    def res(shape):
        nd = len(shape)
        return _resident((None,) + shape, lambda i: (layer,) + (0,) * nd)

    return pl.pallas_call(
        functools.partial(_s5_kernel, bsz=bsz, n_steps=S5_TIME_TILE),
        grid=(rows_total // tr,),
        in_specs=[pl.BlockSpec((tr, S5_WIDTH), lambda i: (i, 0)),
                  res((bsz, S5_NSTATE)), res((bsz, S5_NSTATE)),
                  res((nb, ch, 2 * ns)), res((nb, 2 * ns, ch)),
                  res((1, S5_WIDTH)), res((S5_WIDTH, 2 * S5_WIDTH)), res((1, 2 * S5_WIDTH))],
        out_specs=pl.BlockSpec((tr, S5_WIDTH), lambda i: (i, 0)),
        out_shape=jax.ShapeDtypeStruct((rows_total, S5_WIDTH), BF16),
        scratch_shapes=[pltpu.VMEM((tr, S5_NSTATE), F32), pltpu.VMEM((tr, S5_NSTATE), F32),
                        pltpu.VMEM((bsz, S5_NSTATE), F32), pltpu.VMEM((bsz, S5_NSTATE), F32),
                        pltpu.VMEM((tr, S5_WIDTH), F32)],
        compiler_params=_params("arbitrary"),
        name="s5_mixer",
    )(su_tb, a_re, a_im, bmat, cmat, d_skip, wglu, bglu)


def _merge_kernel(h_ref, ada_ref, npre_ref, npost_ref, yg_ref, yd_ref, ys_ref, wgate_ref, wbr_ref, wout_ref, o_ref):
    h = h_ref[...]
    u = _prenorm_mod(h, npre_ref[1:2, :], ada_ref[3:4, :], ada_ref[4:5, :]).astype(BF16)
    merged = None
    for i, y_ref in enumerate((yg_ref, yd_ref, ys_ref)):
        gate = jax.nn.sigmoid(_dot(u, wgate_ref[:, i * D_MODEL:(i + 1) * D_MODEL]))
        term = gate * _dot(y_ref[...], wbr_ref[i])
        merged = term if merged is None else merged + term
    y = _dot(merged.astype(BF16), wout_ref[...])
    o_ref[...] = h + ada_ref[5:6, :] * (_rms(y) * npost_ref[1:2, :])


def _merge_call(h, ada4, norm_pre, norm_post, y_gla, y_diff, y_s5_t, wgate, wbr, wout, layer):
    bsz, seq, _ = h.shape
    tm = TOKEN_TILE

    def tok(w):
        return pl.BlockSpec((None, tm, w), lambda b, i: (b, i, 0))

    return pl.pallas_call(
        _merge_kernel,
        grid=(bsz, seq // tm),
        in_specs=[
            tok(D_MODEL),
            pl.BlockSpec((None, None, ADA_CHUNKS, D_MODEL), lambda b, i: (layer, b, 0, 0)),
            pl.BlockSpec((None, 3, D_MODEL), lambda b, i: (layer, 0, 0)),
            pl.BlockSpec((None, 3, D_MODEL), lambda b, i: (layer, 0, 0)),
            tok(GLA_V_W), tok(DIFF_V_W),
            pl.BlockSpec((tm, S5_WIDTH), lambda b, i: (i, b)),
            _resident((None, D_MODEL, N_BRANCH * D_MODEL), lambda b, i: (layer, 0, 0)),
            _resident((None, N_BRANCH, S5_WIDTH, D_MODEL), lambda b, i: (layer, 0, 0, 0)),
            _resident((None, D_MODEL, D_MODEL), lambda b, i: (layer, 0, 0)),
        ],
        out_specs=tok(D_MODEL),
        out_shape=jax.ShapeDtypeStruct(h.shape, F32),
        compiler_params=_params("arbitrary", "arbitrary"),
        name="mixer_merge",
    )(h, ada4, norm_pre, norm_post, y_gla, y_diff, y_s5_t, wgate, wbr, wout)


def kernel(x, c, rel_bias, w_ada, b_ada, norm_pre, norm_post, ffn1_w_up, ffn1_w_down, ffn2_w_up, ffn2_w_down,
           w_in, gla_w_alpha, gla_b_alpha, gla_norm, diff_lambda, diff_norm, s5_lam_re, s5_lam_im, s5_log_step,
           s5_b_re, s5_b_im, s5_c_re, s5_c_im, s5_d, s5_w_glu, s5_b_glu, w_branch, w_out):
    bsz, seq, _ = x.shape
    assert bsz == SUBLANES, "the S5 scan keeps the batch on the sublane axis"

    f1 = (ffn1_w_up.astype(BF16), ffn1_w_down.astype(BF16))
    f2 = (ffn2_w_up.astype(BF16), ffn2_w_down.astype(BF16))
    o_alpha = 2 * GLA_QK_W + 2 * GLA_V_W
    o_diff = o_alpha + GLA_LOWRANK
    o_s5 = o_diff + 2 * DIFF_QK_W + DIFF_V_W
    o_gate = o_s5 + S5_WIDTH
    w_in_b = w_in.astype(BF16)
    wgla = w_in_b[:, :, :o_alpha]
    wa = jnp.pad(w_in_b[:, :, o_alpha:o_diff], ((0, 0), (0, 0), (0, LANES - GLA_LOWRANK)))
    walpha = jnp.pad(gla_w_alpha.astype(BF16), ((0, 0), (0, LANES - GLA_LOWRANK), (0, 0)))
    balpha = gla_b_alpha.reshape(DEPTH, 1, GLA_QK_W)
    wdiff = w_in_b[:, :, o_diff:o_s5]
    ws5 = w_in_b[:, :, o_s5:o_gate]
    wgate = w_in_b[:, :, o_gate:]
    wbr = w_branch.astype(BF16)
    wout = w_out.astype(BF16)
    wglu = s5_w_glu.astype(BF16)
    bglu = s5_b_glu.reshape(DEPTH, 1, 2 * S5_WIDTH)
    d_skip = s5_d.reshape(DEPTH, 1, S5_WIDTH)

    ada4 = _ada_call(c, w_ada, b_ada).reshape(DEPTH, bsz, ADA_CHUNKS, D_MODEL)
    bias = _bias_call(rel_bias)
    a_re, a_im, bmat, cmat = _s5_prep_call(s5_lam_re, s5_lam_im, s5_log_step, s5_b_re, s5_b_im,
                                           s5_c_re, s5_c_im, bsz)

    h = x
    for layer in range(DEPTH):
        h = _ffn_call(h, ada4, norm_pre, norm_post, *f1, layer, 0)
        gq, gk, gv, gr, la, dq, dk, dv, su = _inproj_call(h, ada4, norm_pre, wgla, wa, walpha, balpha,
                                                          wdiff, ws5, layer)
        y_gla = _gla_call(gq, gk, gv, gr, la, gla_norm, layer)
        y_diff = _diff_call(dq, dk, dv, bias, diff_lambda, diff_norm, layer)
        y_s5 = _s5_call(su.reshape(seq * bsz, S5_WIDTH), a_re, a_im, bmat, cmat, d_skip, wglu, bglu, layer, bsz)
        h = _merge_call(h, ada4, norm_pre, norm_post, y_gla, y_diff, y_s5.reshape(seq, bsz * S5_WIDTH),
                        wgate, wbr, wout, layer)
        h = _ffn_call(h, ada4, norm_pre, norm_post, *f2, layer, 2)
    return h
```

```python
import functools
import math

import numpy as np
import jax
import jax.numpy as jnp
from jax import lax
from jax.experimental import pallas as pl
from jax.experimental.pallas import tpu as pltpu

F32 = jnp.float32
BF16 = jnp.bfloat16

D_MODEL = 1024
DEPTH = 2
D_FF = 2816
ADA_CHUNKS = 9
EPS = 1e-6

GLA_HEADS = 4
GLA_DK = 64
GLA_DV = 128
GLA_LOWRANK = 16
GLA_TAU = 16.0
GLA_CHUNK = 64
GLA_QK_W = GLA_HEADS * GLA_DK
GLA_V_W = GLA_HEADS * GLA_DV

DIFF_HEADS = 4
DIFF_DQK = 64
DIFF_DV = 128
DIFF_QK_W = DIFF_HEADS * 2 * DIFF_DQK
DIFF_V_W = DIFF_HEADS * DIFF_DV
NUM_BUCKETS = 32
MAX_DISTANCE = 128

S5_GROUPS = 32
S5_GROUP_CH = 16
S5_STATE = 64
S5_WIDTH = S5_GROUPS * S5_GROUP_CH
S5_NSTATE = S5_GROUPS * S5_STATE
S5_GROUP_BLOCKS = 4
S5_BLOCK_CH = S5_WIDTH // S5_GROUP_BLOCKS
S5_BLOCK_STATE = S5_NSTATE // S5_GROUP_BLOCKS

N_BRANCH = 3

SUBLANES = 8
LANES = 128
VMEM_LIMIT_BYTES = 56 * 1024 * 1024

TOKEN_TILE = 512
FF_CHUNK = 256
FF_NCHUNK = D_FF // FF_CHUNK
FFN_LOOKAHEAD = 1
GLA_PAIR = 2 * GLA_CHUNK
GLA_TILE = 512
ATT_TILE = 256
ATT_LOOKAHEAD = 4
S5_TIME_TILE = 32
ADA_COL_TILE = 1536


def _dot(a, b):
    return jnp.dot(a, b, preferred_element_type=F32)


def _dot_nt(a, b):
    return lax.dot_general(a, b, (((1,), (1,)), ((), ())), preferred_element_type=F32)


def _rms(x):
    return x * lax.rsqrt(jnp.mean(x * x, axis=-1, keepdims=True) + EPS)


def _prenorm_mod(h, g, shift, scale):
    return (_rms(h) * g) * (1.0 + scale) + shift


def _params(*sem):
    return pltpu.CompilerParams(dimension_semantics=sem, vmem_limit_bytes=VMEM_LIMIT_BYTES)


def _resident(shape, index_map):
    return pl.BlockSpec(shape, index_map, pipeline_mode=pl.Buffered(1))


def _ada_kernel(c_ref, w_ref, b_ref, o_ref):
    cond = jax.nn.silu(c_ref[...]).astype(BF16)
    o_ref[...] = _dot(cond, w_ref[...].astype(BF16)) + b_ref[...]


def _ada_call(c, w_ada, b_ada):
    bsz = c.shape[0]
    n = ADA_CHUNKS * D_MODEL
    return pl.pallas_call(
        _ada_kernel,
        grid=(DEPTH, n // ADA_COL_TILE),
        in_specs=[
            pl.BlockSpec((bsz, D_MODEL), lambda l, j: (0, 0)),
            pl.BlockSpec((None, D_MODEL, ADA_COL_TILE), lambda l, j: (l, 0, j)),
            pl.BlockSpec((None, 1, ADA_COL_TILE), lambda l, j: (l, 0, j)),
        ],
        out_specs=pl.BlockSpec((None, bsz, ADA_COL_TILE), lambda l, j: (l, 0, j)),
        out_shape=jax.ShapeDtypeStruct((DEPTH, bsz, n), F32),
        compiler_params=_params("arbitrary", "arbitrary"),
        name="ada_proj",
    )(c, w_ada, b_ada.reshape(DEPTH, 1, n))


def _ffn_kernel(h_ref, ada_ref, npre_ref, npost_ref, wup_ref, wdn_ref, o_ref, *, sub):
    h = h_ref[...]
    a0 = 3 * sub
    u = _prenorm_mod(h, npre_ref[sub:sub + 1, :], ada_ref[a0:a0 + 1, :], ada_ref[a0 + 1:a0 + 2, :])
    u = u.astype(BF16)

    def gate_up(j):
        lo = j * FF_CHUNK
        return _dot(u, wup_ref[:, lo:lo + FF_CHUNK]), _dot(u, wup_ref[:, D_FF + lo:D_FF + lo + FF_CHUNK])

    def down(j, gate, up):
        mid = (jax.nn.silu(gate) * up).astype(BF16)
        return _dot(mid, wdn_ref[j * FF_CHUNK:(j + 1) * FF_CHUNK, :])

    acc = None
    pending = []
    for j in range(FF_NCHUNK + FFN_LOOKAHEAD):
        if j < FF_NCHUNK:
            pending.append((j,) + gate_up(j))
        if j >= FFN_LOOKAHEAD:
            d = down(*pending.pop(0))
            acc = d if acc is None else acc + d
    post = _rms(acc) * npost_ref[sub:sub + 1, :]
    o_ref[...] = h + (0.5 * ada_ref[a0 + 2:a0 + 3, :]) * post


def _ffn_call(h, ada4, norm_pre, norm_post, wup, wdn, layer, sub):
    bsz, seq, _ = h.shape
    tm = TOKEN_TILE
    tok = pl.BlockSpec((None, tm, D_MODEL), lambda b, i: (b, i, 0))
    return pl.pallas_call(
        functools.partial(_ffn_kernel, sub=sub),
        grid=(bsz, seq // tm),
        in_specs=[
            tok,
            pl.BlockSpec((None, None, ADA_CHUNKS, D_MODEL), lambda b, i: (layer, b, 0, 0)),
            pl.BlockSpec((None, 3, D_MODEL), lambda b, i: (layer, 0, 0)),
            pl.BlockSpec((None, 3, D_MODEL), lambda b, i: (layer, 0, 0)),
            _resident((None, D_MODEL, 2 * D_FF), lambda b, i: (layer, 0, 0)),
            _resident((None, D_FF, D_MODEL), lambda b, i: (layer, 0, 0)),
        ],
        out_specs=tok,
        out_shape=jax.ShapeDtypeStruct(h.shape, F32),
        compiler_params=_params("arbitrary", "arbitrary"),
        name=f"ffn{sub // 2 + 1}",
    )(h, ada4, norm_pre, norm_post, wup, wdn)


def _log_sigmoid(z):
    return jnp.minimum(z, 0.0) - jnp.log1p(jnp.exp(-jnp.abs(z)))


def _inproj_kernel(h_ref, ada_ref, npre_ref, wgla_ref, wa_ref, walpha_ref, balpha_ref, wdiff_ref, ws5_ref,
                   gq_ref, gk_ref, gv_ref, gr_ref, la_ref, dq_ref, dk_ref, dv_ref, su_ref):
    u = _prenorm_mod(h_ref[...], npre_ref[1:2, :], ada_ref[3:4, :], ada_ref[4:5, :]).astype(BF16)
    pg = _dot(u, wgla_ref[...])
    gq_ref[...] = pg[:, :GLA_QK_W]
    gk_ref[...] = pg[:, GLA_QK_W:2 * GLA_QK_W]
    gv_ref[...] = pg[:, 2 * GLA_QK_W:2 * GLA_QK_W + GLA_V_W].astype(BF16)
    gr_ref[...] = pg[:, 2 * GLA_QK_W + GLA_V_W:]
    a_low = _dot(u, wa_ref[...]).astype(BF16)
    z = _dot(a_low, walpha_ref[...]) + balpha_ref[...]
    la_ref[...] = _log_sigmoid(z) / GLA_TAU
    pd = _dot(u, wdiff_ref[...])
    dq_ref[...] = pd[:, :DIFF_QK_W].astype(BF16)
    dk_ref[...] = pd[:, DIFF_QK_W:2 * DIFF_QK_W].astype(BF16)
    dv_ref[...] = pd[:, 2 * DIFF_QK_W:].astype(BF16)
    su_ref[...] = _dot(u, ws5_ref[...]).astype(BF16)


def _inproj_call(h, ada4, norm_pre, wgla, wa, walpha, balpha, wdiff, ws5, layer):
    bsz, seq, _ = h.shape
    tm = TOKEN_TILE

    def tok(w):
        return pl.BlockSpec((None, tm, w), lambda b, i: (b, i, 0))

    def sds(w, dt):
        return jax.ShapeDtypeStruct((bsz, seq, w), dt)

    def res(arr):
        nd = arr.ndim - 1
        return _resident((None,) + arr.shape[1:], lambda b, i: (layer,) + (0,) * nd)

    return pl.pallas_call(
        _inproj_kernel,
        grid=(bsz, seq // tm),
        in_specs=[
            tok(D_MODEL),
            pl.BlockSpec((None, None, ADA_CHUNKS, D_MODEL), lambda b, i: (layer, b, 0, 0)),
            pl.BlockSpec((None, 3, D_MODEL), lambda b, i: (layer, 0, 0)),
            res(wgla), res(wa), res(walpha), res(balpha), res(wdiff), res(ws5),
        ],
        out_specs=[tok(GLA_QK_W), tok(GLA_QK_W), tok(GLA_V_W), tok(GLA_V_W), tok(GLA_QK_W),
                   tok(DIFF_QK_W), tok(DIFF_QK_W), tok(DIFF_V_W), tok(S5_WIDTH)],
        out_shape=[sds(GLA_QK_W, F32), sds(GLA_QK_W, F32), sds(GLA_V_W, BF16), sds(GLA_V_W, F32),
                   sds(GLA_QK_W, F32), sds(DIFF_QK_W, BF16), sds(DIFF_QK_W, BF16), sds(DIFF_V_W, BF16),
                   sds(S5_WIDTH, BF16)],
        compiler_params=_params("arbitrary", "arbitrary"),
        name="mixer_in_proj",
    )(h, ada4, norm_pre, wgla, wa, walpha, balpha, wdiff, ws5)


def _cumsum_rows(tri, x):
    hi = x.astype(BF16)
    r1 = x - hi.astype(F32)
    mid = r1.astype(BF16)
    lo = (r1 - mid.astype(F32)).astype(BF16)
    return _dot(tri, hi) + _dot(tri, mid) + _dot(tri, lo)


def _gla_kernel(q_ref, k_ref, v_ref, r_ref, la_ref, g_ref, o_ref, s_ref, *, n_pairs):
    @pl.when(pl.program_id(1) == 0)
    def _():
        s_ref[...] = jnp.zeros_like(s_ref)

    row = lax.broadcasted_iota(jnp.int32, (GLA_PAIR, GLA_PAIR), 0)
    col = lax.broadcasted_iota(jnp.int32, (GLA_PAIR, GLA_PAIR), 1)
    causal = (col <= row) & ((row // GLA_CHUNK) == (col // GLA_CHUNK))
    tri = jnp.where(causal, 1.0, 0.0).astype(BF16)
    qlane = lax.broadcasted_iota(jnp.int32, (GLA_PAIR, GLA_QK_W), 1)
    tlane = lax.broadcasted_iota(jnp.int32, (GLA_QK_W, GLA_PAIR), 1)
    first = tlane < GLA_CHUNK

    def decays(p):
        rows = slice(p * GLA_PAIR, (p + 1) * GLA_PAIR)
        b = _cumsum_rows(tri, la_ref[rows, :])
        b_t = b.T
        qd = (q_ref[rows, :] * (GLA_DK ** -0.5)) * jnp.exp(b)
        k_t = k_ref[rows, :].T
        bl0 = b_t[:, GLA_CHUNK - 1:GLA_CHUNK]
        bl1 = b_t[:, GLA_PAIR - 1:GLA_PAIR]
        kdec_t = k_t * jnp.exp(jnp.where(first, bl0, bl1) - b_t)
        return dict(
            qh=[jnp.where(qlane // GLA_DK == h, qd, 0.0).astype(BF16) for h in range(GLA_HEADS)],
            kd_t=(k_t * jnp.exp(-b_t)).astype(BF16),
            kdec0=jnp.where(first, kdec_t, 0.0).astype(BF16),
            kdec1=jnp.where(first, 0.0, kdec_t).astype(BF16),
            dec0=jnp.exp(bl0), dec1=jnp.exp(bl1))

    def products(p, d, s0):
        v = v_ref[p * GLA_PAIR:(p + 1) * GLA_PAIR, :]
        ds0, ds1 = [], []
        for h in range(GLA_HEADS):
            hr = slice(h * GLA_DK, (h + 1) * GLA_DK)
            vh = v[:, h * GLA_DV:(h + 1) * GLA_DV]
            ds0.append(_dot(d["kdec0"][hr, :], vh))
            ds1.append(_dot(d["kdec1"][hr, :], vh))
        att = [jnp.where(causal, _dot(qh, d["kd_t"]), 0.0).astype(BF16) for qh in d["qh"]]
        s1 = d["dec0"] * s0 + jnp.concatenate(ds0, axis=0)
        s2 = d["dec1"] * s1 + jnp.concatenate(ds1, axis=0)
        s0b = s0.astype(BF16)
        s1b = s1.astype(BF16)
        inter = [jnp.concatenate([_dot(qh[:GLA_CHUNK], s0b), _dot(qh[GLA_CHUNK:], s1b)], axis=0)
                 for qh in d["qh"]]
        return s2, (v, att, inter)

    def outputs(p, v, att, inter):
        rows = slice(p * GLA_PAIR, (p + 1) * GLA_PAIR)
        r = r_ref[rows, :]
        for h in range(GLA_HEADS):
            vc = slice(h * GLA_DV, (h + 1) * GLA_DV)
            o = _rms(_dot(att[h], v[:, vc]) + inter[h]) * g_ref[:, vc]
            o_ref[rows, vc] = (o * jax.nn.silu(r[:, vc])).astype(BF16)

    state = s_ref[...]
    dec, prod = {}, {}
    for i in range(n_pairs + 2):
        if i < n_pairs:
            dec[i] = decays(i)
        if 0 <= i - 1 < n_pairs:
            state, prod[i - 1] = products(i - 1, dec.pop(i - 1), state)
        if 0 <= i - 2 < n_pairs:
            outputs(i - 2, *prod.pop(i - 2))
    s_ref[...] = state


def _gla_call(gq, gk, gv, gr, la, gla_norm, layer):
    bsz, seq, _ = gq.shape
    tl = GLA_TILE

    def tok(w):
        return pl.BlockSpec((None, tl, w), lambda b, i: (b, i, 0))

    return pl.pallas_call(
        functools.partial(_gla_kernel, n_pairs=tl // GLA_PAIR),
        grid=(bsz, seq // tl),
        in_specs=[tok(GLA_QK_W), tok(GLA_QK_W), tok(GLA_V_W), tok(GLA_V_W), tok(GLA_QK_W),
                  pl.BlockSpec((None, 1, GLA_V_W), lambda b, i: (layer, 0, 0))],
        out_specs=tok(GLA_V_W),
        out_shape=jax.ShapeDtypeStruct((bsz, seq, GLA_V_W), BF16),
        scratch_shapes=[pltpu.VMEM((GLA_QK_W, GLA_DV), F32)],
        compiler_params=_params("arbitrary", "arbitrary"),
        name="gla_mixer",
    )(gq, gk, gv, gr, la, gla_norm.reshape(DEPTH, 1, GLA_V_W))


def _t5_bucket_tiles():
    t = ATT_TILE
    r = np.arange(t)[:, None]
    c = np.arange(t)[None, :]
    max_exact = NUM_BUCKETS // 2
    tiles = []
    for off in range(3):
        dist = off * t + r - c
        d = np.maximum(dist, 1).astype(np.float32)
        large = max_exact + (np.log(d / np.float32(max_exact)) / np.float32(math.log(MAX_DISTANCE / max_exact))
                             * np.float32(NUM_BUCKETS - max_exact)).astype(np.int32)
        large = np.minimum(large, NUM_BUCKETS - 1)
        bucket = np.where(dist < max_exact, dist, large)
        tiles.append(np.where(dist >= 0, bucket, -1).T)
    far = tiles[2]
    assert (far == NUM_BUCKETS - 1).all()
    return np.stack(tiles).astype(np.int32)


def _bias_kernel(rel_ref, bucket_ref, o_ref):
    m = pl.program_id(0)
    bucket = bucket_ref[...]
    acc = jnp.full(bucket.shape, -jnp.inf, F32)
    for kk in range(NUM_BUCKETS):
        acc = jnp.where(bucket == kk, rel_ref[kk, m], acc)
    o_ref[...] = acc


def _bias_call(rel_bias):
    n_maps = 2 * DIFF_HEADS
    t = ATT_TILE
    return pl.pallas_call(
        _bias_kernel,
        grid=(n_maps,),
        in_specs=[pl.BlockSpec(memory_space=pltpu.SMEM),
                  pl.BlockSpec((3, t, t), lambda m: (0, 0, 0))],
        out_specs=pl.BlockSpec((None, 3, t, t), lambda m: (m, 0, 0, 0)),
        out_shape=jax.ShapeDtypeStruct((n_maps, 3, t, t), F32),
        compiler_params=_params("arbitrary"),
        name="t5_bias_tiles",
    )(rel_bias, jnp.asarray(_t5_bucket_tiles()))


def _diff_kernel(q_ref, k_ref, v_ref, bias_ref, lam_ref, g_ref, o_ref, *, lam_init, n_tiles):
    t = ATT_TILE
    scale = DIFF_DQK ** -0.5
    lv = lam_ref[...]
    lam = (jnp.exp(jnp.sum(lv[0:1] * lv[1:2], axis=-1, keepdims=True))
           - jnp.exp(jnp.sum(lv[2:3] * lv[3:4], axis=-1, keepdims=True)) + lam_init)
    v_t = jnp.concatenate([v_ref[...].astype(F32).T, jnp.ones((16, v_ref.shape[0]), F32)], axis=0).astype(BF16)
    lane = lax.broadcasted_iota(jnp.int32, (t, 2 * DIFF_DQK), 1)
    far = [bias_ref[mp, 2, 0:1, 0:1] for mp in range(2)]

    tasks = [(qi, kj, mp) for qi in range(n_tiles) for kj in range(qi + 1) for mp in range(2)]
    masked_q = {}
    state = {}
    done = {}

    def logits(qi, kj, mp):
        if qi not in masked_q:
            q = q_ref[qi * t:(qi + 1) * t, :]
            zero = jnp.zeros_like(q)
            masked_q[qi] = (jnp.where(lane < DIFF_DQK, q, zero) * scale,
                            jnp.where(lane < DIFF_DQK, zero, q) * scale)
        s = _dot_nt(k_ref[kj * t:(kj + 1) * t, :], masked_q[qi][mp])
        return s + bias_ref[mp, qi - kj] if qi - kj < 2 else s

    def accumulate(qi, kj, mp, s):
        shift = 0.0 if qi - kj < 2 else far[mp]
        m_blk = jnp.max(s, axis=0, keepdims=True) + shift
        prev = state.get((qi, mp))
        m_new = m_blk if prev is None else jnp.maximum(prev[0], m_blk)
        p = jnp.exp((s - (m_new - shift)).astype(BF16))
        pv = _dot(v_t[:, kj * t:(kj + 1) * t], p)
        if prev is None:
            state[(qi, mp)] = (m_new, pv)
        else:
            state[(qi, mp)] = (m_new, jnp.exp(prev[0] - m_new) * prev[1] + pv)
        if kj == qi:
            _, acc = state.pop((qi, mp))
            done.setdefault(qi, {})[mp] = acc[:DIFF_DV] / acc[DIFF_DV:DIFF_DV + 1]
            if len(done[qi]) == 2:
                o = done[qi][0] - lam * done[qi][1]
                o = o * lax.rsqrt(jnp.mean(o * o, axis=0, keepdims=True) + EPS)
                o_ref[qi * t:(qi + 1) * t, :] = (o.T * g_ref[...] * (1.0 - lam_init)).astype(BF16)
                del done[qi]

    pending = []
    for task in tasks:
        pending.append((task, logits(*task)))
        if len(pending) > ATT_LOOKAHEAD:
            ready, s = pending.pop(0)
            accumulate(*ready, s)
    for ready, s in pending:
        accumulate(*ready, s)


def _diff_call(dq, dk, dv, bias, diff_lambda, diff_norm, layer):
    bsz, seq, _ = dq.shape
    t = ATT_TILE
    lam_init = 0.8 - 0.6 * math.exp(-0.3 * layer)
    return pl.pallas_call(
        functools.partial(_diff_kernel, lam_init=lam_init, n_tiles=seq // t),
        grid=(bsz, DIFF_HEADS),
        in_specs=[
            pl.BlockSpec((None, seq, 2 * DIFF_DQK), lambda b, h: (b, 0, h)),
            pl.BlockSpec((None, seq, 2 * DIFF_DQK), lambda b, h: (b, 0, h)),
            pl.BlockSpec((None, seq, DIFF_DV), lambda b, h: (b, 0, h)),
            pl.BlockSpec((2, 3, t, t), lambda b, h: (h, 0, 0, 0)),
            pl.BlockSpec((None, 4, DIFF_DQK), lambda b, h: (layer, 0, 0)),
            pl.BlockSpec((None, 1, DIFF_DV), lambda b, h: (layer, 0, h)),
        ],
        out_specs=pl.BlockSpec((None, seq, DIFF_DV), lambda b, h: (b, 0, h)),
        out_shape=jax.ShapeDtypeStruct((bsz, seq, DIFF_V_W), BF16),
        compiler_params=_params("arbitrary", "arbitrary"),
        name="diff_attention",
    )(dq, dk, dv, bias, diff_lambda, diff_norm.reshape(DEPTH, 1, DIFF_V_W))


def _s5_prep_kernel(lre_ref, lim_ref, lstep_ref, bre_ref, bim_ref, cre_ref, cim_ref,
                    are_ref, aim_ref, bmat_ref, cmat_ref):
    lr = lre_ref[...]
    li = lim_ref[...]
    dt = jnp.exp(lstep_ref[...])
    mag = jnp.exp(lr * dt)
    a_re = mag * jnp.cos(li * dt)
    a_im = mag * jnp.sin(li * dt)
    are_ref[...] = jnp.broadcast_to(a_re, are_ref.shape)
    aim_ref[...] = jnp.broadcast_to(a_im, aim_ref.shape)
    nr, ni = a_re - 1.0, a_im
    den = lr * lr + li * li
    f_re = (nr * lr + ni * li) / den
    f_im = (ni * lr - nr * li) / den
    ns = S5_BLOCK_STATE
    for gb in range(S5_GROUP_BLOCKS):
        fr = f_re[:, gb * ns:(gb + 1) * ns]
        fi = f_im[:, gb * ns:(gb + 1) * ns]
        br = bre_ref[gb]
        bi = bim_ref[gb]
        bmat_ref[gb, :, :ns] = (fr * br - fi * bi).astype(BF16)
        bmat_ref[gb, :, ns:] = (fr * bi + fi * br).astype(BF16)
        cmat_ref[gb, :ns, :] = cre_ref[gb].astype(BF16)
        cmat_ref[gb, ns:, :] = (-cim_ref[gb]).astype(BF16)


def _s5_prep_call(lam_re, lam_im, log_step, b_re, b_im, c_re, c_im, bsz):
    d = DEPTH
    nb, ch, ns = S5_GROUP_BLOCKS, S5_BLOCK_CH, S5_BLOCK_STATE
    gpb = S5_GROUPS // nb
    eye = jnp.eye(gpb, dtype=F32)

    def in_blocks(b):
        t = b.reshape(d, nb, gpb, S5_STATE, S5_GROUP_CH).transpose(0, 1, 2, 4, 3)
        return (t[:, :, :, :, None, :] * eye[None, None, :, None, :, None]).reshape(d, nb, ch, ns)

    def out_blocks(c):
        t = c.reshape(d, nb, gpb, S5_GROUP_CH, S5_STATE).transpose(0, 1, 4, 2, 3)
        return (t[:, :, None, :, :, :] * eye[None, None, :, None, :, None]).reshape(d, nb, ns, ch)

    def row(x):
        return x.reshape(d, 1, S5_NSTATE)

    lstep = jnp.broadcast_to(log_step[:, :, None], (d, S5_GROUPS, S5_STATE))

    def full(shape):
        nd = len(shape)
        return pl.BlockSpec((None,) + shape, lambda l: (l,) + (0,) * nd)

    return pl.pallas_call(
        _s5_prep_kernel,
        grid=(d,),
        in_specs=[full((1, S5_NSTATE))] * 3 + [full((nb, ch, ns))] * 2 + [full((nb, ns, ch))] * 2,
        out_specs=[full((bsz, S5_NSTATE)), full((bsz, S5_NSTATE)),
                   full((nb, ch, 2 * ns)), full((nb, 2 * ns, ch))],
        out_shape=[jax.ShapeDtypeStruct((d, bsz, S5_NSTATE), F32), jax.ShapeDtypeStruct((d, bsz, S5_NSTATE), F32),
                   jax.ShapeDtypeStruct((d, nb, ch, 2 * ns), BF16), jax.ShapeDtypeStruct((d, nb, 2 * ns, ch), BF16)],
        compiler_params=_params("arbitrary"),
        name="s5_discretize",
    )(row(lam_re), row(lam_im), row(lstep), in_blocks(b_re), in_blocks(b_im), out_blocks(c_re), out_blocks(c_im))


def _s5_kernel(u_ref, are_ref, aim_ref, bmat_ref, cmat_ref, d_ref, wglu_ref, bglu_ref, o_ref,
               tb_ref, xr_ref, xi_ref, sr_ref, si_ref, *, bsz, n_steps):
    @pl.when(pl.program_id(0) == 0)
    def _():
        sr_ref[...] = jnp.zeros_like(sr_ref)
        si_ref[...] = jnp.zeros_like(si_ref)

    ns, ch, nb = S5_BLOCK_STATE, S5_BLOCK_CH, S5_GROUP_BLOCKS
    for b in range(bsz):
        ub = u_ref[b].astype(F32)
        for gb in range(nb):
            tb_ref[gb, pl.ds(b, n_steps, stride=bsz), :] = ub[:, gb * ch:(gb + 1) * ch]
    u = [tb_ref[gb] for gb in range(nb)]

    def drive(gb):
        return _dot(u[gb].astype(BF16), bmat_ref[gb])

    def scan(gb, bu):
        cols = slice(gb * ns, (gb + 1) * ns)
        ar, ai = are_ref[:, cols], aim_ref[:, cols]
        pr, pi = sr_ref[:, cols], si_ref[:, cols]
        for t in range(n_steps):
            rows = slice(t * bsz, (t + 1) * bsz)
            pr, pi = ar * pr - ai * pi + bu[rows, :ns], ar * pi + ai * pr + bu[rows, ns:]
            xr_ref[rows, cols] = pr
            xi_ref[rows, cols] = pi
        sr_ref[:, cols] = pr
        si_ref[:, cols] = pi

    def readout(gb):
        cols = slice(gb * ns, (gb + 1) * ns)
        return (_dot(xr_ref[:, cols].astype(BF16), cmat_ref[gb, :ns, :])
                + _dot(xi_ref[:, cols].astype(BF16), cmat_ref[gb, ns:, :]))

    bu, ys = {0: drive(0)}, []
    for gb in range(nb):
        if gb + 1 < nb:
            bu[gb + 1] = drive(gb + 1)
        scan(gb, bu.pop(gb))
        if gb >= 1:
            ys.append(readout(gb - 1))
    ys.append(readout(nb - 1))

    y = jax.nn.gelu(jnp.concatenate(ys, axis=-1) + d_ref[...] * jnp.concatenate(u, axis=-1))
    z = _dot(y.astype(BF16), wglu_ref[...]) + bglu_ref[...]
    out = z[:, :S5_WIDTH] * jax.nn.sigmoid(z[:, S5_WIDTH:])
    for gb in range(nb):
        tb_ref[gb] = out[:, gb * ch:(gb + 1) * ch]
    for b in range(bsz):
        for gb in range(nb):
            o_ref[b, :, gb * ch:(gb + 1) * ch] = tb_ref[gb, pl.ds(b, n_steps, stride=bsz), :].astype(BF16)


def _s5_call(su, a_re, a_im, bmat, cmat, d_skip, wglu, bglu, layer):
    bsz, seq, _ = su.shape
    tl = S5_TIME_TILE
    tr = tl * bsz
    nb, ch, ns = S5_GROUP_BLOCKS, S5_BLOCK_CH, S5_BLOCK_STATE

    def res(shape):
        nd = len(shape)
        return _resident((None,) + shape, lambda i: (layer,) + (0,) * nd)

    tok = pl.BlockSpec((bsz, tl, S5_WIDTH), lambda i: (0, i, 0))
    return pl.pallas_call(
        functools.partial(_s5_kernel, bsz=bsz, n_steps=tl),
        grid=(seq // tl,),
        in_specs=[tok,
                  res((bsz, S5_NSTATE)), res((bsz, S5_NSTATE)),
                  res((nb, ch, 2 * ns)), res((nb, 2 * ns, ch)),
                  res((1, S5_WIDTH)), res((S5_WIDTH, 2 * S5_WIDTH)), res((1, 2 * S5_WIDTH))],
        out_specs=tok,
        out_shape=jax.ShapeDtypeStruct((bsz, seq, S5_WIDTH), BF16),
        scratch_shapes=[pltpu.VMEM((nb, tr, ch), F32),
                        pltpu.VMEM((tr, S5_NSTATE), F32), pltpu.VMEM((tr, S5_NSTATE), F32),
                        pltpu.VMEM((bsz, S5_NSTATE), F32), pltpu.VMEM((bsz, S5_NSTATE), F32)],
        compiler_params=_params("arbitrary"),
        name="s5_mixer",
    )(su, a_re, a_im, bmat, cmat, d_skip, wglu, bglu)


def _merge_kernel(h_ref, ada_ref, npre_ref, npost_ref, yg_ref, yd_ref, ys_ref, wgate_ref, wbr_ref, wout_ref, o_ref):
    h = h_ref[...]
    u = _prenorm_mod(h, npre_ref[1:2, :], ada_ref[3:4, :], ada_ref[4:5, :]).astype(BF16)
    merged = None
    for i, y_ref in enumerate((yg_ref, yd_ref, ys_ref)):
        gate = jax.nn.sigmoid(_dot(u, wgate_ref[:, i * D_MODEL:(i + 1) * D_MODEL]))
        term = gate * _dot(y_ref[...], wbr_ref[i])
        merged = term if merged is None else merged + term
    y = _dot(merged.astype(BF16), wout_ref[...])
    o_ref[...] = h + ada_ref[5:6, :] * (_rms(y) * npost_ref[1:2, :])


def _merge_call(h, ada4, norm_pre, norm_post, y_gla, y_diff, y_s5, wgate, wbr, wout, layer):
    bsz, seq, _ = h.shape
    tm = TOKEN_TILE

    def tok(w):
        return pl.BlockSpec((None, tm, w), lambda b, i: (b, i, 0))

    return pl.pallas_call(
        _merge_kernel,
        grid=(bsz, seq // tm),
        in_specs=[
            tok(D_MODEL),
            pl.BlockSpec((None, None, ADA_CHUNKS, D_MODEL), lambda b, i: (layer, b, 0, 0)),
            pl.BlockSpec((None, 3, D_MODEL), lambda b, i: (layer, 0, 0)),
            pl.BlockSpec((None, 3, D_MODEL), lambda b, i: (layer, 0, 0)),
            tok(GLA_V_W), tok(DIFF_V_W), tok(S5_WIDTH),
            _resident((None, D_MODEL, N_BRANCH * D_MODEL), lambda b, i: (layer, 0, 0)),
            _resident((None, N_BRANCH, S5_WIDTH, D_MODEL), lambda b, i: (layer, 0, 0, 0)),
            _resident((None, D_MODEL, D_MODEL), lambda b, i: (layer, 0, 0)),
        ],
        out_specs=tok(D_MODEL),
        out_shape=jax.ShapeDtypeStruct(h.shape, F32),
        compiler_params=_params("arbitrary", "arbitrary"),
        name="mixer_merge",
    )(h, ada4, norm_pre, norm_post, y_gla, y_diff, y_s5, wgate, wbr, wout)


def kernel(x, c, rel_bias, w_ada, b_ada, norm_pre, norm_post, ffn1_w_up, ffn1_w_down, ffn2_w_up, ffn2_w_down,
           w_in, gla_w_alpha, gla_b_alpha, gla_norm, diff_lambda, diff_norm, s5_lam_re, s5_lam_im, s5_log_step,
           s5_b_re, s5_b_im, s5_c_re, s5_c_im, s5_d, s5_w_glu, s5_b_glu, w_branch, w_out):
    bsz, seq, _ = x.shape
    assert bsz == SUBLANES, "the S5 scan keeps the batch on the sublane axis"

    f1 = (ffn1_w_up.astype(BF16), ffn1_w_down.astype(BF16))
    f2 = (ffn2_w_up.astype(BF16), ffn2_w_down.astype(BF16))
    o_alpha = 2 * GLA_QK_W + 2 * GLA_V_W
    o_diff = o_alpha + GLA_LOWRANK
    o_s5 = o_diff + 2 * DIFF_QK_W + DIFF_V_W
    o_gate = o_s5 + S5_WIDTH
    w_in_b = w_in.astype(BF16)
    wgla = w_in_b[:, :, :o_alpha]
    wa = jnp.pad(w_in_b[:, :, o_alpha:o_diff], ((0, 0), (0, 0), (0, LANES - GLA_LOWRANK)))
    walpha = jnp.pad(gla_w_alpha.astype(BF16), ((0, 0), (0, LANES - GLA_LOWRANK), (0, 0)))
    balpha = gla_b_alpha.reshape(DEPTH, 1, GLA_QK_W)
    wdiff = w_in_b[:, :, o_diff:o_s5]
    ws5 = w_in_b[:, :, o_s5:o_gate]
    wgate = w_in_b[:, :, o_gate:]
    wbr = w_branch.astype(BF16)
    wout = w_out.astype(BF16)
    wglu = s5_w_glu.astype(BF16)
    bglu = s5_b_glu.reshape(DEPTH, 1, 2 * S5_WIDTH)
    d_skip = s5_d.reshape(DEPTH, 1, S5_WIDTH)

    ada4 = _ada_call(c, w_ada, b_ada).reshape(DEPTH, bsz, ADA_CHUNKS, D_MODEL)
    bias = _bias_call(rel_bias)
    a_re, a_im, bmat, cmat = _s5_prep_call(s5_lam_re, s5_lam_im, s5_log_step, s5_b_re, s5_b_im,
                                           s5_c_re, s5_c_im, bsz)

    h = x
    for layer in range(DEPTH):
        h = _ffn_call(h, ada4, norm_pre, norm_post, *f1, layer, 0)
        gq, gk, gv, gr, la, dq, dk, dv, su = _inproj_call(h, ada4, norm_pre, wgla, wa, walpha, balpha,
                                                          wdiff, ws5, layer)
        y_gla = _gla_call(gq, gk, gv, gr, la, gla_norm, layer)
        y_diff = _diff_call(dq, dk, dv, bias, diff_lambda, diff_norm, layer)
        y_s5 = _s5_call(su, a_re, a_im, bmat, cmat, d_skip, wglu, bglu, layer)
        h = _merge_call(h, ada4, norm_pre, norm_post, y_gla, y_diff, y_s5, wgate, wbr, wout, layer)
        h = _ffn_call(h, ada4, norm_pre, norm_post, *f2, layer, 2)
    return h
```

```python
import functools
import math

import numpy as np
import jax
import jax.numpy as jnp
from jax import lax
from jax.experimental import pallas as pl
from jax.experimental.pallas import tpu as pltpu

F32 = jnp.float32
BF16 = jnp.bfloat16

D_MODEL = 1024
DEPTH = 2
D_FF = 2816
ADA_CHUNKS = 9
EPS = 1e-6

GLA_HEADS = 4
GLA_DK = 64
GLA_DV = 128
GLA_LOWRANK = 16
GLA_TAU = 16.0
GLA_CHUNK = 64
GLA_QK_W = GLA_HEADS * GLA_DK
GLA_V_W = GLA_HEADS * GLA_DV

DIFF_HEADS = 4
DIFF_DQK = 64
DIFF_DV = 128
DIFF_QK_W = DIFF_HEADS * 2 * DIFF_DQK
DIFF_V_W = DIFF_HEADS * DIFF_DV
NUM_BUCKETS = 32
MAX_DISTANCE = 128

S5_GROUPS = 32
S5_GROUP_CH = 16
S5_STATE = 64
S5_WIDTH = S5_GROUPS * S5_GROUP_CH
S5_NSTATE = S5_GROUPS * S5_STATE
S5_GROUP_BLOCKS = 4
S5_BLOCK_CH = S5_WIDTH // S5_GROUP_BLOCKS
S5_BLOCK_STATE = S5_NSTATE // S5_GROUP_BLOCKS

N_BRANCH = 3

SUBLANES = 8
LANES = 128
VMEM_LIMIT_BYTES = 56 * 1024 * 1024

TOKEN_TILE = 1024
ROW_BLOCK = 512
FF_CHUNK = 256
FF_NCHUNK = D_FF // FF_CHUNK
FFN_LOOKAHEAD = 1
GLA_PAIR = 2 * GLA_CHUNK
GLA_TILE = 512
ATT_TILE = 256
ATT_LOOKAHEAD = 4
S5_TIME_TILE = 64
ADA_COL_TILE = 1536


def _dot(a, b):
    return jnp.dot(a, b, preferred_element_type=F32)


def _dot_nt(a, b):
    return lax.dot_general(a, b, (((1,), (1,)), ((), ())), preferred_element_type=F32)


def _rms(x):
    return x * lax.rsqrt(jnp.mean(x * x, axis=-1, keepdims=True) + EPS)


def _prenorm_mod(h, g, shift, scale):
    return (_rms(h) * g) * (1.0 + scale) + shift


def _params(*sem):
    return pltpu.CompilerParams(dimension_semantics=sem, vmem_limit_bytes=VMEM_LIMIT_BYTES)


def _resident(shape, index_map):
    return pl.BlockSpec(shape, index_map, pipeline_mode=pl.Buffered(1))


def _ada_kernel(c_ref, w_ref, b_ref, o_ref):
    cond = jax.nn.silu(c_ref[...]).astype(BF16)
    o_ref[...] = _dot(cond, w_ref[...].astype(BF16)) + b_ref[...]


def _ada_call(c, w_ada, b_ada):
    bsz = c.shape[0]
    n = ADA_CHUNKS * D_MODEL
    return pl.pallas_call(
        _ada_kernel,
        grid=(DEPTH, n // ADA_COL_TILE),
        in_specs=[
            pl.BlockSpec((bsz, D_MODEL), lambda l, j: (0, 0)),
            pl.BlockSpec((None, D_MODEL, ADA_COL_TILE), lambda l, j: (l, 0, j)),
            pl.BlockSpec((None, 1, ADA_COL_TILE), lambda l, j: (l, 0, j)),
        ],
        out_specs=pl.BlockSpec((None, bsz, ADA_COL_TILE), lambda l, j: (l, 0, j)),
        out_shape=jax.ShapeDtypeStruct((DEPTH, bsz, n), F32),
        compiler_params=_params("arbitrary", "arbitrary"),
        name="ada_proj",
    )(c, w_ada, b_ada.reshape(DEPTH, 1, n))


def _ffn_kernel(h_ref, ada_ref, npre_ref, npost_ref, wup_ref, wdn_ref, o_ref, *, sub):
    a0 = 3 * sub

    def down(j, gate, up):
        mid = (jax.nn.silu(gate) * up).astype(BF16)
        return _dot(mid, wdn_ref[j * FF_CHUNK:(j + 1) * FF_CHUNK, :])

    for r in range(h_ref.shape[0] // ROW_BLOCK):
        rows = slice(r * ROW_BLOCK, (r + 1) * ROW_BLOCK)
        h = h_ref[rows, :]
        u = _prenorm_mod(h, npre_ref[sub:sub + 1, :], ada_ref[a0:a0 + 1, :], ada_ref[a0 + 1:a0 + 2, :])
        u = u.astype(BF16)

        def gate_up(j, u=u):
            lo = j * FF_CHUNK
            return _dot(u, wup_ref[:, lo:lo + FF_CHUNK]), _dot(u, wup_ref[:, D_FF + lo:D_FF + lo + FF_CHUNK])

        acc = None
        pending = []
        for j in range(FF_NCHUNK + FFN_LOOKAHEAD):
            if j < FF_NCHUNK:
                pending.append((j,) + gate_up(j))
            if j >= FFN_LOOKAHEAD:
                d = down(*pending.pop(0))
                acc = d if acc is None else acc + d
        post = _rms(acc) * npost_ref[sub:sub + 1, :]
        o_ref[rows, :] = h + (0.5 * ada_ref[a0 + 2:a0 + 3, :]) * post


def _ffn_call(h, ada4, norm_pre, norm_post, wup, wdn, layer, sub):
    bsz, seq, _ = h.shape
    tm = TOKEN_TILE
    tok = pl.BlockSpec((None, tm, D_MODEL), lambda b, i: (b, i, 0))
    return pl.pallas_call(
        functools.partial(_ffn_kernel, sub=sub),
        grid=(bsz, seq // tm),
        in_specs=[
            tok,
            pl.BlockSpec((None, None, ADA_CHUNKS, D_MODEL), lambda b, i: (layer, b, 0, 0)),
            pl.BlockSpec((None, 3, D_MODEL), lambda b, i: (layer, 0, 0)),
            pl.BlockSpec((None, 3, D_MODEL), lambda b, i: (layer, 0, 0)),
            _resident((None, D_MODEL, 2 * D_FF), lambda b, i: (layer, 0, 0)),
            _resident((None, D_FF, D_MODEL), lambda b, i: (layer, 0, 0)),
        ],
        out_specs=tok,
        out_shape=jax.ShapeDtypeStruct(h.shape, F32),
        compiler_params=_params("arbitrary", "arbitrary"),
        name=f"ffn{sub // 2 + 1}",
    )(h, ada4, norm_pre, norm_post, wup, wdn)


def _log_sigmoid(z):
    return jnp.minimum(z, 0.0) - jnp.log1p(jnp.exp(-jnp.abs(z)))


def _inproj_kernel(h_ref, ada_ref, npre_ref, wgla_ref, wa_ref, walpha_ref, balpha_ref, wdiff_ref, ws5_ref,
                   gq_ref, gk_ref, gv_ref, gr_ref, la_ref, dq_ref, dk_ref, dv_ref, su_ref):
    for r in range(h_ref.shape[0] // ROW_BLOCK):
        rows = slice(r * ROW_BLOCK, (r + 1) * ROW_BLOCK)
        u = _prenorm_mod(h_ref[rows, :], npre_ref[1:2, :], ada_ref[3:4, :], ada_ref[4:5, :]).astype(BF16)
        pg = _dot(u, wgla_ref[...])
        gq_ref[rows, :] = pg[:, :GLA_QK_W]
        gk_ref[rows, :] = pg[:, GLA_QK_W:2 * GLA_QK_W]
        gv_ref[rows, :] = pg[:, 2 * GLA_QK_W:2 * GLA_QK_W + GLA_V_W].astype(BF16)
        gr_ref[rows, :] = pg[:, 2 * GLA_QK_W + GLA_V_W:]
        a_low = _dot(u, wa_ref[...]).astype(BF16)
        z = _dot(a_low, walpha_ref[...]) + balpha_ref[...]
        la_ref[rows, :] = _log_sigmoid(z) / GLA_TAU
        pd = _dot(u, wdiff_ref[...])
        dq_ref[rows, :] = pd[:, :DIFF_QK_W].astype(BF16)
        dk_ref[rows, :] = pd[:, DIFF_QK_W:2 * DIFF_QK_W].astype(BF16)
        dv_ref[rows, :] = pd[:, 2 * DIFF_QK_W:].astype(BF16)
        su_ref[rows, :] = _dot(u, ws5_ref[...]).astype(BF16)


def _inproj_call(h, ada4, norm_pre, wgla, wa, walpha, balpha, wdiff, ws5, layer):
    bsz, seq, _ = h.shape
    tm = TOKEN_TILE

    def tok(w):
        return pl.BlockSpec((None, tm, w), lambda b, i: (b, i, 0))

    def sds(w, dt):
        return jax.ShapeDtypeStruct((bsz, seq, w), dt)

    def res(arr):
        nd = arr.ndim - 1
        return _resident((None,) + arr.shape[1:], lambda b, i: (layer,) + (0,) * nd)

    return pl.pallas_call(
        _inproj_kernel,
        grid=(bsz, seq // tm),
        in_specs=[
            tok(D_MODEL),
            pl.BlockSpec((None, None, ADA_CHUNKS, D_MODEL), lambda b, i: (layer, b, 0, 0)),
            pl.BlockSpec((None, 3, D_MODEL), lambda b, i: (layer, 0, 0)),
            res(wgla), res(wa), res(walpha), res(balpha), res(wdiff), res(ws5),
        ],
        out_specs=[tok(GLA_QK_W), tok(GLA_QK_W), tok(GLA_V_W), tok(GLA_V_W), tok(GLA_QK_W),
                   tok(DIFF_QK_W), tok(DIFF_QK_W), tok(DIFF_V_W), tok(S5_WIDTH)],
        out_shape=[sds(GLA_QK_W, F32), sds(GLA_QK_W, F32), sds(GLA_V_W, BF16), sds(GLA_V_W, F32),
                   sds(GLA_QK_W, F32), sds(DIFF_QK_W, BF16), sds(DIFF_QK_W, BF16), sds(DIFF_V_W, BF16),
                   sds(S5_WIDTH, BF16)],
        compiler_params=_params("arbitrary", "arbitrary"),
        name="mixer_in_proj",
    )(h, ada4, norm_pre, wgla, wa, walpha, balpha, wdiff, ws5)


def _cumsum_rows(tri, x):
    hi = x.astype(BF16)
    r1 = x - hi.astype(F32)
    mid = r1.astype(BF16)
    lo = (r1 - mid.astype(F32)).astype(BF16)
    return _dot(tri, hi) + _dot(tri, mid) + _dot(tri, lo)


def _gla_kernel(q_ref, k_ref, v_ref, r_ref, la_ref, g_ref, o_ref, s_ref, *, n_pairs):
    @pl.when(pl.program_id(1) == 0)
    def _():
        s_ref[...] = jnp.zeros_like(s_ref)

    row = lax.broadcasted_iota(jnp.int32, (GLA_PAIR, GLA_PAIR), 0)
    col = lax.broadcasted_iota(jnp.int32, (GLA_PAIR, GLA_PAIR), 1)
    causal = (col <= row) & ((row // GLA_CHUNK) == (col // GLA_CHUNK))
    tri = jnp.where(causal, 1.0, 0.0).astype(BF16)
    qlane = lax.broadcasted_iota(jnp.int32, (GLA_PAIR, GLA_QK_W), 1)
    tlane = lax.broadcasted_iota(jnp.int32, (GLA_QK_W, GLA_PAIR), 1)
    first = tlane < GLA_CHUNK

    def decays(p):
        rows = slice(p * GLA_PAIR, (p + 1) * GLA_PAIR)
        b = _cumsum_rows(tri, la_ref[rows, :])
        b_t = b.T
        qd = (q_ref[rows, :] * (GLA_DK ** -0.5)) * jnp.exp(b)
        k_t = k_ref[rows, :].T
        bl0 = b_t[:, GLA_CHUNK - 1:GLA_CHUNK]
        bl1 = b_t[:, GLA_PAIR - 1:GLA_PAIR]
        kdec_t = k_t * jnp.exp(jnp.where(first, bl0, bl1) - b_t)
        return dict(
            qh=[jnp.where(qlane // GLA_DK == h, qd, 0.0).astype(BF16) for h in range(GLA_HEADS)],
            kd_t=(k_t * jnp.exp(-b_t)).astype(BF16),
            kdec0=jnp.where(first, kdec_t, 0.0).astype(BF16),
            kdec1=jnp.where(first, 0.0, kdec_t).astype(BF16),
            dec0=jnp.exp(bl0), dec1=jnp.exp(bl1))

    def products(p, d, s0):
        v = v_ref[p * GLA_PAIR:(p + 1) * GLA_PAIR, :]
        ds0, ds1 = [], []
        for h in range(GLA_HEADS):
            hr = slice(h * GLA_DK, (h + 1) * GLA_DK)
            vh = v[:, h * GLA_DV:(h + 1) * GLA_DV]
            ds0.append(_dot(d["kdec0"][hr, :], vh))
            ds1.append(_dot(d["kdec1"][hr, :], vh))
        att = [jnp.where(causal, _dot(qh, d["kd_t"]), 0.0).astype(BF16) for qh in d["qh"]]
        s1 = d["dec0"] * s0 + jnp.concatenate(ds0, axis=0)
        s2 = d["dec1"] * s1 + jnp.concatenate(ds1, axis=0)
        s0b = s0.astype(BF16)
        s1b = s1.astype(BF16)
        inter = [jnp.concatenate([_dot(qh[:GLA_CHUNK], s0b), _dot(qh[GLA_CHUNK:], s1b)], axis=0)
                 for qh in d["qh"]]
        return s2, (v, att, inter)

    def outputs(p, v, att, inter):
        rows = slice(p * GLA_PAIR, (p + 1) * GLA_PAIR)
        r = r_ref[rows, :]
        for h in range(GLA_HEADS):
            vc = slice(h * GLA_DV, (h + 1) * GLA_DV)
            o = _rms(_dot(att[h], v[:, vc]) + inter[h]) * g_ref[:, vc]
            o_ref[rows, vc] = (o * jax.nn.silu(r[:, vc])).astype(BF16)

    state = s_ref[...]
    dec, prod = {}, {}
    for i in range(n_pairs + 2):
        if i < n_pairs:
            dec[i] = decays(i)
        if 0 <= i - 1 < n_pairs:
            state, prod[i - 1] = products(i - 1, dec.pop(i - 1), state)
        if 0 <= i - 2 < n_pairs:
            outputs(i - 2, *prod.pop(i - 2))
    s_ref[...] = state


def _gla_call(gq, gk, gv, gr, la, gla_norm, layer):
    bsz, seq, _ = gq.shape
    tl = GLA_TILE

    def tok(w):
        return pl.BlockSpec((None, tl, w), lambda b, i: (b, i, 0))

    return pl.pallas_call(
        functools.partial(_gla_kernel, n_pairs=tl // GLA_PAIR),
        grid=(bsz, seq // tl),
        in_specs=[tok(GLA_QK_W), tok(GLA_QK_W), tok(GLA_V_W), tok(GLA_V_W), tok(GLA_QK_W),
                  pl.BlockSpec((None, 1, GLA_V_W), lambda b, i: (layer, 0, 0))],
        out_specs=tok(GLA_V_W),
        out_shape=jax.ShapeDtypeStruct((bsz, seq, GLA_V_W), BF16),
        scratch_shapes=[pltpu.VMEM((GLA_QK_W, GLA_DV), F32)],
        compiler_params=_params("arbitrary", "arbitrary"),
        name="gla_mixer",
    )(gq, gk, gv, gr, la, gla_norm.reshape(DEPTH, 1, GLA_V_W))


def _t5_bucket_tiles():
    t = ATT_TILE
    r = np.arange(t)[:, None]
    c = np.arange(t)[None, :]
    max_exact = NUM_BUCKETS // 2
    tiles = []
    for off in range(3):
        dist = off * t + r - c
        d = np.maximum(dist, 1).astype(np.float32)
        large = max_exact + (np.log(d / np.float32(max_exact)) / np.float32(math.log(MAX_DISTANCE / max_exact))
                             * np.float32(NUM_BUCKETS - max_exact)).astype(np.int32)
        large = np.minimum(large, NUM_BUCKETS - 1)
        bucket = np.where(dist < max_exact, dist, large)
        tiles.append(np.where(dist >= 0, bucket, -1).T)
    far = tiles[2]
    assert (far == NUM_BUCKETS - 1).all()
    return np.stack(tiles).astype(np.int32)


def _bias_kernel(rel_ref, bucket_ref, o_ref):
    m = pl.program_id(0)
    bucket = bucket_ref[...]
    acc = jnp.full(bucket.shape, -jnp.inf, F32)
    for kk in range(NUM_BUCKETS):
        acc = jnp.where(bucket == kk, rel_ref[kk, m], acc)
    o_ref[...] = acc


def _bias_call(rel_bias):
    n_maps = 2 * DIFF_HEADS
    t = ATT_TILE
    return pl.pallas_call(
        _bias_kernel,
        grid=(n_maps,),
        in_specs=[pl.BlockSpec(memory_space=pltpu.SMEM),
                  pl.BlockSpec((3, t, t), lambda m: (0, 0, 0))],
        out_specs=pl.BlockSpec((None, 3, t, t), lambda m: (m, 0, 0, 0)),
        out_shape=jax.ShapeDtypeStruct((n_maps, 3, t, t), F32),
        compiler_params=_params("arbitrary"),
        name="t5_bias_tiles",
    )(rel_bias, jnp.asarray(_t5_bucket_tiles()))


def _attention_steps(load_q, load_k, v_all, bias_ref, lam_ref, g_ref, o_ref, lam_init, n_tiles):
    t = ATT_TILE
    scale = DIFF_DQK ** -0.5
    lv = lam_ref[...]
    lam = (jnp.exp(jnp.sum(lv[0:1] * lv[1:2], axis=-1, keepdims=True))
           - jnp.exp(jnp.sum(lv[2:3] * lv[3:4], axis=-1, keepdims=True)) + lam_init)
    v_t = jnp.concatenate([v_all.astype(F32).T, jnp.ones((16, v_all.shape[0]), F32)], axis=0).astype(BF16)
    lane = lax.broadcasted_iota(jnp.int32, (t, 2 * DIFF_DQK), 1)
    far = [bias_ref[mp, 2, 0:1, 0:1] for mp in range(2)]

    tasks = [(qi, kj, mp) for qi in range(n_tiles) for kj in range(qi + 1) for mp in range(2)]
    masked_q = {}
    state = {}
    done = {}

    def logits(qi, kj, mp):
        if qi not in masked_q:
            q = load_q(slice(qi * t, (qi + 1) * t))
            zero = jnp.zeros_like(q)
            masked_q[qi] = (jnp.where(lane < DIFF_DQK, q, zero) * scale,
                            jnp.where(lane < DIFF_DQK, zero, q) * scale)
        s = _dot_nt(load_k(slice(kj * t, (kj + 1) * t)), masked_q[qi][mp])
        return s + bias_ref[mp, qi - kj] if qi - kj < 2 else s

    def accumulate(qi, kj, mp, s):
        shift = 0.0 if qi - kj < 2 else far[mp]
        m_blk = jnp.max(s, axis=0, keepdims=True) + shift
        prev = state.get((qi, mp))
        m_new = m_blk if prev is None else jnp.maximum(prev[0], m_blk)
        p = jnp.exp((s - (m_new - shift)).astype(BF16))
        pv = _dot(v_t[:, kj * t:(kj + 1) * t], p)
        if prev is None:
            state[(qi, mp)] = (m_new, pv)
        else:
            state[(qi, mp)] = (m_new, jnp.exp(prev[0] - m_new) * prev[1] + pv)
        if kj == qi:
            _, acc = state.pop((qi, mp))
            done.setdefault(qi, {})[mp] = acc[:DIFF_DV] / acc[DIFF_DV:DIFF_DV + 1]
            if len(done[qi]) == 2:
                o = done[qi][0] - lam * done[qi][1]
                o = o * lax.rsqrt(jnp.mean(o * o, axis=0, keepdims=True) + EPS)
                o_ref[qi * t:(qi + 1) * t, :] = (o.T * g_ref[...] * (1.0 - lam_init)).astype(BF16)
                del done[qi]

    pending = []
    for task in tasks:
        pending.append((task, logits(*task)))
        if len(pending) > ATT_LOOKAHEAD:
            ready, s = pending.pop(0)
            accumulate(*ready, s)
        yield
    for ready, s in pending:
        accumulate(*ready, s)
        yield


def _diff_kernel(q_ref, k_ref, v_ref, bias_ref, lam_ref, g_ref, o_ref, *, lam_init, n_tiles):
    for _ in _attention_steps(lambda sl: q_ref[sl, :], lambda sl: k_ref[sl, :], v_ref[...],
                              bias_ref, lam_ref, g_ref, o_ref, lam_init, n_tiles):
        pass


def _diff_call(dq, dk, dv, bias, diff_lambda, diff_norm, layer):
    bsz, seq, _ = dq.shape
    t = ATT_TILE
    lam_init = 0.8 - 0.6 * math.exp(-0.3 * layer)
    return pl.pallas_call(
        functools.partial(_diff_kernel, lam_init=lam_init, n_tiles=seq // t),
        grid=(bsz, DIFF_HEADS),
        in_specs=[
            pl.BlockSpec((None, seq, 2 * DIFF_DQK), lambda b, h: (b, 0, h)),
            pl.BlockSpec((None, seq, 2 * DIFF_DQK), lambda b, h: (b, 0, h)),
            pl.BlockSpec((None, seq, DIFF_DV), lambda b, h: (b, 0, h)),
            pl.BlockSpec((2, 3, t, t), lambda b, h: (h, 0, 0, 0)),
            pl.BlockSpec((None, 4, DIFF_DQK), lambda b, h: (layer, 0, 0)),
            pl.BlockSpec((None, 1, DIFF_DV), lambda b, h: (layer, 0, h)),
        ],
        out_specs=pl.BlockSpec((None, seq, DIFF_DV), lambda b, h: (b, 0, h)),
        out_shape=jax.ShapeDtypeStruct((bsz, seq, DIFF_V_W), BF16),
        compiler_params=_params("arbitrary", "arbitrary"),
        name="diff_attention",
    )(dq, dk, dv, bias, diff_lambda, diff_norm.reshape(DEPTH, 1, DIFF_V_W))


def _s5_prep_kernel(lre_ref, lim_ref, lstep_ref, bre_ref, bim_ref, cre_ref, cim_ref,
                    are_ref, aim_ref, bmat_ref, cmat_ref):
    lr = lre_ref[...]
    li = lim_ref[...]
    dt = jnp.exp(lstep_ref[...])
    mag = jnp.exp(lr * dt)
    a_re = mag * jnp.cos(li * dt)
    a_im = mag * jnp.sin(li * dt)
    are_ref[...] = jnp.broadcast_to(a_re, are_ref.shape)
    aim_ref[...] = jnp.broadcast_to(a_im, aim_ref.shape)
    nr, ni = a_re - 1.0, a_im
    den = lr * lr + li * li
    f_re = (nr * lr + ni * li) / den
    f_im = (ni * lr - nr * li) / den
    ns = S5_BLOCK_STATE
    for gb in range(S5_GROUP_BLOCKS):
        fr = f_re[:, gb * ns:(gb + 1) * ns]
        fi = f_im[:, gb * ns:(gb + 1) * ns]
        br = bre_ref[gb]
        bi = bim_ref[gb]
        bmat_ref[gb, :, :ns] = (fr * br - fi * bi).astype(BF16)
        bmat_ref[gb, :, ns:] = (fr * bi + fi * br).astype(BF16)
        cmat_ref[gb, :ns, :] = cre_ref[gb].astype(BF16)
        cmat_ref[gb, ns:, :] = (-cim_ref[gb]).astype(BF16)


def _s5_prep_call(lam_re, lam_im, log_step, b_re, b_im, c_re, c_im, bsz):
    d = DEPTH
    nb, ch, ns = S5_GROUP_BLOCKS, S5_BLOCK_CH, S5_BLOCK_STATE
    gpb = S5_GROUPS // nb
    eye = jnp.eye(gpb, dtype=F32)

    def in_blocks(b):
        t = b.reshape(d, nb, gpb, S5_STATE, S5_GROUP_CH).transpose(0, 1, 2, 4, 3)
        return (t[:, :, :, :, None, :] * eye[None, None, :, None, :, None]).reshape(d, nb, ch, ns)

    def out_blocks(c):
        t = c.reshape(d, nb, gpb, S5_GROUP_CH, S5_STATE).transpose(0, 1, 4, 2, 3)
        return (t[:, :, None, :, :, :] * eye[None, None, :, None, :, None]).reshape(d, nb, ns, ch)

    def row(x):
        return x.reshape(d, 1, S5_NSTATE)

    lstep = jnp.broadcast_to(log_step[:, :, None], (d, S5_GROUPS, S5_STATE))

    def full(shape):
        nd = len(shape)
        return pl.BlockSpec((None,) + shape, lambda l: (l,) + (0,) * nd)

    return pl.pallas_call(
        _s5_prep_kernel,
        grid=(d,),
        in_specs=[full((1, S5_NSTATE))] * 3 + [full((nb, ch, ns))] * 2 + [full((nb, ns, ch))] * 2,
        out_specs=[full((bsz, S5_NSTATE)), full((bsz, S5_NSTATE)),
                   full((nb, ch, 2 * ns)), full((nb, 2 * ns, ch))],
        out_shape=[jax.ShapeDtypeStruct((d, bsz, S5_NSTATE), F32), jax.ShapeDtypeStruct((d, bsz, S5_NSTATE), F32),
                   jax.ShapeDtypeStruct((d, nb, ch, 2 * ns), BF16), jax.ShapeDtypeStruct((d, nb, 2 * ns, ch), BF16)],
        compiler_params=_params("arbitrary"),
        name="s5_discretize",
    )(row(lam_re), row(lam_im), row(lstep), in_blocks(b_re), in_blocks(b_im), out_blocks(c_re), out_blocks(c_im))


def _s5_kernel(u_ref, are_ref, aim_ref, bmat_ref, cmat_ref, d_ref, wglu_ref, bglu_ref, o_ref,
               tb_ref, xr_ref, xi_ref, sr_ref, si_ref, *, bsz, n_steps):
    @pl.when(pl.program_id(0) == 0)
    def _():
        sr_ref[...] = jnp.zeros_like(sr_ref)
        si_ref[...] = jnp.zeros_like(si_ref)

    ns, ch, nb = S5_BLOCK_STATE, S5_BLOCK_CH, S5_GROUP_BLOCKS
    for b in range(bsz):
        ub = u_ref[b].astype(F32)
        for gb in range(nb):
            tb_ref[gb, pl.ds(b, n_steps, stride=bsz), :] = ub[:, gb * ch:(gb + 1) * ch]
    u = [tb_ref[gb] for gb in range(nb)]

    def drive(gb):
        return _dot(u[gb].astype(BF16), bmat_ref[gb])

    def scan(gb, bu):
        cols = slice(gb * ns, (gb + 1) * ns)
        ar, ai = are_ref[:, cols], aim_ref[:, cols]
        pr, pi = sr_ref[:, cols], si_ref[:, cols]
        for t in range(n_steps):
            rows = slice(t * bsz, (t + 1) * bsz)
            pr, pi = ar * pr - ai * pi + bu[rows, :ns], ar * pi + ai * pr + bu[rows, ns:]
            xr_ref[rows, cols] = pr
            xi_ref[rows, cols] = pi
        sr_ref[:, cols] = pr
        si_ref[:, cols] = pi

    def readout(gb):
        cols = slice(gb * ns, (gb + 1) * ns)
        return (_dot(xr_ref[:, cols].astype(BF16), cmat_ref[gb, :ns, :])
                + _dot(xi_ref[:, cols].astype(BF16), cmat_ref[gb, ns:, :]))

    bu, ys = {0: drive(0)}, []
    for gb in range(nb):
        if gb + 1 < nb:
            bu[gb + 1] = drive(gb + 1)
        scan(gb, bu.pop(gb))
        if gb >= 1:
            ys.append(readout(gb - 1))
    ys.append(readout(nb - 1))

    y = jax.nn.gelu(jnp.concatenate(ys, axis=-1) + d_ref[...] * jnp.concatenate(u, axis=-1))
    z = _dot(y.astype(BF16), wglu_ref[...]) + bglu_ref[...]
    out = z[:, :S5_WIDTH] * jax.nn.sigmoid(z[:, S5_WIDTH:])
    for gb in range(nb):
        tb_ref[gb] = out[:, gb * ch:(gb + 1) * ch]
    for b in range(bsz):
        for gb in range(nb):
            o_ref[b, :, gb * ch:(gb + 1) * ch] = tb_ref[gb, pl.ds(b, n_steps, stride=bsz), :].astype(BF16)


def _s5_call(su, a_re, a_im, bmat, cmat, d_skip, wglu, bglu, layer):
    bsz, seq, _ = su.shape
    tl = S5_TIME_TILE
    tr = tl * bsz
    nb, ch, ns = S5_GROUP_BLOCKS, S5_BLOCK_CH, S5_BLOCK_STATE

    def res(shape):
        nd = len(shape)
        return _resident((None,) + shape, lambda i: (layer,) + (0,) * nd)

    tok = pl.BlockSpec((bsz, tl, S5_WIDTH), lambda i: (0, i, 0))
    return pl.pallas_call(
        functools.partial(_s5_kernel, bsz=bsz, n_steps=tl),
        grid=(seq // tl,),
        in_specs=[tok,
                  res((bsz, S5_NSTATE)), res((bsz, S5_NSTATE)),
                  res((nb, ch, 2 * ns)), res((nb, 2 * ns, ch)),
                  res((1, S5_WIDTH)), res((S5_WIDTH, 2 * S5_WIDTH)), res((1, 2 * S5_WIDTH))],
        out_specs=tok,
        out_shape=jax.ShapeDtypeStruct((bsz, seq, S5_WIDTH), BF16),
        scratch_shapes=[pltpu.VMEM((nb, tr, ch), F32),
                        pltpu.VMEM((tr, S5_NSTATE), F32), pltpu.VMEM((tr, S5_NSTATE), F32),
                        pltpu.VMEM((bsz, S5_NSTATE), F32), pltpu.VMEM((bsz, S5_NSTATE), F32)],
        compiler_params=_params("arbitrary"),
        name="s5_mixer",
    )(su, a_re, a_im, bmat, cmat, d_skip, wglu, bglu)


def _merge_kernel(h_ref, ada_ref, npre_ref, npost_ref, yg_ref, yd_ref, ys_ref, wgate_ref, wbr_ref, wout_ref, o_ref):
    for r in range(h_ref.shape[0] // ROW_BLOCK):
        rows = slice(r * ROW_BLOCK, (r + 1) * ROW_BLOCK)
        h = h_ref[rows, :]
        u = _prenorm_mod(h, npre_ref[1:2, :], ada_ref[3:4, :], ada_ref[4:5, :]).astype(BF16)
        merged = None
        for i, y_ref in enumerate((yg_ref, yd_ref, ys_ref)):
            gate = jax.nn.sigmoid(_dot(u, wgate_ref[:, i * D_MODEL:(i + 1) * D_MODEL]))
            term = gate * _dot(y_ref[rows, :], wbr_ref[i])
            merged = term if merged is None else merged + term
        y = _dot(merged.astype(BF16), wout_ref[...])
        o_ref[rows, :] = h + ada_ref[5:6, :] * (_rms(y) * npost_ref[1:2, :])


def _merge_call(h, ada4, norm_pre, norm_post, y_gla, y_diff, y_s5, wgate, wbr, wout, layer):
    bsz, seq, _ = h.shape
    tm = TOKEN_TILE

    def tok(w):
        return pl.BlockSpec((None, tm, w), lambda b, i: (b, i, 0))

    return pl.pallas_call(
        _merge_kernel,
        grid=(bsz, seq // tm),
        in_specs=[
            tok(D_MODEL),
            pl.BlockSpec((None, None, ADA_CHUNKS, D_MODEL), lambda b, i: (layer, b, 0, 0)),
            pl.BlockSpec((None, 3, D_MODEL), lambda b, i: (layer, 0, 0)),
            pl.BlockSpec((None, 3, D_MODEL), lambda b, i: (layer, 0, 0)),
            tok(GLA_V_W), tok(DIFF_V_W), tok(S5_WIDTH),
            _resident((None, D_MODEL, N_BRANCH * D_MODEL), lambda b, i: (layer, 0, 0)),
            _resident((None, N_BRANCH, S5_WIDTH, D_MODEL), lambda b, i: (layer, 0, 0, 0)),
            _resident((None, D_MODEL, D_MODEL), lambda b, i: (layer, 0, 0)),
        ],
        out_specs=tok(D_MODEL),
        out_shape=jax.ShapeDtypeStruct(h.shape, F32),
        compiler_params=_params("arbitrary", "arbitrary"),
        name="mixer_merge",
    )(h, ada4, norm_pre, norm_post, y_gla, y_diff, y_s5, wgate, wbr, wout)


def kernel(x, c, rel_bias, w_ada, b_ada, norm_pre, norm_post, ffn1_w_up, ffn1_w_down, ffn2_w_up, ffn2_w_down,
           w_in, gla_w_alpha, gla_b_alpha, gla_norm, diff_lambda, diff_norm, s5_lam_re, s5_lam_im, s5_log_step,
           s5_b_re, s5_b_im, s5_c_re, s5_c_im, s5_d, s5_w_glu, s5_b_glu, w_branch, w_out):
    bsz, seq, _ = x.shape
    assert bsz == SUBLANES, "the S5 scan keeps the batch on the sublane axis"

    f1 = (ffn1_w_up.astype(BF16), ffn1_w_down.astype(BF16))
    f2 = (ffn2_w_up.astype(BF16), ffn2_w_down.astype(BF16))
    o_alpha = 2 * GLA_QK_W + 2 * GLA_V_W
    o_diff = o_alpha + GLA_LOWRANK
    o_s5 = o_diff + 2 * DIFF_QK_W + DIFF_V_W
    o_gate = o_s5 + S5_WIDTH
    w_in_b = w_in.astype(BF16)
    wgla = w_in_b[:, :, :o_alpha]
    wa = jnp.pad(w_in_b[:, :, o_alpha:o_diff], ((0, 0), (0, 0), (0, LANES - GLA_LOWRANK)))
    walpha = jnp.pad(gla_w_alpha.astype(BF16), ((0, 0), (0, LANES - GLA_LOWRANK), (0, 0)))
    balpha = gla_b_alpha.reshape(DEPTH, 1, GLA_QK_W)
    wdiff = w_in_b[:, :, o_diff:o_s5]
    ws5 = w_in_b[:, :, o_s5:o_gate]
    wgate = w_in_b[:, :, o_gate:]
    wbr = w_branch.astype(BF16)
    wout = w_out.astype(BF16)
    wglu = s5_w_glu.astype(BF16)
    bglu = s5_b_glu.reshape(DEPTH, 1, 2 * S5_WIDTH)
    d_skip = s5_d.reshape(DEPTH, 1, S5_WIDTH)

    ada4 = _ada_call(c, w_ada, b_ada).reshape(DEPTH, bsz, ADA_CHUNKS, D_MODEL)
    bias = _bias_call(rel_bias)
    a_re, a_im, bmat, cmat = _s5_prep_call(s5_lam_re, s5_lam_im, s5_log_step, s5_b_re, s5_b_im,
                                           s5_c_re, s5_c_im, bsz)

    h = x
    for layer in range(DEPTH):
        h = _ffn_call(h, ada4, norm_pre, norm_post, *f1, layer, 0)
        gq, gk, gv, gr, la, dq, dk, dv, su = _inproj_call(h, ada4, norm_pre, wgla, wa, walpha, balpha,
                                                          wdiff, ws5, layer)
        y_gla = _gla_call(gq, gk, gv, gr, la, gla_norm, layer)
        y_diff = _diff_call(dq, dk, dv, bias, diff_lambda, diff_norm, layer)
        y_s5 = _s5_call(su, a_re, a_im, bmat, cmat, d_skip, wglu, bglu, layer)
        h = _merge_call(h, ada4, norm_pre, norm_post, y_gla, y_diff, y_s5, wgate, wbr, wout, layer)
        h = _ffn_call(h, ada4, norm_pre, norm_post, *f2, layer, 2)
    return h
```

```python
import functools
import math

import numpy as np
import jax
import jax.numpy as jnp
from jax import lax
from jax.experimental import pallas as pl
from jax.experimental.pallas import tpu as pltpu

F32 = jnp.float32
BF16 = jnp.bfloat16

D_MODEL = 1024
DEPTH = 2
D_FF = 2816
ADA_CHUNKS = 9
EPS = 1e-6

GLA_HEADS = 4
GLA_DK = 64
GLA_DV = 128
GLA_LOWRANK = 16
GLA_TAU = 16.0
GLA_CHUNK = 64
GLA_QK_W = GLA_HEADS * GLA_DK
GLA_V_W = GLA_HEADS * GLA_DV

DIFF_HEADS = 4
DIFF_DQK = 64
DIFF_DV = 128
DIFF_QK_W = DIFF_HEADS * 2 * DIFF_DQK
DIFF_V_W = DIFF_HEADS * DIFF_DV
NUM_BUCKETS = 32
MAX_DISTANCE = 128

S5_GROUPS = 32
S5_GROUP_CH = 16
S5_STATE = 64
S5_WIDTH = S5_GROUPS * S5_GROUP_CH
S5_NSTATE = S5_GROUPS * S5_STATE
S5_GROUP_BLOCKS = 4
S5_BLOCK_CH = S5_WIDTH // S5_GROUP_BLOCKS
S5_BLOCK_STATE = S5_NSTATE // S5_GROUP_BLOCKS

N_BRANCH = 3

SUBLANES = 8
LANES = 128
VMEM_LIMIT_BYTES = 56 * 1024 * 1024

TOKEN_TILE = 1024
ROW_BLOCK = 512
FF_CHUNK = 256
FF_NCHUNK = D_FF // FF_CHUNK
FFN_LOOKAHEAD = 1
GLA_PAIR = 2 * GLA_CHUNK
GLA_TILE = 512
ATT_TILE = 256
ATT_LOOKAHEAD = 4
S5_TIME_TILE = 64
ADA_COL_TILE = 1536
MIX_COLS = {"gate": (N_BRANCH * D_MODEL, 0), "gla": (2 * GLA_QK_W + 2 * GLA_V_W, 2),
            "diff": (2 * DIFF_QK_W + DIFF_V_W, 3), "s5": (S5_WIDTH, 12), "alpha": (LANES, 52)}


def _dot(a, b):
    return jnp.dot(a, b, preferred_element_type=F32)


def _dot_nt(a, b):
    return lax.dot_general(a, b, (((1,), (1,)), ((), ())), preferred_element_type=F32)


def _rms(x):
    return x * lax.rsqrt(jnp.mean(x * x, axis=-1, keepdims=True) + EPS)


def _prenorm_mod(h, g, shift, scale):
    return (_rms(h) * g) * (1.0 + scale) + shift


def _params(*sem):
    return pltpu.CompilerParams(dimension_semantics=sem, vmem_limit_bytes=VMEM_LIMIT_BYTES)


def _resident(shape, index_map):
    return pl.BlockSpec(shape, index_map, pipeline_mode=pl.Buffered(1))


def _ada_kernel(c_ref, w_ref, b_ref, o_ref):
    cond = jax.nn.silu(c_ref[...]).astype(BF16)
    o_ref[...] = _dot(cond, w_ref[...].astype(BF16)) + b_ref[...]


def _ada_call(c, w_ada, b_ada):
    bsz = c.shape[0]
    n = ADA_CHUNKS * D_MODEL
    return pl.pallas_call(
        _ada_kernel,
        grid=(DEPTH, n // ADA_COL_TILE),
        in_specs=[
            pl.BlockSpec((bsz, D_MODEL), lambda l, j: (0, 0)),
            pl.BlockSpec((None, D_MODEL, ADA_COL_TILE), lambda l, j: (l, 0, j)),
            pl.BlockSpec((None, 1, ADA_COL_TILE), lambda l, j: (l, 0, j)),
        ],
        out_specs=pl.BlockSpec((None, bsz, ADA_COL_TILE), lambda l, j: (l, 0, j)),
        out_shape=jax.ShapeDtypeStruct((DEPTH, bsz, n), F32),
        compiler_params=_params("arbitrary", "arbitrary"),
        name="ada_proj",
    )(c, w_ada, b_ada.reshape(DEPTH, 1, n))


def _ffn_kernel(h_ref, ada_ref, npre_ref, npost_ref, wup_ref, wdn_ref, o_ref, *, sub):
    a0 = 3 * sub

    def down(j, gate, up):
        mid = (jax.nn.silu(gate) * up).astype(BF16)
        return _dot(mid, wdn_ref[j * FF_CHUNK:(j + 1) * FF_CHUNK, :])

    for r in range(h_ref.shape[0] // ROW_BLOCK):
        rows = slice(r * ROW_BLOCK, (r + 1) * ROW_BLOCK)
        h = h_ref[rows, :]
        u = _prenorm_mod(h, npre_ref[sub:sub + 1, :], ada_ref[a0:a0 + 1, :], ada_ref[a0 + 1:a0 + 2, :])
        u = u.astype(BF16)

        def gate_up(j, u=u):
            lo = j * FF_CHUNK
            return (_dot(u, wup_ref[:, lo:lo + FF_CHUNK].astype(BF16)),
                    _dot(u, wup_ref[:, D_FF + lo:D_FF + lo + FF_CHUNK].astype(BF16)))

        acc = None
        pending = []
        for j in range(FF_NCHUNK + FFN_LOOKAHEAD):
            if j < FF_NCHUNK:
                pending.append((j,) + gate_up(j))
            if j >= FFN_LOOKAHEAD:
                d = down(*pending.pop(0))
                acc = d if acc is None else acc + d
        post = _rms(acc) * npost_ref[sub:sub + 1, :]
        o_ref[rows, :] = h + (0.5 * ada_ref[a0 + 2:a0 + 3, :]) * post


def _ffn_call(h, ada4, norm_pre, norm_post, wup, wdn, layer, sub):
    bsz, seq, _ = h.shape
    tm = TOKEN_TILE
    tok = pl.BlockSpec((None, tm, D_MODEL), lambda b, i: (b, i, 0))
    return pl.pallas_call(
        functools.partial(_ffn_kernel, sub=sub),
        grid=(bsz, seq // tm),
        in_specs=[
            tok,
            pl.BlockSpec((None, None, ADA_CHUNKS, D_MODEL), lambda b, i: (layer, b, 0, 0)),
            pl.BlockSpec((None, 3, D_MODEL), lambda b, i: (layer, 0, 0)),
            pl.BlockSpec((None, 3, D_MODEL), lambda b, i: (layer, 0, 0)),
            _resident((None, D_MODEL, 2 * D_FF), lambda b, i: (layer, 0, 0)),
            _resident((None, D_FF, D_MODEL), lambda b, i: (layer, 0, 0)),
        ],
        out_specs=tok,
        out_shape=jax.ShapeDtypeStruct(h.shape, F32),
        compiler_params=_params("arbitrary", "arbitrary"),
        name=f"ffn{sub // 2 + 1}",
    )(h, ada4, norm_pre, norm_post, wup, wdn)


def _log_sigmoid(z):
    return jnp.minimum(z, 0.0) - jnp.log1p(jnp.exp(-jnp.abs(z)))


def _inproj_kernel(h_ref, ada_ref, npre_ref, wgla_ref, wa_ref, walpha_ref, balpha_ref, wdiff_ref, ws5_ref,
                   gq_ref, gk_ref, gv_ref, gr_ref, la_ref, dq_ref, dk_ref, dv_ref, su_ref):
    for r in range(h_ref.shape[0] // ROW_BLOCK):
        rows = slice(r * ROW_BLOCK, (r + 1) * ROW_BLOCK)
        u = _prenorm_mod(h_ref[rows, :], npre_ref[1:2, :], ada_ref[3:4, :], ada_ref[4:5, :]).astype(BF16)
        pg = _dot(u, wgla_ref[...])
        gq_ref[rows, :] = pg[:, :GLA_QK_W]
        gk_ref[rows, :] = pg[:, GLA_QK_W:2 * GLA_QK_W]
        gv_ref[rows, :] = pg[:, 2 * GLA_QK_W:2 * GLA_QK_W + GLA_V_W].astype(BF16)
        gr_ref[rows, :] = pg[:, 2 * GLA_QK_W + GLA_V_W:]
        a_low = _dot(u, wa_ref[...]).astype(BF16)
        z = _dot(a_low, walpha_ref[...]) + balpha_ref[...]
        la_ref[rows, :] = _log_sigmoid(z) / GLA_TAU
        pd = _dot(u, wdiff_ref[...])
        dq_ref[rows, :] = pd[:, :DIFF_QK_W].astype(BF16)
        dk_ref[rows, :] = pd[:, DIFF_QK_W:2 * DIFF_QK_W].astype(BF16)
        dv_ref[rows, :] = pd[:, 2 * DIFF_QK_W:].astype(BF16)
        su_ref[rows, :] = _dot(u, ws5_ref[...]).astype(BF16)


def _inproj_call(h, ada4, norm_pre, w_mix, walpha, balpha, layer):
    bsz, seq, _ = h.shape
    tm = TOKEN_TILE

    def tok(w):
        return pl.BlockSpec((None, tm, w), lambda b, i: (b, i, 0))

    def sds(w, dt):
        return jax.ShapeDtypeStruct((bsz, seq, w), dt)

    def res(arr):
        nd = arr.ndim - 1
        return _resident((None,) + arr.shape[1:], lambda b, i: (layer,) + (0,) * nd)

    def mix(group):
        width, block = MIX_COLS[group]
        return _resident((None, D_MODEL, width), lambda b, i: (layer, 0, block))

    return pl.pallas_call(
        _inproj_kernel,
        grid=(bsz, seq // tm),
        in_specs=[
            tok(D_MODEL),
            pl.BlockSpec((None, None, ADA_CHUNKS, D_MODEL), lambda b, i: (layer, b, 0, 0)),
            pl.BlockSpec((None, 3, D_MODEL), lambda b, i: (layer, 0, 0)),
            mix("gla"), mix("alpha"), res(walpha), res(balpha), mix("diff"), mix("s5"),
        ],
        out_specs=[tok(GLA_QK_W), tok(GLA_QK_W), tok(GLA_V_W), tok(GLA_V_W), tok(GLA_QK_W),
                   tok(DIFF_QK_W), tok(DIFF_QK_W), tok(DIFF_V_W), tok(S5_WIDTH)],
        out_shape=[sds(GLA_QK_W, F32), sds(GLA_QK_W, F32), sds(GLA_V_W, BF16), sds(GLA_V_W, F32),
                   sds(GLA_QK_W, F32), sds(DIFF_QK_W, BF16), sds(DIFF_QK_W, BF16), sds(DIFF_V_W, BF16),
                   sds(S5_WIDTH, BF16)],
        compiler_params=_params("arbitrary", "arbitrary"),
        name="mixer_in_proj",
    )(h, ada4, norm_pre, w_mix, w_mix, walpha, balpha, w_mix, w_mix)


def _cumsum_rows(tri, x):
    hi = x.astype(BF16)
    r1 = x - hi.astype(F32)
    mid = r1.astype(BF16)
    lo = (r1 - mid.astype(F32)).astype(BF16)
    return _dot(tri, hi) + _dot(tri, mid) + _dot(tri, lo)


def _gla_kernel(q_ref, k_ref, v_ref, r_ref, la_ref, g_ref, o_ref, s_ref, *, n_pairs):
    @pl.when(pl.program_id(1) == 0)
    def _():
        s_ref[...] = jnp.zeros_like(s_ref)

    row = lax.broadcasted_iota(jnp.int32, (GLA_PAIR, GLA_PAIR), 0)
    col = lax.broadcasted_iota(jnp.int32, (GLA_PAIR, GLA_PAIR), 1)
    causal = (col <= row) & ((row // GLA_CHUNK) == (col // GLA_CHUNK))
    tri = jnp.where(causal, 1.0, 0.0).astype(BF16)
    qlane = lax.broadcasted_iota(jnp.int32, (GLA_PAIR, GLA_QK_W), 1)
    tlane = lax.broadcasted_iota(jnp.int32, (GLA_QK_W, GLA_PAIR), 1)
    first = tlane < GLA_CHUNK

    def decays(p):
        rows = slice(p * GLA_PAIR, (p + 1) * GLA_PAIR)
        b = _cumsum_rows(tri, la_ref[rows, :])
        b_t = b.T
        qd = (q_ref[rows, :] * (GLA_DK ** -0.5)) * jnp.exp(b)
        k_t = k_ref[rows, :].T
        bl0 = b_t[:, GLA_CHUNK - 1:GLA_CHUNK]
        bl1 = b_t[:, GLA_PAIR - 1:GLA_PAIR]
        kdec_t = k_t * jnp.exp(jnp.where(first, bl0, bl1) - b_t)
        return dict(
            qh=[jnp.where(qlane // GLA_DK == h, qd, 0.0).astype(BF16) for h in range(GLA_HEADS)],
            kd_t=(k_t * jnp.exp(-b_t)).astype(BF16),
            kdec0=jnp.where(first, kdec_t, 0.0).astype(BF16),
            kdec1=jnp.where(first, 0.0, kdec_t).astype(BF16),
            dec0=jnp.exp(bl0), dec1=jnp.exp(bl1))

    def products(p, d, s0):
        v = v_ref[p * GLA_PAIR:(p + 1) * GLA_PAIR, :]
        ds0, ds1 = [], []
        for h in range(GLA_HEADS):
            hr = slice(h * GLA_DK, (h + 1) * GLA_DK)
            vh = v[:, h * GLA_DV:(h + 1) * GLA_DV]
            ds0.append(_dot(d["kdec0"][hr, :], vh))
            ds1.append(_dot(d["kdec1"][hr, :], vh))
        att = [jnp.where(causal, _dot(qh, d["kd_t"]), 0.0).astype(BF16) for qh in d["qh"]]
        s1 = d["dec0"] * s0 + jnp.concatenate(ds0, axis=0)
        s2 = d["dec1"] * s1 + jnp.concatenate(ds1, axis=0)
        s0b = s0.astype(BF16)
        s1b = s1.astype(BF16)
        inter = [jnp.concatenate([_dot(qh[:GLA_CHUNK], s0b), _dot(qh[GLA_CHUNK:], s1b)], axis=0)
                 for qh in d["qh"]]
        return s2, (v, att, inter)

    def outputs(p, v, att, inter):
        rows = slice(p * GLA_PAIR, (p + 1) * GLA_PAIR)
        r = r_ref[rows, :]
        for h in range(GLA_HEADS):
            vc = slice(h * GLA_DV, (h + 1) * GLA_DV)
            o = _rms(_dot(att[h], v[:, vc]) + inter[h]) * g_ref[:, vc]
            o_ref[rows, vc] = (o * jax.nn.silu(r[:, vc])).astype(BF16)

    state = s_ref[...]
    dec, prod = {}, {}
    for i in range(n_pairs + 2):
        if i < n_pairs:
            dec[i] = decays(i)
        if 0 <= i - 1 < n_pairs:
            state, prod[i - 1] = products(i - 1, dec.pop(i - 1), state)
        if 0 <= i - 2 < n_pairs:
            outputs(i - 2, *prod.pop(i - 2))
    s_ref[...] = state


def _gla_call(gq, gk, gv, gr, la, gla_norm, layer):
    bsz, seq, _ = gq.shape
    tl = GLA_TILE

    def tok(w):
        return pl.BlockSpec((None, tl, w), lambda b, i: (b, i, 0))

    return pl.pallas_call(
        functools.partial(_gla_kernel, n_pairs=tl // GLA_PAIR),
        grid=(bsz, seq // tl),
        in_specs=[tok(GLA_QK_W), tok(GLA_QK_W), tok(GLA_V_W), tok(GLA_V_W), tok(GLA_QK_W),
                  pl.BlockSpec((None, 1, GLA_V_W), lambda b, i: (layer, 0, 0))],
        out_specs=tok(GLA_V_W),
        out_shape=jax.ShapeDtypeStruct((bsz, seq, GLA_V_W), BF16),
        scratch_shapes=[pltpu.VMEM((GLA_QK_W, GLA_DV), F32)],
        compiler_params=_params("arbitrary", "arbitrary"),
        name="gla_mixer",
    )(gq, gk, gv, gr, la, gla_norm.reshape(DEPTH, 1, GLA_V_W))


def _t5_bucket_tiles():
    t = ATT_TILE
    r = np.arange(t)[:, None]
    c = np.arange(t)[None, :]
    max_exact = NUM_BUCKETS // 2
    tiles = []
    for off in range(3):
        dist = off * t + r - c
        d = np.maximum(dist, 1).astype(np.float32)
        large = max_exact + (np.log(d / np.float32(max_exact)) / np.float32(math.log(MAX_DISTANCE / max_exact))
                             * np.float32(NUM_BUCKETS - max_exact)).astype(np.int32)
        large = np.minimum(large, NUM_BUCKETS - 1)
        bucket = np.where(dist < max_exact, dist, large)
        tiles.append(np.where(dist >= 0, bucket, -1).T)
    far = tiles[2]
    assert (far == NUM_BUCKETS - 1).all()
    return np.stack(tiles).astype(np.int32)


def _bias_kernel(rel_ref, bucket_ref, o_ref):
    m = pl.program_id(0)
    bucket = bucket_ref[...]
    acc = jnp.full(bucket.shape, -jnp.inf, F32)
    for kk in range(NUM_BUCKETS):
        acc = jnp.where(bucket == kk, rel_ref[kk, m], acc)
    o_ref[...] = acc


def _bias_call(rel_bias):
    n_maps = 2 * DIFF_HEADS
    t = ATT_TILE
    return pl.pallas_call(
        _bias_kernel,
        grid=(n_maps,),
        in_specs=[pl.BlockSpec(memory_space=pltpu.SMEM),
                  pl.BlockSpec((3, t, t), lambda m: (0, 0, 0))],
        out_specs=pl.BlockSpec((None, 3, t, t), lambda m: (m, 0, 0, 0)),
        out_shape=jax.ShapeDtypeStruct((n_maps, 3, t, t), F32),
        compiler_params=_params("arbitrary"),
        name="t5_bias_tiles",
    )(rel_bias, jnp.asarray(_t5_bucket_tiles()))


def _attention_steps(load_q, load_k, v_all, bias_ref, lam_ref, g_ref, o_ref, lam_init, n_tiles):
    t = ATT_TILE
    scale = DIFF_DQK ** -0.5
    lv = lam_ref[...]
    lam = (jnp.exp(jnp.sum(lv[0:1] * lv[1:2], axis=-1, keepdims=True))
           - jnp.exp(jnp.sum(lv[2:3] * lv[3:4], axis=-1, keepdims=True)) + lam_init)
    v_t = jnp.concatenate([v_all.astype(F32).T, jnp.ones((16, v_all.shape[0]), F32)], axis=0).astype(BF16)
    lane = lax.broadcasted_iota(jnp.int32, (t, 2 * DIFF_DQK), 1)
    far = [bias_ref[mp, 2, 0:1, 0:1] for mp in range(2)]

    tasks = [(qi, kj, mp) for qi in range(n_tiles) for kj in range(qi + 1) for mp in range(2)]
    masked_q = {}
    state = {}
    done = {}

    def logits(qi, kj, mp):
        if qi not in masked_q:
            q = load_q(slice(qi * t, (qi + 1) * t))
            zero = jnp.zeros_like(q)
            masked_q[qi] = (jnp.where(lane < DIFF_DQK, q, zero) * scale,
                            jnp.where(lane < DIFF_DQK, zero, q) * scale)
        s = _dot_nt(load_k(slice(kj * t, (kj + 1) * t)), masked_q[qi][mp])
        return s + bias_ref[mp, qi - kj] if qi - kj < 2 else s

    def accumulate(qi, kj, mp, s):
        shift = 0.0 if qi - kj < 2 else far[mp]
        m_blk = jnp.max(s, axis=0, keepdims=True) + shift
        prev = state.get((qi, mp))
        m_new = m_blk if prev is None else jnp.maximum(prev[0], m_blk)
        p = jnp.exp((s - (m_new - shift)).astype(BF16))
        pv = _dot(v_t[:, kj * t:(kj + 1) * t], p)
        if prev is None:
            state[(qi, mp)] = (m_new, pv)
        else:
            state[(qi, mp)] = (m_new, jnp.exp(prev[0] - m_new) * prev[1] + pv)
        if kj == qi:
            _, acc = state.pop((qi, mp))
            done.setdefault(qi, {})[mp] = acc[:DIFF_DV] / acc[DIFF_DV:DIFF_DV + 1]
            if len(done[qi]) == 2:
                o = done[qi][0] - lam * done[qi][1]
                o = o * lax.rsqrt(jnp.mean(o * o, axis=0, keepdims=True) + EPS)
                o_ref[qi * t:(qi + 1) * t, :] = (o.T * g_ref[...] * (1.0 - lam_init)).astype(BF16)
                del done[qi]

    pending = []
    for task in tasks:
        pending.append((task, logits(*task)))
        if len(pending) > ATT_LOOKAHEAD:
            ready, s = pending.pop(0)
            accumulate(*ready, s)
        yield
    for ready, s in pending:
        accumulate(*ready, s)
        yield


def _diff_kernel(q_ref, k_ref, v_ref, bias_ref, lam_ref, g_ref, o_ref, *, lam_init, n_tiles):
    for _ in _attention_steps(lambda sl: q_ref[sl, :], lambda sl: k_ref[sl, :], v_ref[...],
                              bias_ref, lam_ref, g_ref, o_ref, lam_init, n_tiles):
        pass


def _diff_call(dq, dk, dv, bias, diff_lambda, diff_norm, layer):
    bsz, seq, _ = dq.shape
    t = ATT_TILE
    lam_init = 0.8 - 0.6 * math.exp(-0.3 * layer)
    return pl.pallas_call(
        functools.partial(_diff_kernel, lam_init=lam_init, n_tiles=seq // t),
        grid=(bsz, DIFF_HEADS),
        in_specs=[
            pl.BlockSpec((None, seq, 2 * DIFF_DQK), lambda b, h: (b, 0, h)),
            pl.BlockSpec((None, seq, 2 * DIFF_DQK), lambda b, h: (b, 0, h)),
            pl.BlockSpec((None, seq, DIFF_DV), lambda b, h: (b, 0, h)),
            pl.BlockSpec((2, 3, t, t), lambda b, h: (h, 0, 0, 0)),
            pl.BlockSpec((None, 4, DIFF_DQK), lambda b, h: (layer, 0, 0)),
            pl.BlockSpec((None, 1, DIFF_DV), lambda b, h: (layer, 0, h)),
        ],
        out_specs=pl.BlockSpec((None, seq, DIFF_DV), lambda b, h: (b, 0, h)),
        out_shape=jax.ShapeDtypeStruct((bsz, seq, DIFF_V_W), BF16),
        compiler_params=_params("arbitrary", "arbitrary"),
        name="diff_attention",
    )(dq, dk, dv, bias, diff_lambda, diff_norm.reshape(DEPTH, 1, DIFF_V_W))


def _s5_prep_kernel(lre_ref, lim_ref, lstep_ref, bre_ref, bim_ref, cre_ref, cim_ref,
                    are_ref, aim_ref, bmat_ref, cmat_ref):
    lr = lre_ref[...]
    li = lim_ref[...]
    dt = jnp.exp(lstep_ref[...])
    mag = jnp.exp(lr * dt)
    a_re = mag * jnp.cos(li * dt)
    a_im = mag * jnp.sin(li * dt)
    are_ref[...] = jnp.broadcast_to(a_re, are_ref.shape)
    aim_ref[...] = jnp.broadcast_to(a_im, aim_ref.shape)
    nr, ni = a_re - 1.0, a_im
    den = lr * lr + li * li
    f_re = (nr * lr + ni * li) / den
    f_im = (ni * lr - nr * li) / den
    ns = S5_BLOCK_STATE
    for gb in range(S5_GROUP_BLOCKS):
        fr = f_re[:, gb * ns:(gb + 1) * ns]
        fi = f_im[:, gb * ns:(gb + 1) * ns]
        br = bre_ref[gb]
        bi = bim_ref[gb]
        bmat_ref[gb, :, :ns] = (fr * br - fi * bi).astype(BF16)
        bmat_ref[gb, :, ns:] = (fr * bi + fi * br).astype(BF16)
        cmat_ref[gb, :ns, :] = cre_ref[gb].astype(BF16)
        cmat_ref[gb, ns:, :] = (-cim_ref[gb]).astype(BF16)


def _s5_prep_call(lam_re, lam_im, log_step, b_re, b_im, c_re, c_im, bsz):
    d = DEPTH
    nb, ch, ns = S5_GROUP_BLOCKS, S5_BLOCK_CH, S5_BLOCK_STATE
    gpb = S5_GROUPS // nb
    eye = jnp.eye(gpb, dtype=F32)

    def in_blocks(b):
        t = b.reshape(d, nb, gpb, S5_STATE, S5_GROUP_CH).transpose(0, 1, 2, 4, 3)
        return (t[:, :, :, :, None, :] * eye[None, None, :, None, :, None]).reshape(d, nb, ch, ns)

    def out_blocks(c):
        t = c.reshape(d, nb, gpb, S5_GROUP_CH, S5_STATE).transpose(0, 1, 4, 2, 3)
        return (t[:, :, None, :, :, :] * eye[None, None, :, None, :, None]).reshape(d, nb, ns, ch)

    def row(x):
        return x.reshape(d, 1, S5_NSTATE)

    lstep = jnp.broadcast_to(log_step[:, :, None], (d, S5_GROUPS, S5_STATE))

    def full(shape):
        nd = len(shape)
        return pl.BlockSpec((None,) + shape, lambda l: (l,) + (0,) * nd)

    return pl.pallas_call(
        _s5_prep_kernel,
        grid=(d,),
        in_specs=[full((1, S5_NSTATE))] * 3 + [full((nb, ch, ns))] * 2 + [full((nb, ns, ch))] * 2,
        out_specs=[full((bsz, S5_NSTATE)), full((bsz, S5_NSTATE)),
                   full((nb, ch, 2 * ns)), full((nb, 2 * ns, ch))],
        out_shape=[jax.ShapeDtypeStruct((d, bsz, S5_NSTATE), F32), jax.ShapeDtypeStruct((d, bsz, S5_NSTATE), F32),
                   jax.ShapeDtypeStruct((d, nb, ch, 2 * ns), BF16), jax.ShapeDtypeStruct((d, nb, 2 * ns, ch), BF16)],
        compiler_params=_params("arbitrary"),
        name="s5_discretize",
    )(row(lam_re), row(lam_im), row(lstep), in_blocks(b_re), in_blocks(b_im), out_blocks(c_re), out_blocks(c_im))


def _s5_kernel(u_ref, are_ref, aim_ref, bmat_ref, cmat_ref, d_ref, wglu_ref, bglu_ref, o_ref,
               tb_ref, xr_ref, xi_ref, sr_ref, si_ref, *, bsz, n_steps):
    @pl.when(pl.program_id(0) == 0)
    def _():
        sr_ref[...] = jnp.zeros_like(sr_ref)
        si_ref[...] = jnp.zeros_like(si_ref)

    ns, ch, nb = S5_BLOCK_STATE, S5_BLOCK_CH, S5_GROUP_BLOCKS
    for b in range(bsz):
        ub = u_ref[b].astype(F32)
        for gb in range(nb):
            tb_ref[gb, pl.ds(b, n_steps, stride=bsz), :] = ub[:, gb * ch:(gb + 1) * ch]
    u = [tb_ref[gb] for gb in range(nb)]

    def drive(gb):
        return _dot(u[gb].astype(BF16), bmat_ref[gb])

    def scan(gb, bu):
        cols = slice(gb * ns, (gb + 1) * ns)
        ar, ai = are_ref[:, cols], aim_ref[:, cols]
        pr, pi = sr_ref[:, cols], si_ref[:, cols]
        for t in range(n_steps):
            rows = slice(t * bsz, (t + 1) * bsz)
            pr, pi = ar * pr - ai * pi + bu[rows, :ns], ar * pi + ai * pr + bu[rows, ns:]
            xr_ref[rows, cols] = pr
            xi_ref[rows, cols] = pi
        sr_ref[:, cols] = pr
        si_ref[:, cols] = pi

    def readout(gb):
        cols = slice(gb * ns, (gb + 1) * ns)
        return (_dot(xr_ref[:, cols].astype(BF16), cmat_ref[gb, :ns, :])
                + _dot(xi_ref[:, cols].astype(BF16), cmat_ref[gb, ns:, :]))

    bu, ys = {0: drive(0)}, []
    for gb in range(nb):
        if gb + 1 < nb:
            bu[gb + 1] = drive(gb + 1)
        scan(gb, bu.pop(gb))
        if gb >= 1:
            ys.append(readout(gb - 1))
    ys.append(readout(nb - 1))

    y = jax.nn.gelu(jnp.concatenate(ys, axis=-1) + d_ref[...] * jnp.concatenate(u, axis=-1))
    z = _dot(y.astype(BF16), wglu_ref[...]) + bglu_ref[...]
    out = z[:, :S5_WIDTH] * jax.nn.sigmoid(z[:, S5_WIDTH:])
    for gb in range(nb):
        tb_ref[gb] = out[:, gb * ch:(gb + 1) * ch]
    for b in range(bsz):
        for gb in range(nb):
            o_ref[b, :, gb * ch:(gb + 1) * ch] = tb_ref[gb, pl.ds(b, n_steps, stride=bsz), :].astype(BF16)


def _s5_call(su, a_re, a_im, bmat, cmat, d_skip, wglu, bglu, layer):
    bsz, seq, _ = su.shape
    tl = S5_TIME_TILE
    tr = tl * bsz
    nb, ch, ns = S5_GROUP_BLOCKS, S5_BLOCK_CH, S5_BLOCK_STATE

    def res(shape):
        nd = len(shape)
        return _resident((None,) + shape, lambda i: (layer,) + (0,) * nd)

    tok = pl.BlockSpec((bsz, tl, S5_WIDTH), lambda i: (0, i, 0))
    return pl.pallas_call(
        functools.partial(_s5_kernel, bsz=bsz, n_steps=tl),
        grid=(seq // tl,),
        in_specs=[tok,
                  res((bsz, S5_NSTATE)), res((bsz, S5_NSTATE)),
                  res((nb, ch, 2 * ns)), res((nb, 2 * ns, ch)),
                  res((1, S5_WIDTH)), res((S5_WIDTH, 2 * S5_WIDTH)), res((1, 2 * S5_WIDTH))],
        out_specs=tok,
        out_shape=jax.ShapeDtypeStruct((bsz, seq, S5_WIDTH), BF16),
        scratch_shapes=[pltpu.VMEM((nb, tr, ch), F32),
                        pltpu.VMEM((tr, S5_NSTATE), F32), pltpu.VMEM((tr, S5_NSTATE), F32),
                        pltpu.VMEM((bsz, S5_NSTATE), F32), pltpu.VMEM((bsz, S5_NSTATE), F32)],
        compiler_params=_params("arbitrary"),
        name="s5_mixer",
    )(su, a_re, a_im, bmat, cmat, d_skip, wglu, bglu)


def _merge_kernel(h_ref, ada_ref, npre_ref, npost_ref, yg_ref, yd_ref, ys_ref, wgate_ref, wbr_ref, wout_ref, o_ref):
    for r in range(h_ref.shape[0] // ROW_BLOCK):
        rows = slice(r * ROW_BLOCK, (r + 1) * ROW_BLOCK)
        h = h_ref[rows, :]
        u = _prenorm_mod(h, npre_ref[1:2, :], ada_ref[3:4, :], ada_ref[4:5, :]).astype(BF16)
        merged = None
        for i, y_ref in enumerate((yg_ref, yd_ref, ys_ref)):
            gate = jax.nn.sigmoid(_dot(u, wgate_ref[:, i * D_MODEL:(i + 1) * D_MODEL]))
            term = gate * _dot(y_ref[rows, :], wbr_ref[i])
            merged = term if merged is None else merged + term
        y = _dot(merged.astype(BF16), wout_ref[...])
        o_ref[rows, :] = h + ada_ref[5:6, :] * (_rms(y) * npost_ref[1:2, :])


def _merge_call(h, ada4, norm_pre, norm_post, y_gla, y_diff, y_s5, w_mix, wbr, wout, layer):
    bsz, seq, _ = h.shape
    tm = TOKEN_TILE

    def tok(w):
        return pl.BlockSpec((None, tm, w), lambda b, i: (b, i, 0))

    return pl.pallas_call(
        _merge_kernel,
        grid=(bsz, seq // tm),
        in_specs=[
            tok(D_MODEL),
            pl.BlockSpec((None, None, ADA_CHUNKS, D_MODEL), lambda b, i: (layer, b, 0, 0)),
            pl.BlockSpec((None, 3, D_MODEL), lambda b, i: (layer, 0, 0)),
            pl.BlockSpec((None, 3, D_MODEL), lambda b, i: (layer, 0, 0)),
            tok(GLA_V_W), tok(DIFF_V_W), tok(S5_WIDTH),
            _resident((None, D_MODEL, MIX_COLS["gate"][0]), lambda b, i: (layer, 0, MIX_COLS["gate"][1])),
            _resident((None, N_BRANCH, S5_WIDTH, D_MODEL), lambda b, i: (layer, 0, 0, 0)),
            _resident((None, D_MODEL, D_MODEL), lambda b, i: (layer, 0, 0)),
        ],
        out_specs=tok(D_MODEL),
        out_shape=jax.ShapeDtypeStruct(h.shape, F32),
        compiler_params=_params("arbitrary", "arbitrary"),
        name="mixer_merge",
    )(h, ada4, norm_pre, norm_post, y_gla, y_diff, y_s5, w_mix, wbr, wout)


def kernel(x, c, rel_bias, w_ada, b_ada, norm_pre, norm_post, ffn1_w_up, ffn1_w_down, ffn2_w_up, ffn2_w_down,
           w_in, gla_w_alpha, gla_b_alpha, gla_norm, diff_lambda, diff_norm, s5_lam_re, s5_lam_im, s5_log_step,
           s5_b_re, s5_b_im, s5_c_re, s5_c_im, s5_d, s5_w_glu, s5_b_glu, w_branch, w_out):
    bsz, seq, _ = x.shape
    assert bsz == SUBLANES, "the S5 scan keeps the batch on the sublane axis"

    f1 = (ffn1_w_up, ffn1_w_down.astype(BF16))
    f2 = (ffn2_w_up, ffn2_w_down.astype(BF16))
    o_alpha = 2 * GLA_QK_W + 2 * GLA_V_W
    o_diff = o_alpha + GLA_LOWRANK
    o_s5 = o_diff + 2 * DIFF_QK_W + DIFF_V_W
    o_gate = o_s5 + S5_WIDTH
    w_mix = jnp.concatenate(
        [w_in[:, :, o_gate:], w_in[:, :, :o_alpha], w_in[:, :, o_diff:o_s5], w_in[:, :, o_s5:o_gate],
         w_in[:, :, o_alpha:o_diff], jnp.zeros((DEPTH, D_MODEL, LANES - GLA_LOWRANK), F32)], axis=-1).astype(BF16)
    walpha = jnp.pad(gla_w_alpha.astype(BF16), ((0, 0), (0, LANES - GLA_LOWRANK), (0, 0)))
    balpha = gla_b_alpha.reshape(DEPTH, 1, GLA_QK_W)
    wbr = w_branch.astype(BF16)
    wout = w_out.astype(BF16)
    wglu = s5_w_glu.astype(BF16)
    bglu = s5_b_glu.reshape(DEPTH, 1, 2 * S5_WIDTH)
    d_skip = s5_d.reshape(DEPTH, 1, S5_WIDTH)

    ada4 = _ada_call(c, w_ada, b_ada).reshape(DEPTH, bsz, ADA_CHUNKS, D_MODEL)
    bias = _bias_call(rel_bias)
    a_re, a_im, bmat, cmat = _s5_prep_call(s5_lam_re, s5_lam_im, s5_log_step, s5_b_re, s5_b_im,
                                           s5_c_re, s5_c_im, bsz)

    h = x
    for layer in range(DEPTH):
        h = _ffn_call(h, ada4, norm_pre, norm_post, *f1, layer, 0)
        gq, gk, gv, gr, la, dq, dk, dv, su = _inproj_call(h, ada4, norm_pre, w_mix, walpha, balpha, layer)
        y_gla = _gla_call(gq, gk, gv, gr, la, gla_norm, layer)
        y_diff = _diff_call(dq, dk, dv, bias, diff_lambda, diff_norm, layer)
        y_s5 = _s5_call(su, a_re, a_im, bmat, cmat, d_skip, wglu, bglu, layer)
        h = _merge_call(h, ada4, norm_pre, norm_post, y_gla, y_diff, y_s5, w_mix, wbr, wout, layer)
        h = _ffn_call(h, ada4, norm_pre, norm_post, *f2, layer, 2)
    return h
```

```python
import functools
import math

import numpy as np
import jax
import jax.numpy as jnp
from jax import lax
from jax.experimental import pallas as pl
from jax.experimental.pallas import tpu as pltpu

F32 = jnp.float32
BF16 = jnp.bfloat16

D_MODEL = 1024
DEPTH = 2
D_FF = 2816
ADA_CHUNKS = 9
EPS = 1e-6
LOG2E = math.log2(math.e)

GLA_HEADS = 4
GLA_DK = 64
GLA_DV = 128
GLA_LOWRANK = 16
GLA_TAU = 16.0
GLA_CHUNK = 64
GLA_QK_W = GLA_HEADS * GLA_DK
GLA_V_W = GLA_HEADS * GLA_DV

DIFF_HEADS = 4
DIFF_DQK = 64
DIFF_DV = 128
DIFF_QK_W = DIFF_HEADS * 2 * DIFF_DQK
DIFF_V_W = DIFF_HEADS * DIFF_DV
NUM_BUCKETS = 32
MAX_DISTANCE = 128

S5_GROUPS = 32
S5_GROUP_CH = 16
S5_STATE = 64
S5_WIDTH = S5_GROUPS * S5_GROUP_CH
S5_NSTATE = S5_GROUPS * S5_STATE
S5_GROUP_BLOCKS = 4
S5_BLOCK_CH = S5_WIDTH // S5_GROUP_BLOCKS
S5_BLOCK_STATE = S5_NSTATE // S5_GROUP_BLOCKS

N_BRANCH = 3

SUBLANES = 8
LANES = 128
VMEM_LIMIT_BYTES = 56 * 1024 * 1024

TOKEN_TILE = 1024
ROW_BLOCK = 512
FF_CHUNK = 256
FF_NCHUNK = D_FF // FF_CHUNK
FFN_LOOKAHEAD = 1
GLA_PAIR = 2 * GLA_CHUNK
GLA_TILE = 512
ATT_TILE = 256
ATT_LOOKAHEAD = 4
S5_TIME_TILE = 64
ADA_COL_TILE = 1536
MIX_COLS = {"gate": (N_BRANCH * D_MODEL, 0), "gla": (2 * GLA_QK_W + 2 * GLA_V_W, 2),
            "diff": (2 * DIFF_QK_W + DIFF_V_W, 3), "s5": (S5_WIDTH, 12), "alpha": (LANES, 52)}


def _dot(a, b):
    return jnp.dot(a, b, preferred_element_type=F32)


def _dot_nt(a, b):
    return lax.dot_general(a, b, (((1,), (1,)), ((), ())), preferred_element_type=F32)


def _rms(x):
    return x * lax.rsqrt(jnp.mean(x * x, axis=-1, keepdims=True) + EPS)


def _prenorm_mod(h, g, shift, scale):
    return (_rms(h) * g) * (1.0 + scale) + shift


def _params(*sem):
    return pltpu.CompilerParams(dimension_semantics=sem, vmem_limit_bytes=VMEM_LIMIT_BYTES)


def _resident(shape, index_map):
    return pl.BlockSpec(shape, index_map, pipeline_mode=pl.Buffered(1))


def _ada_kernel(c_ref, w_ref, b_ref, o_ref):
    cond = jax.nn.silu(c_ref[...]).astype(BF16)
    o_ref[...] = _dot(cond, w_ref[...].astype(BF16)) + b_ref[...]


def _ada_call(c, w_ada, b_ada):
    bsz = c.shape[0]
    n = ADA_CHUNKS * D_MODEL
    return pl.pallas_call(
        _ada_kernel,
        grid=(DEPTH, n // ADA_COL_TILE),
        in_specs=[
            pl.BlockSpec((bsz, D_MODEL), lambda l, j: (0, 0)),
            pl.BlockSpec((None, D_MODEL, ADA_COL_TILE), lambda l, j: (l, 0, j)),
            pl.BlockSpec((None, 1, ADA_COL_TILE), lambda l, j: (l, 0, j)),
        ],
        out_specs=pl.BlockSpec((None, bsz, ADA_COL_TILE), lambda l, j: (l, 0, j)),
        out_shape=jax.ShapeDtypeStruct((DEPTH, bsz, n), F32),
        compiler_params=_params("arbitrary", "arbitrary"),
        name="ada_proj",
    )(c, w_ada, b_ada.reshape(DEPTH, 1, n))


def _ffn_kernel(h_ref, ada_ref, npre_ref, npost_ref, wup_ref, wdn_ref, o_ref, *, sub):
    a0 = 3 * sub

    def down(j, gate, up):
        mid = (jax.nn.silu(gate) * up).astype(BF16)
        return _dot(mid, wdn_ref[j * FF_CHUNK:(j + 1) * FF_CHUNK, :])

    for r in range(h_ref.shape[0] // ROW_BLOCK):
        rows = slice(r * ROW_BLOCK, (r + 1) * ROW_BLOCK)
        h = h_ref[rows, :]
        u = _prenorm_mod(h, npre_ref[sub:sub + 1, :], ada_ref[a0:a0 + 1, :], ada_ref[a0 + 1:a0 + 2, :])
        u = u.astype(BF16)

        def gate_up(j, u=u):
            lo = j * FF_CHUNK
            return (_dot(u, wup_ref[:, lo:lo + FF_CHUNK].astype(BF16)),
                    _dot(u, wup_ref[:, D_FF + lo:D_FF + lo + FF_CHUNK].astype(BF16)))

        acc = None
        pending = []
        for j in range(FF_NCHUNK + FFN_LOOKAHEAD):
            if j < FF_NCHUNK:
                pending.append((j,) + gate_up(j))
            if j >= FFN_LOOKAHEAD:
                d = down(*pending.pop(0))
                acc = d if acc is None else acc + d
        post = _rms(acc) * npost_ref[sub:sub + 1, :]
        o_ref[rows, :] = h + (0.5 * ada_ref[a0 + 2:a0 + 3, :]) * post


def _ffn_call(h, ada4, norm_pre, norm_post, wup, wdn, layer, sub):
    bsz, seq, _ = h.shape
    tm = TOKEN_TILE
    tok = pl.BlockSpec((None, tm, D_MODEL), lambda b, i: (b, i, 0))
    return pl.pallas_call(
        functools.partial(_ffn_kernel, sub=sub),
        grid=(bsz, seq // tm),
        in_specs=[
            tok,
            pl.BlockSpec((None, None, ADA_CHUNKS, D_MODEL), lambda b, i: (layer, b, 0, 0)),
            pl.BlockSpec((None, 3, D_MODEL), lambda b, i: (layer, 0, 0)),
            pl.BlockSpec((None, 3, D_MODEL), lambda b, i: (layer, 0, 0)),
            _resident((None, D_MODEL, 2 * D_FF), lambda b, i: (layer, 0, 0)),
            _resident((None, D_FF, D_MODEL), lambda b, i: (layer, 0, 0)),
        ],
        out_specs=tok,
        out_shape=jax.ShapeDtypeStruct(h.shape, F32),
        compiler_params=_params("arbitrary", "arbitrary"),
        name=f"ffn{sub // 2 + 1}",
    )(h, ada4, norm_pre, norm_post, wup, wdn)


def _log_sigmoid(z):
    return jnp.minimum(z, 0.0) - jnp.log1p(jnp.exp(-jnp.abs(z)))


def _inproj_kernel(h_ref, ada_ref, npre_ref, wgla_ref, wa_ref, walpha_ref, balpha_ref, wdiff_ref, ws5_ref,
                   gq_ref, gk_ref, gv_ref, gr_ref, la_ref, dq_ref, dk_ref, dv_ref, su_ref):
    for r in range(h_ref.shape[0] // ROW_BLOCK):
        rows = slice(r * ROW_BLOCK, (r + 1) * ROW_BLOCK)
        u = _prenorm_mod(h_ref[rows, :], npre_ref[1:2, :], ada_ref[3:4, :], ada_ref[4:5, :]).astype(BF16)
        pg = _dot(u, wgla_ref[...])
        gq_ref[rows, :] = pg[:, :GLA_QK_W]
        gk_ref[rows, :] = pg[:, GLA_QK_W:2 * GLA_QK_W]
        gv_ref[rows, :] = pg[:, 2 * GLA_QK_W:2 * GLA_QK_W + GLA_V_W].astype(BF16)
        gr_ref[rows, :] = pg[:, 2 * GLA_QK_W + GLA_V_W:]
        a_low = _dot(u, wa_ref[...]).astype(BF16)
        z = _dot(a_low, walpha_ref[...]) + balpha_ref[...]
        la_ref[rows, :] = _log_sigmoid(z) / GLA_TAU
        pd = _dot(u, wdiff_ref[...])
        dq_ref[rows, :] = (pd[:, :DIFF_QK_W] * (DIFF_DQK ** -0.5 * LOG2E)).astype(BF16)
        dk_ref[rows, :] = pd[:, DIFF_QK_W:2 * DIFF_QK_W].astype(BF16)
        dv_ref[rows, :] = pd[:, 2 * DIFF_QK_W:].astype(BF16)
        su_ref[rows, :] = _dot(u, ws5_ref[...]).astype(BF16)


def _inproj_call(h, ada4, norm_pre, w_mix, walpha, balpha, layer):
    bsz, seq, _ = h.shape
    tm = TOKEN_TILE

    def tok(w):
        return pl.BlockSpec((None, tm, w), lambda b, i: (b, i, 0))

    def sds(w, dt):
        return jax.ShapeDtypeStruct((bsz, seq, w), dt)

    def res(arr):
        nd = arr.ndim - 1
        return _resident((None,) + arr.shape[1:], lambda b, i: (layer,) + (0,) * nd)

    def mix(group):
        width, block = MIX_COLS[group]
        return _resident((None, D_MODEL, width), lambda b, i: (layer, 0, block))

    return pl.pallas_call(
        _inproj_kernel,
        grid=(bsz, seq // tm),
        in_specs=[
            tok(D_MODEL),
            pl.BlockSpec((None, None, ADA_CHUNKS, D_MODEL), lambda b, i: (layer, b, 0, 0)),
            pl.BlockSpec((None, 3, D_MODEL), lambda b, i: (layer, 0, 0)),
            mix("gla"), mix("alpha"), res(walpha), res(balpha), mix("diff"), mix("s5"),
        ],
        out_specs=[tok(GLA_QK_W), tok(GLA_QK_W), tok(GLA_V_W), tok(GLA_V_W), tok(GLA_QK_W),
                   tok(DIFF_QK_W), tok(DIFF_QK_W), tok(DIFF_V_W), tok(S5_WIDTH)],
        out_shape=[sds(GLA_QK_W, F32), sds(GLA_QK_W, F32), sds(GLA_V_W, BF16), sds(GLA_V_W, F32),
                   sds(GLA_QK_W, F32), sds(DIFF_QK_W, BF16), sds(DIFF_QK_W, BF16), sds(DIFF_V_W, BF16),
                   sds(S5_WIDTH, BF16)],
        compiler_params=_params("arbitrary", "arbitrary"),
        name="mixer_in_proj",
    )(h, ada4, norm_pre, w_mix, w_mix, walpha, balpha, w_mix, w_mix)


def _cumsum_rows(tri, x):
    hi = x.astype(BF16)
    r1 = x - hi.astype(F32)
    mid = r1.astype(BF16)
    lo = (r1 - mid.astype(F32)).astype(BF16)
    return _dot(tri, hi) + _dot(tri, mid) + _dot(tri, lo)


def _gla_kernel(q_ref, k_ref, v_ref, r_ref, la_ref, g_ref, o_ref, s_ref, *, n_pairs):
    @pl.when(pl.program_id(1) == 0)
    def _():
        s_ref[...] = jnp.zeros_like(s_ref)

    row = lax.broadcasted_iota(jnp.int32, (GLA_PAIR, GLA_PAIR), 0)
    col = lax.broadcasted_iota(jnp.int32, (GLA_PAIR, GLA_PAIR), 1)
    causal = (col <= row) & ((row // GLA_CHUNK) == (col // GLA_CHUNK))
    tri = jnp.where(causal, 1.0, 0.0).astype(BF16)
    qlane = lax.broadcasted_iota(jnp.int32, (GLA_PAIR, GLA_QK_W), 1)
    tlane = lax.broadcasted_iota(jnp.int32, (GLA_QK_W, GLA_PAIR), 1)
    first = tlane < GLA_CHUNK

    def decays(p):
        rows = slice(p * GLA_PAIR, (p + 1) * GLA_PAIR)
        b = _cumsum_rows(tri, la_ref[rows, :])
        b_t = b.T
        qd = (q_ref[rows, :] * (GLA_DK ** -0.5)) * jnp.exp(b)
        k_t = k_ref[rows, :].T
        bl0 = b_t[:, GLA_CHUNK - 1:GLA_CHUNK]
        bl1 = b_t[:, GLA_PAIR - 1:GLA_PAIR]
        kdec_t = k_t * jnp.exp(jnp.where(first, bl0, bl1) - b_t)
        return dict(
            qh=[jnp.where(qlane // GLA_DK == h, qd, 0.0).astype(BF16) for h in range(GLA_HEADS)],
            kd_t=(k_t * jnp.exp(-b_t)).astype(BF16),
            kdec0=jnp.where(first, kdec_t, 0.0).astype(BF16),
            kdec1=jnp.where(first, 0.0, kdec_t).astype(BF16),
            dec0=jnp.exp(bl0), dec1=jnp.exp(bl1))

    def products(p, d, s0):
        v = v_ref[p * GLA_PAIR:(p + 1) * GLA_PAIR, :]
        ds0, ds1 = [], []
        for h in range(GLA_HEADS):
            hr = slice(h * GLA_DK, (h + 1) * GLA_DK)
            vh = v[:, h * GLA_DV:(h + 1) * GLA_DV]
            ds0.append(_dot(d["kdec0"][hr, :], vh))
            ds1.append(_dot(d["kdec1"][hr, :], vh))
        att = [jnp.where(causal, _dot(qh, d["kd_t"]), 0.0).astype(BF16) for qh in d["qh"]]
        s1 = d["dec0"] * s0 + jnp.concatenate(ds0, axis=0)
        s2 = d["dec1"] * s1 + jnp.concatenate(ds1, axis=0)
        s0b = s0.astype(BF16)
        s1b = s1.astype(BF16)
        inter = [jnp.concatenate([_dot(qh[:GLA_CHUNK], s0b), _dot(qh[GLA_CHUNK:], s1b)], axis=0)
                 for qh in d["qh"]]
        return s2, (v, att, inter)

    def outputs(p, v, att, inter):
        rows = slice(p * GLA_PAIR, (p + 1) * GLA_PAIR)
        r = r_ref[rows, :]
        for h in range(GLA_HEADS):
            vc = slice(h * GLA_DV, (h + 1) * GLA_DV)
            o = _rms(_dot(att[h], v[:, vc]) + inter[h]) * g_ref[:, vc]
            o_ref[rows, vc] = (o * jax.nn.silu(r[:, vc])).astype(BF16)

    state = s_ref[...]
    dec, prod = {}, {}
    for i in range(n_pairs + 2):
        if i < n_pairs:
            dec[i] = decays(i)
        if 0 <= i - 1 < n_pairs:
            state, prod[i - 1] = products(i - 1, dec.pop(i - 1), state)
        if 0 <= i - 2 < n_pairs:
            outputs(i - 2, *prod.pop(i - 2))
    s_ref[...] = state


def _gla_call(gq, gk, gv, gr, la, gla_norm, layer):
    bsz, seq, _ = gq.shape
    tl = GLA_TILE

    def tok(w):
        return pl.BlockSpec((None, tl, w), lambda b, i: (b, i, 0))

    return pl.pallas_call(
        functools.partial(_gla_kernel, n_pairs=tl // GLA_PAIR),
        grid=(bsz, seq // tl),
        in_specs=[tok(GLA_QK_W), tok(GLA_QK_W), tok(GLA_V_W), tok(GLA_V_W), tok(GLA_QK_W),
                  pl.BlockSpec((None, 1, GLA_V_W), lambda b, i: (layer, 0, 0))],
        out_specs=tok(GLA_V_W),
        out_shape=jax.ShapeDtypeStruct((bsz, seq, GLA_V_W), BF16),
        scratch_shapes=[pltpu.VMEM((GLA_QK_W, GLA_DV), F32)],
        compiler_params=_params("arbitrary", "arbitrary"),
        name="gla_mixer",
    )(gq, gk, gv, gr, la, gla_norm.reshape(DEPTH, 1, GLA_V_W))


def _t5_bucket_tiles():
    t = ATT_TILE
    r = np.arange(t)[:, None]
    c = np.arange(t)[None, :]
    max_exact = NUM_BUCKETS // 2
    tiles = []
    for off in range(3):
        dist = off * t + r - c
        d = np.maximum(dist, 1).astype(np.float32)
        large = max_exact + (np.log(d / np.float32(max_exact)) / np.float32(math.log(MAX_DISTANCE / max_exact))
                             * np.float32(NUM_BUCKETS - max_exact)).astype(np.int32)
        large = np.minimum(large, NUM_BUCKETS - 1)
        bucket = np.where(dist < max_exact, dist, large)
        tiles.append(np.where(dist >= 0, bucket, -1).T)
    far = tiles[2]
    assert (far == NUM_BUCKETS - 1).all()
    return np.stack(tiles).astype(np.int32)


def _bias_kernel(rel_ref, bucket_ref, o_ref):
    m = pl.program_id(0)
    bucket = bucket_ref[...]
    acc = jnp.full(bucket.shape, -jnp.inf, F32)
    for kk in range(NUM_BUCKETS):
        acc = jnp.where(bucket == kk, rel_ref[kk, m], acc)
    o_ref[...] = acc * LOG2E


def _bias_call(rel_bias):
    n_maps = 2 * DIFF_HEADS
    t = ATT_TILE
    return pl.pallas_call(
        _bias_kernel,
        grid=(n_maps,),
        in_specs=[pl.BlockSpec(memory_space=pltpu.SMEM),
                  pl.BlockSpec((3, t, t), lambda m: (0, 0, 0))],
        out_specs=pl.BlockSpec((None, 3, t, t), lambda m: (m, 0, 0, 0)),
        out_shape=jax.ShapeDtypeStruct((n_maps, 3, t, t), F32),
        compiler_params=_params("arbitrary"),
        name="t5_bias_tiles",
    )(rel_bias, jnp.asarray(_t5_bucket_tiles()))


def _attention_steps(load_q, load_k, v_all, bias_ref, lam_ref, g_ref, o_ref, lam_init, n_tiles):
    t = ATT_TILE
    lv = lam_ref[...]
    lam = (jnp.exp(jnp.sum(lv[0:1] * lv[1:2], axis=-1, keepdims=True))
           - jnp.exp(jnp.sum(lv[2:3] * lv[3:4], axis=-1, keepdims=True)) + lam_init)
    v_t = jnp.concatenate([v_all.astype(F32).T, jnp.ones((16, v_all.shape[0]), F32)], axis=0).astype(BF16)
    lane = lax.broadcasted_iota(jnp.int32, (t, 2 * DIFF_DQK), 1)
    far = [bias_ref[mp, 2, 0:1, 0:1] for mp in range(2)]

    tasks = [(qi, kj, mp) for qi in range(n_tiles) for kj in range(qi + 1) for mp in range(2)]
    masked_q = {}
    state = {}
    done = {}

    def logits(qi, kj, mp):
        if qi not in masked_q:
            q = load_q(slice(qi * t, (qi + 1) * t))
            zero = jnp.zeros_like(q)
            masked_q[qi] = (jnp.where(lane < DIFF_DQK, q, zero), jnp.where(lane < DIFF_DQK, zero, q))
        s = _dot_nt(load_k(slice(kj * t, (kj + 1) * t)), masked_q[qi][mp])
        return s + bias_ref[mp, qi - kj] if qi - kj < 2 else s

    def accumulate(qi, kj, mp, s):
        shift = 0.0 if qi - kj < 2 else far[mp]
        m_blk = jnp.max(s, axis=0, keepdims=True) + shift
        prev = state.get((qi, mp))
        m_new = m_blk if prev is None else jnp.maximum(prev[0], m_blk)
        p = jnp.exp2((s - (m_new - shift)).astype(BF16))
        pv = _dot(v_t[:, kj * t:(kj + 1) * t], p)
        if prev is None:
            state[(qi, mp)] = (m_new, pv)
        else:
            state[(qi, mp)] = (m_new, jnp.exp2(prev[0] - m_new) * prev[1] + pv)
        if kj == qi:
            _, acc = state.pop((qi, mp))
            done.setdefault(qi, {})[mp] = acc[:DIFF_DV] / acc[DIFF_DV:DIFF_DV + 1]
            if len(done[qi]) == 2:
                o = done[qi][0] - lam * done[qi][1]
                o = o * lax.rsqrt(jnp.mean(o * o, axis=0, keepdims=True) + EPS)
                o_ref[qi * t:(qi + 1) * t, :] = (o.T * g_ref[...] * (1.0 - lam_init)).astype(BF16)
                del done[qi]

    pending = []
    for task in tasks:
        pending.append((task, logits(*task)))
        if len(pending) > ATT_LOOKAHEAD:
            ready, s = pending.pop(0)
            accumulate(*ready, s)
        yield
    for ready, s in pending:
        accumulate(*ready, s)
        yield


def _diff_kernel(q_ref, k_ref, v_ref, bias_ref, lam_ref, g_ref, o_ref, *, lam_init, n_tiles):
    for _ in _attention_steps(lambda sl: q_ref[sl, :], lambda sl: k_ref[sl, :], v_ref[...],
                              bias_ref, lam_ref, g_ref, o_ref, lam_init, n_tiles):
        pass


def _diff_call(dq, dk, dv, bias, diff_lambda, diff_norm, layer):
    bsz, seq, _ = dq.shape
    t = ATT_TILE
    lam_init = 0.8 - 0.6 * math.exp(-0.3 * layer)
    return pl.pallas_call(
        functools.partial(_diff_kernel, lam_init=lam_init, n_tiles=seq // t),
        grid=(bsz, DIFF_HEADS),
        in_specs=[
            pl.BlockSpec((None, seq, 2 * DIFF_DQK), lambda b, h: (b, 0, h)),
            pl.BlockSpec((None, seq, 2 * DIFF_DQK), lambda b, h: (b, 0, h)),
            pl.BlockSpec((None, seq, DIFF_DV), lambda b, h: (b, 0, h)),
            pl.BlockSpec((2, 3, t, t), lambda b, h: (h, 0, 0, 0)),
            pl.BlockSpec((None, 4, DIFF_DQK), lambda b, h: (layer, 0, 0)),
            pl.BlockSpec((None, 1, DIFF_DV), lambda b, h: (layer, 0, h)),
        ],
        out_specs=pl.BlockSpec((None, seq, DIFF_DV), lambda b, h: (b, 0, h)),
        out_shape=jax.ShapeDtypeStruct((bsz, seq, DIFF_V_W), BF16),
        compiler_params=_params("arbitrary", "arbitrary"),
        name="diff_attention",
    )(dq, dk, dv, bias, diff_lambda, diff_norm.reshape(DEPTH, 1, DIFF_V_W))


def _s5_prep_kernel(lre_ref, lim_ref, lstep_ref, bre_ref, bim_ref, cre_ref, cim_ref,
                    are_ref, aim_ref, bmat_ref, cmat_ref):
    lr = lre_ref[...]
    li = lim_ref[...]
    dt = jnp.exp(lstep_ref[...])
    mag = jnp.exp(lr * dt)
    a_re = mag * jnp.cos(li * dt)
    a_im = mag * jnp.sin(li * dt)
    are_ref[...] = jnp.broadcast_to(a_re, are_ref.shape)
    aim_ref[...] = jnp.broadcast_to(a_im, aim_ref.shape)
    nr, ni = a_re - 1.0, a_im
    den = lr * lr + li * li
    f_re = (nr * lr + ni * li) / den
    f_im = (ni * lr - nr * li) / den
    ns = S5_BLOCK_STATE
    for gb in range(S5_GROUP_BLOCKS):
        fr = f_re[:, gb * ns:(gb + 1) * ns]
        fi = f_im[:, gb * ns:(gb + 1) * ns]
        br = bre_ref[gb]
        bi = bim_ref[gb]
        bmat_ref[gb, :, :ns] = (fr * br - fi * bi).astype(BF16)
        bmat_ref[gb, :, ns:] = (fr * bi + fi * br).astype(BF16)
        cmat_ref[gb, :ns, :] = cre_ref[gb].astype(BF16)
        cmat_ref[gb, ns:, :] = (-cim_ref[gb]).astype(BF16)


def _s5_prep_call(lam_re, lam_im, log_step, b_re, b_im, c_re, c_im, bsz):
    d = DEPTH
    nb, ch, ns = S5_GROUP_BLOCKS, S5_BLOCK_CH, S5_BLOCK_STATE
    gpb = S5_GROUPS // nb
    eye = jnp.eye(gpb, dtype=F32)

    def in_blocks(b):
        t = b.reshape(d, nb, gpb, S5_STATE, S5_GROUP_CH).transpose(0, 1, 2, 4, 3)
        return (t[:, :, :, :, None, :] * eye[None, None, :, None, :, None]).reshape(d, nb, ch, ns)

    def out_blocks(c):
        t = c.reshape(d, nb, gpb, S5_GROUP_CH, S5_STATE).transpose(0, 1, 4, 2, 3)
        return (t[:, :, None, :, :, :] * eye[None, None, :, None, :, None]).reshape(d, nb, ns, ch)

    def row(x):
        return x.reshape(d, 1, S5_NSTATE)

    lstep = jnp.broadcast_to(log_step[:, :, None], (d, S5_GROUPS, S5_STATE))

    def full(shape):
        nd = len(shape)
        return pl.BlockSpec((None,) + shape, lambda l: (l,) + (0,) * nd)

    return pl.pallas_call(
        _s5_prep_kernel,
        grid=(d,),
        in_specs=[full((1, S5_NSTATE))] * 3 + [full((nb, ch, ns))] * 2 + [full((nb, ns, ch))] * 2,
        out_specs=[full((bsz, S5_NSTATE)), full((bsz, S5_NSTATE)),
                   full((nb, ch, 2 * ns)), full((nb, 2 * ns, ch))],
        out_shape=[jax.ShapeDtypeStruct((d, bsz, S5_NSTATE), F32), jax.ShapeDtypeStruct((d, bsz, S5_NSTATE), F32),
                   jax.ShapeDtypeStruct((d, nb, ch, 2 * ns), BF16), jax.ShapeDtypeStruct((d, nb, 2 * ns, ch), BF16)],
        compiler_params=_params("arbitrary"),
        name="s5_discretize",
    )(row(lam_re), row(lam_im), row(lstep), in_blocks(b_re), in_blocks(b_im), out_blocks(c_re), out_blocks(c_im))


def _s5_kernel(u_ref, are_ref, aim_ref, bmat_ref, cmat_ref, d_ref, wglu_ref, bglu_ref, o_ref,
               tb_ref, xr_ref, xi_ref, sr_ref, si_ref, *, bsz, n_steps):
    @pl.when(pl.program_id(0) == 0)
    def _():
        sr_ref[...] = jnp.zeros_like(sr_ref)
        si_ref[...] = jnp.zeros_like(si_ref)

    ns, ch, nb = S5_BLOCK_STATE, S5_BLOCK_CH, S5_GROUP_BLOCKS
    for b in range(bsz):
        ub = u_ref[b].astype(F32)
        for gb in range(nb):
            tb_ref[gb, pl.ds(b, n_steps, stride=bsz), :] = ub[:, gb * ch:(gb + 1) * ch]
    u = [tb_ref[gb] for gb in range(nb)]

    def drive(gb):
        return _dot(u[gb].astype(BF16), bmat_ref[gb])

    def scan(gb, bu):
        cols = slice(gb * ns, (gb + 1) * ns)
        ar, ai = are_ref[:, cols], aim_ref[:, cols]
        pr, pi = sr_ref[:, cols], si_ref[:, cols]
        for t in range(n_steps):
            rows = slice(t * bsz, (t + 1) * bsz)
            pr, pi = ar * pr - ai * pi + bu[rows, :ns], ar * pi + ai * pr + bu[rows, ns:]
            xr_ref[rows, cols] = pr
            xi_ref[rows, cols] = pi
        sr_ref[:, cols] = pr
        si_ref[:, cols] = pi

    def readout(gb):
        cols = slice(gb * ns, (gb + 1) * ns)
        return (_dot(xr_ref[:, cols].astype(BF16), cmat_ref[gb, :ns, :])
                + _dot(xi_ref[:, cols].astype(BF16), cmat_ref[gb, ns:, :]))

    bu, ys = {0: drive(0)}, []
    for gb in range(nb):
        if gb + 1 < nb:
            bu[gb + 1] = drive(gb + 1)
        scan(gb, bu.pop(gb))
        if gb >= 1:
            ys.append(readout(gb - 1))
    ys.append(readout(nb - 1))

    y = jax.nn.gelu(jnp.concatenate(ys, axis=-1) + d_ref[...] * jnp.concatenate(u, axis=-1))
    z = _dot(y.astype(BF16), wglu_ref[...]) + bglu_ref[...]
    out = z[:, :S5_WIDTH] * jax.nn.sigmoid(z[:, S5_WIDTH:])
    for gb in range(nb):
        tb_ref[gb] = out[:, gb * ch:(gb + 1) * ch]
    for b in range(bsz):
        for gb in range(nb):
            o_ref[b, :, gb * ch:(gb + 1) * ch] = tb_ref[gb, pl.ds(b, n_steps, stride=bsz), :].astype(BF16)


def _s5_call(su, a_re, a_im, bmat, cmat, d_skip, wglu, bglu, layer):
    bsz, seq, _ = su.shape
    tl = S5_TIME_TILE
    tr = tl * bsz
    nb, ch, ns = S5_GROUP_BLOCKS, S5_BLOCK_CH, S5_BLOCK_STATE

    def res(shape):
        nd = len(shape)
        return _resident((None,) + shape, lambda i: (layer,) + (0,) * nd)

    tok = pl.BlockSpec((bsz, tl, S5_WIDTH), lambda i: (0, i, 0))
    return pl.pallas_call(
        functools.partial(_s5_kernel, bsz=bsz, n_steps=tl),
        grid=(seq // tl,),
        in_specs=[tok,
                  res((bsz, S5_NSTATE)), res((bsz, S5_NSTATE)),
                  res((nb, ch, 2 * ns)), res((nb, 2 * ns, ch)),
                  res((1, S5_WIDTH)), res((S5_WIDTH, 2 * S5_WIDTH)), res((1, 2 * S5_WIDTH))],
        out_specs=tok,
        out_shape=jax.ShapeDtypeStruct((bsz, seq, S5_WIDTH), BF16),
        scratch_shapes=[pltpu.VMEM((nb, tr, ch), F32),
                        pltpu.VMEM((tr, S5_NSTATE), F32), pltpu.VMEM((tr, S5_NSTATE), F32),
                        pltpu.VMEM((bsz, S5_NSTATE), F32), pltpu.VMEM((bsz, S5_NSTATE), F32)],
        compiler_params=_params("arbitrary"),
        name="s5_mixer",
    )(su, a_re, a_im, bmat, cmat, d_skip, wglu, bglu)


def _merge_kernel(h_ref, ada_ref, npre_ref, npost_ref, yg_ref, yd_ref, ys_ref, wgate_ref, wbr_ref, wout_ref, o_ref):
    for r in range(h_ref.shape[0] // ROW_BLOCK):
        rows = slice(r * ROW_BLOCK, (r + 1) * ROW_BLOCK)
        h = h_ref[rows, :]
        u = _prenorm_mod(h, npre_ref[1:2, :], ada_ref[3:4, :], ada_ref[4:5, :]).astype(BF16)
        merged = None
        for i, y_ref in enumerate((yg_ref, yd_ref, ys_ref)):
            gate = jax.nn.sigmoid(_dot(u, wgate_ref[:, i * D_MODEL:(i + 1) * D_MODEL]))
            term = gate * _dot(y_ref[rows, :], wbr_ref[i])
            merged = term if merged is None else merged + term
        y = _dot(merged.astype(BF16), wout_ref[...])
        o_ref[rows, :] = h + ada_ref[5:6, :] * (_rms(y) * npost_ref[1:2, :])


def _merge_call(h, ada4, norm_pre, norm_post, y_gla, y_diff, y_s5, w_mix, wbr, wout, layer):
    bsz, seq, _ = h.shape
    tm = TOKEN_TILE

    def tok(w):
        return pl.BlockSpec((None, tm, w), lambda b, i: (b, i, 0))

    return pl.pallas_call(
        _merge_kernel,
        grid=(bsz, seq // tm),
        in_specs=[
            tok(D_MODEL),
            pl.BlockSpec((None, None, ADA_CHUNKS, D_MODEL), lambda b, i: (layer, b, 0, 0)),
            pl.BlockSpec((None, 3, D_MODEL), lambda b, i: (layer, 0, 0)),
            pl.BlockSpec((None, 3, D_MODEL), lambda b, i: (layer, 0, 0)),
            tok(GLA_V_W), tok(DIFF_V_W), tok(S5_WIDTH),
            _resident((None, D_MODEL, MIX_COLS["gate"][0]), lambda b, i: (layer, 0, MIX_COLS["gate"][1])),
            _resident((None, N_BRANCH, S5_WIDTH, D_MODEL), lambda b, i: (layer, 0, 0, 0)),
            _resident((None, D_MODEL, D_MODEL), lambda b, i: (layer, 0, 0)),
        ],
        out_specs=tok(D_MODEL),
        out_shape=jax.ShapeDtypeStruct(h.shape, F32),
        compiler_params=_params("arbitrary", "arbitrary"),
        name="mixer_merge",
    )(h, ada4, norm_pre, norm_post, y_gla, y_diff, y_s5, w_mix, wbr, wout)


def kernel(x, c, rel_bias, w_ada, b_ada, norm_pre, norm_post, ffn1_w_up, ffn1_w_down, ffn2_w_up, ffn2_w_down,
           w_in, gla_w_alpha, gla_b_alpha, gla_norm, diff_lambda, diff_norm, s5_lam_re, s5_lam_im, s5_log_step,
           s5_b_re, s5_b_im, s5_c_re, s5_c_im, s5_d, s5_w_glu, s5_b_glu, w_branch, w_out):
    bsz, seq, _ = x.shape
    assert bsz == SUBLANES, "the S5 scan keeps the batch on the sublane axis"

    f1 = (ffn1_w_up, ffn1_w_down.astype(BF16))
    f2 = (ffn2_w_up, ffn2_w_down.astype(BF16))
    o_alpha = 2 * GLA_QK_W + 2 * GLA_V_W
    o_diff = o_alpha + GLA_LOWRANK
    o_s5 = o_diff + 2 * DIFF_QK_W + DIFF_V_W
    o_gate = o_s5 + S5_WIDTH
    w_in_b = w_in.astype(BF16)
    w_mix = jnp.concatenate(
        [w_in_b[:, :, o_gate:], w_in_b[:, :, :o_alpha], w_in_b[:, :, o_diff:o_s5], w_in_b[:, :, o_s5:o_gate],
         w_in_b[:, :, o_alpha:o_diff], jnp.zeros((DEPTH, D_MODEL, LANES - GLA_LOWRANK), BF16)], axis=-1)
    walpha = jnp.pad(gla_w_alpha.astype(BF16), ((0, 0), (0, LANES - GLA_LOWRANK), (0, 0)))
    balpha = gla_b_alpha.reshape(DEPTH, 1, GLA_QK_W)
    wbr = w_branch.astype(BF16)
    wout = w_out.astype(BF16)
    wglu = s5_w_glu.astype(BF16)
    bglu = s5_b_glu.reshape(DEPTH, 1, 2 * S5_WIDTH)
    d_skip = s5_d.reshape(DEPTH, 1, S5_WIDTH)

    ada4 = _ada_call(c, w_ada, b_ada).reshape(DEPTH, bsz, ADA_CHUNKS, D_MODEL)
    bias = _bias_call(rel_bias)
    a_re, a_im, bmat, cmat = _s5_prep_call(s5_lam_re, s5_lam_im, s5_log_step, s5_b_re, s5_b_im,
                                           s5_c_re, s5_c_im, bsz)

    h = x
    for layer in range(DEPTH):
        h = _ffn_call(h, ada4, norm_pre, norm_post, *f1, layer, 0)
        gq, gk, gv, gr, la, dq, dk, dv, su = _inproj_call(h, ada4, norm_pre, w_mix, walpha, balpha, layer)
        y_gla = _gla_call(gq, gk, gv, gr, la, gla_norm, layer)
        y_diff = _diff_call(dq, dk, dv, bias, diff_lambda, diff_norm, layer)
        y_s5 = _s5_call(su, a_re, a_im, bmat, cmat, d_skip, wglu, bglu, layer)
        h = _merge_call(h, ada4, norm_pre, norm_post, y_gla, y_diff, y_s5, w_mix, wbr, wout, layer)
        h = _ffn_call(h, ada4, norm_pre, norm_post, *f2, layer, 2)
    return h
```

```python
import functools
import math

import numpy as np
import jax
import jax.numpy as jnp
from jax import lax
from jax.experimental import pallas as pl
from jax.experimental.pallas import tpu as pltpu

F32 = jnp.float32
BF16 = jnp.bfloat16

D_MODEL = 1024
DEPTH = 2
D_FF = 2816
ADA_CHUNKS = 9
EPS = 1e-6
LOG2E = math.log2(math.e)

GLA_HEADS = 4
GLA_DK = 64
GLA_DV = 128
GLA_LOWRANK = 16
GLA_TAU = 16.0
GLA_CHUNK = 64
GLA_QK_W = GLA_HEADS * GLA_DK
GLA_V_W = GLA_HEADS * GLA_DV

DIFF_HEADS = 4
DIFF_DQK = 64
DIFF_DV = 128
DIFF_QK_W = DIFF_HEADS * 2 * DIFF_DQK
DIFF_V_W = DIFF_HEADS * DIFF_DV
NUM_BUCKETS = 32
MAX_DISTANCE = 128

S5_GROUPS = 32
S5_GROUP_CH = 16
S5_STATE = 64
S5_WIDTH = S5_GROUPS * S5_GROUP_CH
S5_NSTATE = S5_GROUPS * S5_STATE
S5_GROUP_BLOCKS = 4
S5_BLOCK_CH = S5_WIDTH // S5_GROUP_BLOCKS
S5_BLOCK_STATE = S5_NSTATE // S5_GROUP_BLOCKS

N_BRANCH = 3

SUBLANES = 8
LANES = 128
VMEM_LIMIT_BYTES = 56 * 1024 * 1024

TOKEN_TILE = 1024
ROW_BLOCK = 512
FF_CHUNK = 256
FF_NCHUNK = D_FF // FF_CHUNK
FFN_LOOKAHEAD = 1
GLA_PAIR = 2 * GLA_CHUNK
GLA_TILE = 1024
ATT_TILE = 256
ATT_LOOKAHEAD = 4
S5_TIME_TILE = 64
ADA_COL_TILE = 1536
REGROUP_ROWS = 256
MIX_COLS = {"gate": (N_BRANCH * D_MODEL, 0), "gla": (2 * GLA_QK_W + 2 * GLA_V_W, 2),
            "diff": (2 * DIFF_QK_W + DIFF_V_W, 3), "s5": (S5_WIDTH, 12), "alpha": (LANES, 52)}


def _dot(a, b):
    return jnp.dot(a, b, preferred_element_type=F32)


def _dot_nt(a, b):
    return lax.dot_general(a, b, (((1,), (1,)), ((), ())), preferred_element_type=F32)


def _rms(x):
    return x * lax.rsqrt(jnp.mean(x * x, axis=-1, keepdims=True) + EPS)


def _prenorm_mod(h, g, shift, scale):
    return (_rms(h) * g) * (1.0 + scale) + shift


def _params(*sem):
    return pltpu.CompilerParams(dimension_semantics=sem, vmem_limit_bytes=VMEM_LIMIT_BYTES)


def _resident(shape, index_map):
    return pl.BlockSpec(shape, index_map, pipeline_mode=pl.Buffered(1))


def _ada_kernel(c_ref, w_ref, b_ref, o_ref):
    cond = jax.nn.silu(c_ref[...]).astype(BF16)
    o_ref[...] = _dot(cond, w_ref[...].astype(BF16)) + b_ref[...]


def _ada_call(c, w_ada, b_ada):
    bsz = c.shape[0]
    n = ADA_CHUNKS * D_MODEL
    return pl.pallas_call(
        _ada_kernel,
        grid=(DEPTH, n // ADA_COL_TILE),
        in_specs=[
            pl.BlockSpec((bsz, D_MODEL), lambda l, j: (0, 0)),
            pl.BlockSpec((None, D_MODEL, ADA_COL_TILE), lambda l, j: (l, 0, j)),
            pl.BlockSpec((None, 1, ADA_COL_TILE), lambda l, j: (l, 0, j)),
        ],
        out_specs=pl.BlockSpec((None, bsz, ADA_COL_TILE), lambda l, j: (l, 0, j)),
        out_shape=jax.ShapeDtypeStruct((DEPTH, bsz, n), F32),
        compiler_params=_params("arbitrary", "arbitrary"),
        name="ada_proj",
    )(c, w_ada, b_ada.reshape(DEPTH, 1, n))


def _ffn_kernel(h_ref, ada_ref, npre_ref, npost_ref, wup_ref, wdn_ref, o_ref, *, sub):
    a0 = 3 * sub

    def down(j, gate, up):
        mid = (jax.nn.silu(gate) * up).astype(BF16)
        return _dot(mid, wdn_ref[j * FF_CHUNK:(j + 1) * FF_CHUNK, :])

    for r in range(h_ref.shape[0] // ROW_BLOCK):
        rows = slice(r * ROW_BLOCK, (r + 1) * ROW_BLOCK)
        h = h_ref[rows, :]
        u = _prenorm_mod(h, npre_ref[sub:sub + 1, :], ada_ref[a0:a0 + 1, :], ada_ref[a0 + 1:a0 + 2, :])
        u = u.astype(BF16)

        def gate_up(j, u=u):
            lo = j * FF_CHUNK
            return _dot(u, wup_ref[:, lo:lo + FF_CHUNK]), _dot(u, wup_ref[:, D_FF + lo:D_FF + lo + FF_CHUNK])

        acc = None
        pending = []
        for j in range(FF_NCHUNK + FFN_LOOKAHEAD):
            if j < FF_NCHUNK:
                pending.append((j,) + gate_up(j))
            if j >= FFN_LOOKAHEAD:
                d = down(*pending.pop(0))
                acc = d if acc is None else acc + d
        post = _rms(acc) * npost_ref[sub:sub + 1, :]
        o_ref[rows, :] = h + (0.5 * ada_ref[a0 + 2:a0 + 3, :]) * post


def _ffn_call(h, ada4, norm_pre, norm_post, wup, wdn, layer, sub):
    bsz, seq, _ = h.shape
    tm = TOKEN_TILE
    tok = pl.BlockSpec((None, tm, D_MODEL), lambda b, i: (b, i, 0))
    return pl.pallas_call(
        functools.partial(_ffn_kernel, sub=sub),
        grid=(bsz, seq // tm),
        in_specs=[
            tok,
            pl.BlockSpec((None, None, ADA_CHUNKS, D_MODEL), lambda b, i: (layer, b, 0, 0)),
            pl.BlockSpec((None, 3, D_MODEL), lambda b, i: (layer, 0, 0)),
            pl.BlockSpec((None, 3, D_MODEL), lambda b, i: (layer, 0, 0)),
            _resident((None, D_MODEL, 2 * D_FF), lambda b, i: (layer, 0, 0)),
            _resident((None, D_FF, D_MODEL), lambda b, i: (layer, 0, 0)),
        ],
        out_specs=tok,
        out_shape=jax.ShapeDtypeStruct(h.shape, F32),
        compiler_params=_params("arbitrary", "arbitrary"),
        name=f"ffn{sub // 2 + 1}",
    )(h, ada4, norm_pre, norm_post, wup, wdn)


def _regroup_kernel(w_ref, o_ref):
    o_alpha = MIX_COLS["gla"][0]
    o_diff = o_alpha + GLA_LOWRANK
    o_s5 = o_diff + MIX_COLS["diff"][0]
    o_gate = o_s5 + MIX_COLS["s5"][0]
    col = 0
    for lo, hi in ((o_gate, w_ref.shape[1]), (0, o_alpha), (o_diff, o_s5), (o_s5, o_gate), (o_alpha, o_diff)):
        o_ref[:, col:col + hi - lo] = w_ref[:, lo:hi].astype(BF16)
        col += hi - lo
    o_ref[:, col:] = jnp.zeros((o_ref.shape[0], o_ref.shape[1] - col), BF16)


def _regroup_call(w_in):
    depth, rows, width = w_in.shape
    out_width = sum(w for w, _ in MIX_COLS.values())
    tr = REGROUP_ROWS
    return pl.pallas_call(
        _regroup_kernel,
        grid=(depth, rows // tr),
        in_specs=[pl.BlockSpec((None, tr, width), lambda l, i: (l, i, 0))],
        out_specs=pl.BlockSpec((None, tr, out_width), lambda l, i: (l, i, 0)),
        out_shape=jax.ShapeDtypeStruct((depth, rows, out_width), BF16),
        compiler_params=_params("arbitrary", "arbitrary"),
        name="w_in_regroup",
    )(w_in)


def _log_sigmoid(z):
    return jnp.minimum(z, 0.0) - jnp.log1p(jnp.exp(-jnp.abs(z)))


def _inproj_kernel(h_ref, ada_ref, npre_ref, wgla_ref, wa_ref, walpha_ref, balpha_ref, wdiff_ref, ws5_ref,
                   gq_ref, gk_ref, gv_ref, gr_ref, la_ref, dq_ref, dk_ref, dv_ref, su_ref):
    for r in range(h_ref.shape[0] // ROW_BLOCK):
        rows = slice(r * ROW_BLOCK, (r + 1) * ROW_BLOCK)
        u = _prenorm_mod(h_ref[rows, :], npre_ref[1:2, :], ada_ref[3:4, :], ada_ref[4:5, :]).astype(BF16)
        pg = _dot(u, wgla_ref[...])
        gq_ref[rows, :] = pg[:, :GLA_QK_W]
        gk_ref[rows, :] = pg[:, GLA_QK_W:2 * GLA_QK_W]
        gv_ref[rows, :] = pg[:, 2 * GLA_QK_W:2 * GLA_QK_W + GLA_V_W].astype(BF16)
        gr_ref[rows, :] = pg[:, 2 * GLA_QK_W + GLA_V_W:]
        a_low = _dot(u, wa_ref[...]).astype(BF16)
        z = _dot(a_low, walpha_ref[...]) + balpha_ref[...]
        la_ref[rows, :] = _log_sigmoid(z) / GLA_TAU
        pd = _dot(u, wdiff_ref[...])
        dq_ref[rows, :] = (pd[:, :DIFF_QK_W] * (DIFF_DQK ** -0.5 * LOG2E)).astype(BF16)
        dk_ref[rows, :] = pd[:, DIFF_QK_W:2 * DIFF_QK_W].astype(BF16)
        dv_ref[rows, :] = pd[:, 2 * DIFF_QK_W:].astype(BF16)
        su_ref[rows, :] = _dot(u, ws5_ref[...]).astype(BF16)


def _inproj_call(h, ada4, norm_pre, w_mix, walpha, balpha, layer):
    bsz, seq, _ = h.shape
    tm = TOKEN_TILE

    def tok(w):
        return pl.BlockSpec((None, tm, w), lambda b, i: (b, i, 0))

    def sds(w, dt):
        return jax.ShapeDtypeStruct((bsz, seq, w), dt)

    def res(arr):
        nd = arr.ndim - 1
        return _resident((None,) + arr.shape[1:], lambda b, i: (layer,) + (0,) * nd)

    def mix(group):
        width, block = MIX_COLS[group]
        return _resident((None, D_MODEL, width), lambda b, i: (layer, 0, block))

    return pl.pallas_call(
        _inproj_kernel,
        grid=(bsz, seq // tm),
        in_specs=[
            tok(D_MODEL),
            pl.BlockSpec((None, None, ADA_CHUNKS, D_MODEL), lambda b, i: (layer, b, 0, 0)),
            pl.BlockSpec((None, 3, D_MODEL), lambda b, i: (layer, 0, 0)),
            mix("gla"), mix("alpha"), res(walpha), res(balpha), mix("diff"), mix("s5"),
        ],
        out_specs=[tok(GLA_QK_W), tok(GLA_QK_W), tok(GLA_V_W), tok(GLA_V_W), tok(GLA_QK_W),
                   tok(DIFF_QK_W), tok(DIFF_QK_W), tok(DIFF_V_W), tok(S5_WIDTH)],
        out_shape=[sds(GLA_QK_W, F32), sds(GLA_QK_W, F32), sds(GLA_V_W, BF16), sds(GLA_V_W, F32),
                   sds(GLA_QK_W, F32), sds(DIFF_QK_W, BF16), sds(DIFF_QK_W, BF16), sds(DIFF_V_W, BF16),
                   sds(S5_WIDTH, BF16)],
        compiler_params=_params("arbitrary", "arbitrary"),
        name="mixer_in_proj",
    )(h, ada4, norm_pre, w_mix, w_mix, walpha, balpha, w_mix, w_mix)


def _cumsum_rows(tri, x):
    hi = x.astype(BF16)
    r1 = x - hi.astype(F32)
    mid = r1.astype(BF16)
    lo = (r1 - mid.astype(F32)).astype(BF16)
    return _dot(tri, hi) + _dot(tri, mid) + _dot(tri, lo)


def _gla_kernel(q_ref, k_ref, v_ref, r_ref, la_ref, g_ref, o_ref, s_ref, *, n_pairs):
    @pl.when(pl.program_id(1) == 0)
    def _():
        s_ref[...] = jnp.zeros_like(s_ref)

    row = lax.broadcasted_iota(jnp.int32, (GLA_PAIR, GLA_PAIR), 0)
    col = lax.broadcasted_iota(jnp.int32, (GLA_PAIR, GLA_PAIR), 1)
    causal = (col <= row) & ((row // GLA_CHUNK) == (col // GLA_CHUNK))
    tri = jnp.where(causal, 1.0, 0.0).astype(BF16)
    qlane = lax.broadcasted_iota(jnp.int32, (GLA_PAIR, GLA_QK_W), 1)
    tlane = lax.broadcasted_iota(jnp.int32, (GLA_QK_W, GLA_PAIR), 1)
    first = tlane < GLA_CHUNK

    def decays(p):
        rows = slice(p * GLA_PAIR, (p + 1) * GLA_PAIR)
        b = _cumsum_rows(tri, la_ref[rows, :])
        b_t = b.T
        qd = (q_ref[rows, :] * (GLA_DK ** -0.5)) * jnp.exp(b)
        k_t = k_ref[rows, :].T
        bl0 = b_t[:, GLA_CHUNK - 1:GLA_CHUNK]
        bl1 = b_t[:, GLA_PAIR - 1:GLA_PAIR]
        kdec_t = k_t * jnp.exp(jnp.where(first, bl0, bl1) - b_t)
        return dict(
            qh=[jnp.where(qlane // GLA_DK == h, qd, 0.0).astype(BF16) for h in range(GLA_HEADS)],
            kd_t=(k_t * jnp.exp(-b_t)).astype(BF16),
            kdec0=jnp.where(first, kdec_t, 0.0).astype(BF16),
            kdec1=jnp.where(first, 0.0, kdec_t).astype(BF16),
            dec0=jnp.exp(bl0), dec1=jnp.exp(bl1))

    def products(p, d, s0):
        v = v_ref[p * GLA_PAIR:(p + 1) * GLA_PAIR, :]
        ds0, ds1 = [], []
        for h in range(GLA_HEADS):
            hr = slice(h * GLA_DK, (h + 1) * GLA_DK)
            vh = v[:, h * GLA_DV:(h + 1) * GLA_DV]
            ds0.append(_dot(d["kdec0"][hr, :], vh))
            ds1.append(_dot(d["kdec1"][hr, :], vh))
        att = [jnp.where(causal, _dot(qh, d["kd_t"]), 0.0).astype(BF16) for qh in d["qh"]]
        s1 = d["dec0"] * s0 + jnp.concatenate(ds0, axis=0)
        s2 = d["dec1"] * s1 + jnp.concatenate(ds1, axis=0)
        s0b = s0.astype(BF16)
        s1b = s1.astype(BF16)
        inter = [jnp.concatenate([_dot(qh[:GLA_CHUNK], s0b), _dot(qh[GLA_CHUNK:], s1b)], axis=0)
                 for qh in d["qh"]]
        return s2, (v, att, inter)

    def outputs(p, v, att, inter):
        rows = slice(p * GLA_PAIR, (p + 1) * GLA_PAIR)
        r = r_ref[rows, :]
        for h in range(GLA_HEADS):
            vc = slice(h * GLA_DV, (h + 1) * GLA_DV)
            o = _rms(_dot(att[h], v[:, vc]) + inter[h]) * g_ref[:, vc]
            o_ref[rows, vc] = (o * jax.nn.silu(r[:, vc])).astype(BF16)

    state = s_ref[...]
    dec, prod = {}, {}
    for i in range(n_pairs + 2):
        if i < n_pairs:
            dec[i] = decays(i)
        if 0 <= i - 1 < n_pairs:
            state, prod[i - 1] = products(i - 1, dec.pop(i - 1), state)
        if 0 <= i - 2 < n_pairs:
            outputs(i - 2, *prod.pop(i - 2))
    s_ref[...] = state


def _gla_call(gq, gk, gv, gr, la, gla_norm, layer):
    bsz, seq, _ = gq.shape
    tl = GLA_TILE

    def tok(w):
        return pl.BlockSpec((None, tl, w), lambda b, i: (b, i, 0))

    return pl.pallas_call(
        functools.partial(_gla_kernel, n_pairs=tl // GLA_PAIR),
        grid=(bsz, seq // tl),
        in_specs=[tok(GLA_QK_W), tok(GLA_QK_W), tok(GLA_V_W), tok(GLA_V_W), tok(GLA_QK_W),
                  pl.BlockSpec((None, 1, GLA_V_W), lambda b, i: (layer, 0, 0))],
        out_specs=tok(GLA_V_W),
        out_shape=jax.ShapeDtypeStruct((bsz, seq, GLA_V_W), BF16),
        scratch_shapes=[pltpu.VMEM((GLA_QK_W, GLA_DV), F32)],
        compiler_params=_params("arbitrary", "arbitrary"),
        name="gla_mixer",
    )(gq, gk, gv, gr, la, gla_norm.reshape(DEPTH, 1, GLA_V_W))


def _t5_bucket_tiles():
    t = ATT_TILE
    r = np.arange(t)[:, None]
    c = np.arange(t)[None, :]
    max_exact = NUM_BUCKETS // 2
    tiles = []
    for off in range(3):
        dist = off * t + r - c
        d = np.maximum(dist, 1).astype(np.float32)
        large = max_exact + (np.log(d / np.float32(max_exact)) / np.float32(math.log(MAX_DISTANCE / max_exact))
                             * np.float32(NUM_BUCKETS - max_exact)).astype(np.int32)
        large = np.minimum(large, NUM_BUCKETS - 1)
        bucket = np.where(dist < max_exact, dist, large)
        tiles.append(np.where(dist >= 0, bucket, -1).T)
    far = tiles[2]
    assert (far == NUM_BUCKETS - 1).all()
    return np.stack(tiles).astype(np.int32)


def _bias_kernel(rel_ref, bucket_ref, o_ref):
    m = pl.program_id(0)
    bucket = bucket_ref[0:2]
    acc = jnp.full(bucket.shape, -jnp.inf, F32)
    for kk in range(NUM_BUCKETS):
        acc = jnp.where(bucket == kk, rel_ref[kk, m], acc)
    o_ref[0:2] = acc * LOG2E
    o_ref[2] = jnp.full(o_ref.shape[1:], rel_ref[NUM_BUCKETS - 1, m], F32) * LOG2E


def _bias_call(rel_bias):
    n_maps = 2 * DIFF_HEADS
    t = ATT_TILE
    return pl.pallas_call(
        _bias_kernel,
        grid=(n_maps,),
        in_specs=[pl.BlockSpec(memory_space=pltpu.SMEM),
                  pl.BlockSpec((3, t, t), lambda m: (0, 0, 0))],
        out_specs=pl.BlockSpec((None, 3, t, t), lambda m: (m, 0, 0, 0)),
        out_shape=jax.ShapeDtypeStruct((n_maps, 3, t, t), F32),
        compiler_params=_params("arbitrary"),
        name="t5_bias_tiles",
    )(rel_bias, jnp.asarray(_t5_bucket_tiles()))


def _attention_steps(load_q, load_k, v_all, bias_ref, lam_ref, g_ref, o_ref, lam_init, n_tiles):
    t = ATT_TILE
    lv = lam_ref[...]
    lam = (jnp.exp(jnp.sum(lv[0:1] * lv[1:2], axis=-1, keepdims=True))
           - jnp.exp(jnp.sum(lv[2:3] * lv[3:4], axis=-1, keepdims=True)) + lam_init)
    v_t = jnp.concatenate([v_all.astype(F32).T, jnp.ones((16, v_all.shape[0]), F32)], axis=0).astype(BF16)
    lane = lax.broadcasted_iota(jnp.int32, (t, 2 * DIFF_DQK), 1)
    far = [bias_ref[mp, 2, 0:1, 0:1] for mp in range(2)]

    tasks = [(qi, kj, mp) for qi in range(n_tiles) for kj in range(qi + 1) for mp in range(2)]
    masked_q = {}
    state = {}
    done = {}

    def logits(qi, kj, mp):
        if qi not in masked_q:
            q = load_q(slice(qi * t, (qi + 1) * t))
            zero = jnp.zeros_like(q)
            masked_q[qi] = (jnp.where(lane < DIFF_DQK, q, zero), jnp.where(lane < DIFF_DQK, zero, q))
        s = _dot_nt(load_k(slice(kj * t, (kj + 1) * t)), masked_q[qi][mp])
        return s + bias_ref[mp, qi - kj] if qi - kj < 2 else s

    def accumulate(qi, kj, mp, s):
        shift = 0.0 if qi - kj < 2 else far[mp]
        m_blk = jnp.max(s, axis=0, keepdims=True) + shift
        prev = state.get((qi, mp))
        m_new = m_blk if prev is None else jnp.maximum(prev[0], m_blk)
        p = jnp.exp2((s - (m_new - shift)).astype(BF16))
        pv = _dot(v_t[:, kj * t:(kj + 1) * t], p)
        if prev is None:
            state[(qi, mp)] = (m_new, pv)
        else:
            state[(qi, mp)] = (m_new, jnp.exp2(prev[0] - m_new) * prev[1] + pv)
        if kj == qi:
            _, acc = state.pop((qi, mp))
            done.setdefault(qi, {})[mp] = acc[:DIFF_DV] / acc[DIFF_DV:DIFF_DV + 1]
            if len(done[qi]) == 2:
                o = done[qi][0] - lam * done[qi][1]
                o = o * lax.rsqrt(jnp.mean(o * o, axis=0, keepdims=True) + EPS)
                o_ref[qi * t:(qi + 1) * t, :] = (o.T * g_ref[...] * (1.0 - lam_init)).astype(BF16)
                del done[qi]

    pending = []
    for task in tasks:
        pending.append((task, logits(*task)))
        if len(pending) > ATT_LOOKAHEAD:
            ready, s = pending.pop(0)
            accumulate(*ready, s)
        yield
    for ready, s in pending:
        accumulate(*ready, s)
        yield


def _diff_kernel(q_ref, k_ref, v_ref, bias_ref, lam_ref, g_ref, o_ref, *, lam_init, n_tiles):
    for _ in _attention_steps(lambda sl: q_ref[sl, :], lambda sl: k_ref[sl, :], v_ref[...],
                              bias_ref, lam_ref, g_ref, o_ref, lam_init, n_tiles):
        pass


def _diff_call(dq, dk, dv, bias, diff_lambda, diff_norm, layer):
    bsz, seq, _ = dq.shape
    t = ATT_TILE
    lam_init = 0.8 - 0.6 * math.exp(-0.3 * layer)
    return pl.pallas_call(
        functools.partial(_diff_kernel, lam_init=lam_init, n_tiles=seq // t),
        grid=(bsz, DIFF_HEADS),
        in_specs=[
            pl.BlockSpec((None, seq, 2 * DIFF_DQK), lambda b, h: (b, 0, h)),
            pl.BlockSpec((None, seq, 2 * DIFF_DQK), lambda b, h: (b, 0, h)),
            pl.BlockSpec((None, seq, DIFF_DV), lambda b, h: (b, 0, h)),
            pl.BlockSpec((2, 3, t, t), lambda b, h: (h, 0, 0, 0)),
            pl.BlockSpec((None, 4, DIFF_DQK), lambda b, h: (layer, 0, 0)),
            pl.BlockSpec((None, 1, DIFF_DV), lambda b, h: (layer, 0, h)),
        ],
        out_specs=pl.BlockSpec((None, seq, DIFF_DV), lambda b, h: (b, 0, h)),
        out_shape=jax.ShapeDtypeStruct((bsz, seq, DIFF_V_W), BF16),
        compiler_params=_params("arbitrary", "arbitrary"),
        name="diff_attention",
    )(dq, dk, dv, bias, diff_lambda, diff_norm.reshape(DEPTH, 1, DIFF_V_W))


def _s5_prep_kernel(lre_ref, lim_ref, lstep_ref, bre_ref, bim_ref, cre_ref, cim_ref,
                    are_ref, aim_ref, bmat_ref, cmat_ref):
    lr = lre_ref[...]
    li = lim_ref[...]
    dt = jnp.exp(lstep_ref[...])
    mag = jnp.exp(lr * dt)
    a_re = mag * jnp.cos(li * dt)
    a_im = mag * jnp.sin(li * dt)
    are_ref[...] = jnp.broadcast_to(a_re, are_ref.shape)
    aim_ref[...] = jnp.broadcast_to(a_im, aim_ref.shape)
    nr, ni = a_re - 1.0, a_im
    den = lr * lr + li * li
    f_re = (nr * lr + ni * li) / den
    f_im = (ni * lr - nr * li) / den
    ns = S5_BLOCK_STATE
    for gb in range(S5_GROUP_BLOCKS):
        fr = f_re[:, gb * ns:(gb + 1) * ns]
        fi = f_im[:, gb * ns:(gb + 1) * ns]
        br = bre_ref[gb]
        bi = bim_ref[gb]
        bmat_ref[gb, :, :ns] = (fr * br - fi * bi).astype(BF16)
        bmat_ref[gb, :, ns:] = (fr * bi + fi * br).astype(BF16)
        cmat_ref[gb, :ns, :] = cre_ref[gb].astype(BF16)
        cmat_ref[gb, ns:, :] = (-cim_ref[gb]).astype(BF16)


def _s5_prep_call(lam_re, lam_im, log_step, b_re, b_im, c_re, c_im, bsz):
    d = DEPTH
    nb, ch, ns = S5_GROUP_BLOCKS, S5_BLOCK_CH, S5_BLOCK_STATE
    gpb = S5_GROUPS // nb
    eye = jnp.eye(gpb, dtype=F32)

    def in_blocks(b):
        t = b.reshape(d, nb, gpb, S5_STATE, S5_GROUP_CH).transpose(0, 1, 2, 4, 3)
        return (t[:, :, :, :, None, :] * eye[None, None, :, None, :, None]).reshape(d, nb, ch, ns)

    def out_blocks(c):
        t = c.reshape(d, nb, gpb, S5_GROUP_CH, S5_STATE).transpose(0, 1, 4, 2, 3)
        return (t[:, :, None, :, :, :] * eye[None, None, :, None, :, None]).reshape(d, nb, ns, ch)

    def row(x):
        return x.reshape(d, 1, S5_NSTATE)

    lstep = jnp.broadcast_to(log_step[:, :, None], (d, S5_GROUPS, S5_STATE))

    def full(shape):
        nd = len(shape)
        return pl.BlockSpec((None,) + shape, lambda l: (l,) + (0,) * nd)

    return pl.pallas_call(
        _s5_prep_kernel,
        grid=(d,),
        in_specs=[full((1, S5_NSTATE))] * 3 + [full((nb, ch, ns))] * 2 + [full((nb, ns, ch))] * 2,
        out_specs=[full((bsz, S5_NSTATE)), full((bsz, S5_NSTATE)),
                   full((nb, ch, 2 * ns)), full((nb, 2 * ns, ch))],
        out_shape=[jax.ShapeDtypeStruct((d, bsz, S5_NSTATE), F32), jax.ShapeDtypeStruct((d, bsz, S5_NSTATE), F32),
                   jax.ShapeDtypeStruct((d, nb, ch, 2 * ns), BF16), jax.ShapeDtypeStruct((d, nb, 2 * ns, ch), BF16)],
        compiler_params=_params("arbitrary"),
        name="s5_discretize",
    )(row(lam_re), row(lam_im), row(lstep), in_blocks(b_re), in_blocks(b_im), out_blocks(c_re), out_blocks(c_im))


def _s5_kernel(u_ref, are_ref, aim_ref, bmat_ref, cmat_ref, d_ref, wglu_ref, bglu_ref, o_ref,
               tb_ref, xr_ref, xi_ref, sr_ref, si_ref, *, bsz, n_steps):
    @pl.when(pl.program_id(0) == 0)
    def _():
        sr_ref[...] = jnp.zeros_like(sr_ref)
        si_ref[...] = jnp.zeros_like(si_ref)

    ns, ch, nb = S5_BLOCK_STATE, S5_BLOCK_CH, S5_GROUP_BLOCKS
    for b in range(bsz):
        ub = u_ref[b].astype(F32)
        for gb in range(nb):
            tb_ref[gb, pl.ds(b, n_steps, stride=bsz), :] = ub[:, gb * ch:(gb + 1) * ch]
    u = [tb_ref[gb] for gb in range(nb)]

    def drive(gb):
        return _dot(u[gb].astype(BF16), bmat_ref[gb])

    def scan(gb, bu):
        cols = slice(gb * ns, (gb + 1) * ns)
        ar, ai = are_ref[:, cols], aim_ref[:, cols]
        pr, pi = sr_ref[:, cols], si_ref[:, cols]
        for t in range(n_steps):
            rows = slice(t * bsz, (t + 1) * bsz)
            pr, pi = ar * pr - ai * pi + bu[rows, :ns], ar * pi + ai * pr + bu[rows, ns:]
            xr_ref[rows, cols] = pr
            xi_ref[rows, cols] = pi
        sr_ref[:, cols] = pr
        si_ref[:, cols] = pi

    def readout(gb):
        cols = slice(gb * ns, (gb + 1) * ns)
        return (_dot(xr_ref[:, cols].astype(BF16), cmat_ref[gb, :ns, :])
                + _dot(xi_ref[:, cols].astype(BF16), cmat_ref[gb, ns:, :]))

    bu, ys = {0: drive(0)}, []
    for gb in range(nb):
        if gb + 1 < nb:
            bu[gb + 1] = drive(gb + 1)
        scan(gb, bu.pop(gb))
        if gb >= 1:
            ys.append(readout(gb - 1))
    ys.append(readout(nb - 1))

    y = jax.nn.gelu(jnp.concatenate(ys, axis=-1) + d_ref[...] * jnp.concatenate(u, axis=-1))
    z = _dot(y.astype(BF16), wglu_ref[...]) + bglu_ref[...]
    out = z[:, :S5_WIDTH] * jax.nn.sigmoid(z[:, S5_WIDTH:])
    for gb in range(nb):
        tb_ref[gb] = out[:, gb * ch:(gb + 1) * ch]
    for b in range(bsz):
        for gb in range(nb):
            o_ref[b, :, gb * ch:(gb + 1) * ch] = tb_ref[gb, pl.ds(b, n_steps, stride=bsz), :].astype(BF16)


def _s5_call(su, a_re, a_im, bmat, cmat, d_skip, wglu, bglu, layer):
    bsz, seq, _ = su.shape
    tl = S5_TIME_TILE
    tr = tl * bsz
    nb, ch, ns = S5_GROUP_BLOCKS, S5_BLOCK_CH, S5_BLOCK_STATE

    def res(shape):
        nd = len(shape)
        return _resident((None,) + shape, lambda i: (layer,) + (0,) * nd)

    tok = pl.BlockSpec((bsz, tl, S5_WIDTH), lambda i: (0, i, 0))
    return pl.pallas_call(
        functools.partial(_s5_kernel, bsz=bsz, n_steps=tl),
        grid=(seq // tl,),
        in_specs=[tok,
                  res((bsz, S5_NSTATE)), res((bsz, S5_NSTATE)),
                  res((nb, ch, 2 * ns)), res((nb, 2 * ns, ch)),
                  res((1, S5_WIDTH)), res((S5_WIDTH, 2 * S5_WIDTH)), res((1, 2 * S5_WIDTH))],
        out_specs=tok,
        out_shape=jax.ShapeDtypeStruct((bsz, seq, S5_WIDTH), BF16),
        scratch_shapes=[pltpu.VMEM((nb, tr, ch), F32),
                        pltpu.VMEM((tr, S5_NSTATE), F32), pltpu.VMEM((tr, S5_NSTATE), F32),
                        pltpu.VMEM((bsz, S5_NSTATE), F32), pltpu.VMEM((bsz, S5_NSTATE), F32)],
        compiler_params=_params("arbitrary"),
        name="s5_mixer",
    )(su, a_re, a_im, bmat, cmat, d_skip, wglu, bglu)


def _merge_kernel(h_ref, ada_ref, npre_ref, npost_ref, yg_ref, yd_ref, ys_ref, wgate_ref, wbr_ref, wout_ref, o_ref):
    for r in range(h_ref.shape[0] // ROW_BLOCK):
        rows = slice(r * ROW_BLOCK, (r + 1) * ROW_BLOCK)
        h = h_ref[rows, :]
        u = _prenorm_mod(h, npre_ref[1:2, :], ada_ref[3:4, :], ada_ref[4:5, :]).astype(BF16)
        merged = None
        for i, y_ref in enumerate((yg_ref, yd_ref, ys_ref)):
            gate = jax.nn.sigmoid(_dot(u, wgate_ref[:, i * D_MODEL:(i + 1) * D_MODEL]))
            term = gate * _dot(y_ref[rows, :], wbr_ref[i])
            merged = term if merged is None else merged + term
        y = _dot(merged.astype(BF16), wout_ref[...])
        o_ref[rows, :] = h + ada_ref[5:6, :] * (_rms(y) * npost_ref[1:2, :])


def _merge_call(h, ada4, norm_pre, norm_post, y_gla, y_diff, y_s5, w_mix, wbr, wout, layer):
    bsz, seq, _ = h.shape
    tm = TOKEN_TILE

    def tok(w):
        return pl.BlockSpec((None, tm, w), lambda b, i: (b, i, 0))

    return pl.pallas_call(
        _merge_kernel,
        grid=(bsz, seq // tm),
        in_specs=[
            tok(D_MODEL),
            pl.BlockSpec((None, None, ADA_CHUNKS, D_MODEL), lambda b, i: (layer, b, 0, 0)),
            pl.BlockSpec((None, 3, D_MODEL), lambda b, i: (layer, 0, 0)),
            pl.BlockSpec((None, 3, D_MODEL), lambda b, i: (layer, 0, 0)),
            tok(GLA_V_W), tok(DIFF_V_W), tok(S5_WIDTH),
            _resident((None, D_MODEL, MIX_COLS["gate"][0]), lambda b, i: (layer, 0, MIX_COLS["gate"][1])),
            _resident((None, N_BRANCH, S5_WIDTH, D_MODEL), lambda b, i: (layer, 0, 0, 0)),
            _resident((None, D_MODEL, D_MODEL), lambda b, i: (layer, 0, 0)),
        ],
        out_specs=tok(D_MODEL),
        out_shape=jax.ShapeDtypeStruct(h.shape, F32),
        compiler_params=_params("arbitrary", "arbitrary"),
        name="mixer_merge",
    )(h, ada4, norm_pre, norm_post, y_gla, y_diff, y_s5, w_mix, wbr, wout)


def kernel(x, c, rel_bias, w_ada, b_ada, norm_pre, norm_post, ffn1_w_up, ffn1_w_down, ffn2_w_up, ffn2_w_down,
           w_in, gla_w_alpha, gla_b_alpha, gla_norm, diff_lambda, diff_norm, s5_lam_re, s5_lam_im, s5_log_step,
           s5_b_re, s5_b_im, s5_c_re, s5_c_im, s5_d, s5_w_glu, s5_b_glu, w_branch, w_out):
    bsz, seq, _ = x.shape
    assert bsz == SUBLANES, "the S5 scan keeps the batch on the sublane axis"

    f1 = (ffn1_w_up.astype(BF16), ffn1_w_down.astype(BF16))
    f2 = (ffn2_w_up.astype(BF16), ffn2_w_down.astype(BF16))
    w_mix = _regroup_call(w_in)
    walpha = jnp.pad(gla_w_alpha.astype(BF16), ((0, 0), (0, LANES - GLA_LOWRANK), (0, 0)))
    balpha = gla_b_alpha.reshape(DEPTH, 1, GLA_QK_W)
    wbr = w_branch.astype(BF16)
    wout = w_out.astype(BF16)
    wglu = s5_w_glu.astype(BF16)
    bglu = s5_b_glu.reshape(DEPTH, 1, 2 * S5_WIDTH)
    d_skip = s5_d.reshape(DEPTH, 1, S5_WIDTH)

    ada4 = _ada_call(c, w_ada, b_ada).reshape(DEPTH, bsz, ADA_CHUNKS, D_MODEL)
    bias = _bias_call(rel_bias)
    a_re, a_im, bmat, cmat = _s5_prep_call(s5_lam_re, s5_lam_im, s5_log_step, s5_b_re, s5_b_im,
                                           s5_c_re, s5_c_im, bsz)

    h = x
    for layer in range(DEPTH):
        h = _ffn_call(h, ada4, norm_pre, norm_post, *f1, layer, 0)
        gq, gk, gv, gr, la, dq, dk, dv, su = _inproj_call(h, ada4, norm_pre, w_mix, walpha, balpha, layer)
        y_gla = _gla_call(gq, gk, gv, gr, la, gla_norm, layer)
        y_diff = _diff_call(dq, dk, dv, bias, diff_lambda, diff_norm, layer)
        y_s5 = _s5_call(su, a_re, a_im, bmat, cmat, d_skip, wglu, bglu, layer)
        h = _merge_call(h, ada4, norm_pre, norm_post, y_gla, y_diff, y_s5, w_mix, wbr, wout, layer)
        h = _ffn_call(h, ada4, norm_pre, norm_post, *f2, layer, 2)
    return h
```

```python
import functools
import math

import numpy as np
import jax
import jax.numpy as jnp
from jax import lax
from jax.experimental import pallas as pl
from jax.experimental.pallas import tpu as pltpu

F32 = jnp.float32
BF16 = jnp.bfloat16

D_MODEL = 1024
DEPTH = 2
D_FF = 2816
ADA_CHUNKS = 9
EPS = 1e-6
LOG2E = math.log2(math.e)

GLA_HEADS = 4
GLA_DK = 64
GLA_DV = 128
GLA_LOWRANK = 16
GLA_TAU = 16.0
GLA_CHUNK = 64
GLA_QK_W = GLA_HEADS * GLA_DK
GLA_V_W = GLA_HEADS * GLA_DV

DIFF_HEADS = 4
DIFF_DQK = 64
DIFF_DV = 128
DIFF_QK_W = DIFF_HEADS * 2 * DIFF_DQK
DIFF_V_W = DIFF_HEADS * DIFF_DV
NUM_BUCKETS = 32
MAX_DISTANCE = 128

S5_GROUPS = 32
S5_GROUP_CH = 16
S5_STATE = 64
S5_WIDTH = S5_GROUPS * S5_GROUP_CH
S5_NSTATE = S5_GROUPS * S5_STATE
S5_GROUP_BLOCKS = 4
S5_BLOCK_CH = S5_WIDTH // S5_GROUP_BLOCKS
S5_BLOCK_STATE = S5_NSTATE // S5_GROUP_BLOCKS

N_BRANCH = 3

SUBLANES = 8
LANES = 128
ONES_ROWS = 2 * SUBLANES
VMEM_LIMIT_BYTES = 56 * 1024 * 1024

TOKEN_TILE = 1024
ROW_BLOCK = 512
FF_CHUNK = 256
FF_NCHUNK = D_FF // FF_CHUNK
FFN_LOOKAHEAD = 1
GLA_PAIR = 2 * GLA_CHUNK
GLA_TILE = 1024
ATT_TILE = 256
ATT_LOOKAHEAD = 4
S5_TIME_TILE = 64
ADA_COL_TILE = 1536
REGROUP_ROWS = 256
MIX_COLS = {"gate": (N_BRANCH * D_MODEL, 0), "gla": (2 * GLA_QK_W + 2 * GLA_V_W, 2),
            "diff": (2 * DIFF_QK_W + DIFF_V_W, 3), "s5": (S5_WIDTH, 12), "alpha": (LANES, 52)}


def _dot(a, b):
    return jnp.dot(a, b, preferred_element_type=F32)


def _dot_nt(a, b):
    return lax.dot_general(a, b, (((1,), (1,)), ((), ())), preferred_element_type=F32)


def _rms(x):
    return x * lax.rsqrt(jnp.mean(x * x, axis=-1, keepdims=True) + EPS)


def _prenorm_mod(h, g, shift, scale):
    return (_rms(h) * g) * (1.0 + scale) + shift


def _params(*sem):
    return pltpu.CompilerParams(dimension_semantics=sem, vmem_limit_bytes=VMEM_LIMIT_BYTES)


def _resident(shape, index_map):
    return pl.BlockSpec(shape, index_map, pipeline_mode=pl.Buffered(1))


def _ada_kernel(c_ref, w_ref, b_ref, o_ref):
    cond = jax.nn.silu(c_ref[...]).astype(BF16)
    o_ref[...] = _dot(cond, w_ref[...].astype(BF16)) + b_ref[...]


def _ada_call(c, w_ada, b_ada):
    bsz = c.shape[0]
    n = ADA_CHUNKS * D_MODEL
    return pl.pallas_call(
        _ada_kernel,
        grid=(DEPTH, n // ADA_COL_TILE),
        in_specs=[
            pl.BlockSpec((bsz, D_MODEL), lambda l, j: (0, 0)),
            pl.BlockSpec((None, D_MODEL, ADA_COL_TILE), lambda l, j: (l, 0, j)),
            pl.BlockSpec((None, 1, ADA_COL_TILE), lambda l, j: (l, 0, j)),
        ],
        out_specs=pl.BlockSpec((None, bsz, ADA_COL_TILE), lambda l, j: (l, 0, j)),
        out_shape=jax.ShapeDtypeStruct((DEPTH, bsz, n), F32),
        compiler_params=_params("arbitrary", "arbitrary"),
        name="ada_proj",
    )(c, w_ada, b_ada.reshape(DEPTH, 1, n))


def _ffn_kernel(h_ref, ada_ref, npre_ref, npost_ref, wup_ref, wdn_ref, o_ref, *, sub):
    a0 = 3 * sub

    def down(j, gate, up):
        mid = (jax.nn.silu(gate) * up).astype(BF16)
        return _dot(mid, wdn_ref[j * FF_CHUNK:(j + 1) * FF_CHUNK, :])

    for r in range(h_ref.shape[0] // ROW_BLOCK):
        rows = slice(r * ROW_BLOCK, (r + 1) * ROW_BLOCK)
        h = h_ref[rows, :]
        u = _prenorm_mod(h, npre_ref[sub:sub + 1, :], ada_ref[a0:a0 + 1, :], ada_ref[a0 + 1:a0 + 2, :])
        u = u.astype(BF16)

        def gate_up(j, u=u):
            lo = j * FF_CHUNK
            return _dot(u, wup_ref[:, lo:lo + FF_CHUNK]), _dot(u, wup_ref[:, D_FF + lo:D_FF + lo + FF_CHUNK])

        acc = None
        pending = []
        for j in range(FF_NCHUNK + FFN_LOOKAHEAD):
            if j < FF_NCHUNK:
                pending.append((j,) + gate_up(j))
            if j >= FFN_LOOKAHEAD:
                d = down(*pending.pop(0))
                acc = d if acc is None else acc + d
        post = _rms(acc) * npost_ref[sub:sub + 1, :]
        o_ref[rows, :] = h + (0.5 * ada_ref[a0 + 2:a0 + 3, :]) * post


def _ffn_call(h, ada4, norm_pre, norm_post, wup, wdn, layer, sub):
    bsz, seq, _ = h.shape
    tm = TOKEN_TILE
    tok = pl.BlockSpec((None, tm, D_MODEL), lambda b, i: (b, i, 0))
    return pl.pallas_call(
        functools.partial(_ffn_kernel, sub=sub),
        grid=(bsz, seq // tm),
        in_specs=[
            tok,
            pl.BlockSpec((None, None, ADA_CHUNKS, D_MODEL), lambda b, i: (layer, b, 0, 0)),
            pl.BlockSpec((None, 3, D_MODEL), lambda b, i: (layer, 0, 0)),
            pl.BlockSpec((None, 3, D_MODEL), lambda b, i: (layer, 0, 0)),
            _resident((None, D_MODEL, 2 * D_FF), lambda b, i: (layer, 0, 0)),
            _resident((None, D_FF, D_MODEL), lambda b, i: (layer, 0, 0)),
        ],
        out_specs=tok,
        out_shape=jax.ShapeDtypeStruct(h.shape, F32),
        compiler_params=_params("arbitrary", "arbitrary"),
        name=f"ffn{sub // 2 + 1}",
    )(h, ada4, norm_pre, norm_post, wup, wdn)


def _regroup_kernel(w_ref, o_ref):
    o_alpha = MIX_COLS["gla"][0]
    o_diff = o_alpha + GLA_LOWRANK
    o_s5 = o_diff + MIX_COLS["diff"][0]
    o_gate = o_s5 + MIX_COLS["s5"][0]
    col = 0
    for lo, hi in ((o_gate, w_ref.shape[1]), (0, o_alpha), (o_diff, o_s5), (o_s5, o_gate), (o_alpha, o_diff)):
        o_ref[:, col:col + hi - lo] = w_ref[:, lo:hi]
        col += hi - lo
    o_ref[:, col:] = jnp.zeros((o_ref.shape[0], o_ref.shape[1] - col), BF16)


def _regroup_call(w_in):
    depth, rows, width = w_in.shape
    out_width = sum(w for w, _ in MIX_COLS.values())
    tr = REGROUP_ROWS
    return pl.pallas_call(
        _regroup_kernel,
        grid=(depth, rows // tr),
        in_specs=[pl.BlockSpec((None, tr, width), lambda l, i: (l, i, 0))],
        out_specs=pl.BlockSpec((None, tr, out_width), lambda l, i: (l, i, 0)),
        out_shape=jax.ShapeDtypeStruct((depth, rows, out_width), BF16),
        compiler_params=_params("arbitrary", "arbitrary"),
        name="w_in_regroup",
    )(w_in)


def _log_sigmoid(z):
    return jnp.minimum(z, 0.0) - jnp.log1p(jnp.exp(-jnp.abs(z)))


def _inproj_kernel(h_ref, ada_ref, npre_ref, wgla_ref, wa_ref, walpha_ref, balpha_ref, wdiff_ref, ws5_ref,
                   gq_ref, gk_ref, gv_ref, gr_ref, la_ref, dq_ref, dk_ref, dv_ref, su_ref):
    for r in range(h_ref.shape[0] // ROW_BLOCK):
        rows = slice(r * ROW_BLOCK, (r + 1) * ROW_BLOCK)
        u = _prenorm_mod(h_ref[rows, :], npre_ref[1:2, :], ada_ref[3:4, :], ada_ref[4:5, :]).astype(BF16)
        a_low = _dot(u, wa_ref[...]).astype(BF16)
        pg = _dot(u, wgla_ref[...])
        z = _dot(a_low, walpha_ref[...]) + balpha_ref[...]
        gq_ref[rows, :] = pg[:, :GLA_QK_W]
        gk_ref[rows, :] = pg[:, GLA_QK_W:2 * GLA_QK_W]
        gv_ref[rows, :] = pg[:, 2 * GLA_QK_W:2 * GLA_QK_W + GLA_V_W].astype(BF16)
        gr_ref[rows, :] = pg[:, 2 * GLA_QK_W + GLA_V_W:]
        la_ref[rows, :] = _log_sigmoid(z) / GLA_TAU
        pd = _dot(u, wdiff_ref[...])
        dq_ref[rows, :] = (pd[:, :DIFF_QK_W] * (DIFF_DQK ** -0.5 * LOG2E)).astype(BF16)
        dk_ref[rows, :] = pd[:, DIFF_QK_W:2 * DIFF_QK_W].astype(BF16)
        dv_ref[rows, :] = pd[:, 2 * DIFF_QK_W:].astype(BF16)
        su_ref[rows, :] = _dot(u, ws5_ref[...]).astype(BF16)


def _inproj_call(h, ada4, norm_pre, w_mix, walpha, balpha, layer):
    bsz, seq, _ = h.shape
    tm = TOKEN_TILE

    def tok(w):
        return pl.BlockSpec((None, tm, w), lambda b, i: (b, i, 0))

    def sds(w, dt):
        return jax.ShapeDtypeStruct((bsz, seq, w), dt)

    def res(arr):
        nd = arr.ndim - 1
        return _resident((None,) + arr.shape[1:], lambda b, i: (layer,) + (0,) * nd)

    def mix(group):
        width, block = MIX_COLS[group]
        return _resident((None, D_MODEL, width), lambda b, i: (layer, 0, block))

    return pl.pallas_call(
        _inproj_kernel,
        grid=(bsz, seq // tm),
        in_specs=[
            tok(D_MODEL),
            pl.BlockSpec((None, None, ADA_CHUNKS, D_MODEL), lambda b, i: (layer, b, 0, 0)),
            pl.BlockSpec((None, 3, D_MODEL), lambda b, i: (layer, 0, 0)),
            mix("gla"), mix("alpha"), res(walpha), res(balpha), mix("diff"), mix("s5"),
        ],
        out_specs=[tok(GLA_QK_W), tok(GLA_QK_W), tok(GLA_V_W), tok(GLA_V_W), tok(GLA_QK_W),
                   tok(DIFF_QK_W), tok(DIFF_QK_W), tok(DIFF_V_W), tok(S5_WIDTH)],
        out_shape=[sds(GLA_QK_W, F32), sds(GLA_QK_W, F32), sds(GLA_V_W, BF16), sds(GLA_V_W, F32),
                   sds(GLA_QK_W, F32), sds(DIFF_QK_W, BF16), sds(DIFF_QK_W, BF16), sds(DIFF_V_W, BF16),
                   sds(S5_WIDTH, BF16)],
        compiler_params=_params("arbitrary", "arbitrary"),
        name="mixer_in_proj",
    )(h, ada4, norm_pre, w_mix, w_mix, walpha, balpha, w_mix, w_mix)


def _cumsum_rows(tri, x):
    hi = x.astype(BF16)
    r1 = x - hi.astype(F32)
    mid = r1.astype(BF16)
    lo = (r1 - mid.astype(F32)).astype(BF16)
    return _dot(tri, hi) + _dot(tri, mid) + _dot(tri, lo)


def _gla_kernel(q_ref, k_ref, v_ref, r_ref, la_ref, g_ref, o_ref, s_ref, *, n_pairs):
    @pl.when(pl.program_id(1) == 0)
    def _():
        s_ref[...] = jnp.zeros_like(s_ref)

    row = lax.broadcasted_iota(jnp.int32, (GLA_PAIR, GLA_PAIR), 0)
    col = lax.broadcasted_iota(jnp.int32, (GLA_PAIR, GLA_PAIR), 1)
    causal = (col <= row) & ((row // GLA_CHUNK) == (col // GLA_CHUNK))
    tri = jnp.where(causal, 1.0, 0.0).astype(BF16)
    qlane = lax.broadcasted_iota(jnp.int32, (GLA_PAIR, GLA_QK_W), 1)
    tlane = lax.broadcasted_iota(jnp.int32, (GLA_QK_W, GLA_PAIR), 1)
    first = tlane < GLA_CHUNK

    def decays(p):
        rows = slice(p * GLA_PAIR, (p + 1) * GLA_PAIR)
        b = _cumsum_rows(tri, la_ref[rows, :])
        b_t = b.T
        qd = (q_ref[rows, :] * (GLA_DK ** -0.5)) * jnp.exp(b)
        k_t = k_ref[rows, :].T
        bl0 = b_t[:, GLA_CHUNK - 1:GLA_CHUNK]
        bl1 = b_t[:, GLA_PAIR - 1:GLA_PAIR]
        kdec_t = k_t * jnp.exp(jnp.where(first, bl0, bl1) - b_t)
        return dict(
            qh=[jnp.where(qlane // GLA_DK == h, qd, 0.0).astype(BF16) for h in range(GLA_HEADS)],
            kd_t=(k_t * jnp.exp(-b_t)).astype(BF16),
            kdec0=jnp.where(first, kdec_t, 0.0).astype(BF16),
            kdec1=jnp.where(first, 0.0, kdec_t).astype(BF16),
            dec0=jnp.exp(bl0), dec1=jnp.exp(bl1))

    def products(p, d, s0):
        v = v_ref[p * GLA_PAIR:(p + 1) * GLA_PAIR, :]
        ds0, ds1 = [], []
        for h in range(GLA_HEADS):
            hr = slice(h * GLA_DK, (h + 1) * GLA_DK)
            vh = v[:, h * GLA_DV:(h + 1) * GLA_DV]
            ds0.append(_dot(d["kdec0"][hr, :], vh))
            ds1.append(_dot(d["kdec1"][hr, :], vh))
        att = [jnp.where(causal, _dot(qh, d["kd_t"]), 0.0).astype(BF16) for qh in d["qh"]]
        s1 = d["dec0"] * s0 + jnp.concatenate(ds0, axis=0)
        s2 = d["dec1"] * s1 + jnp.concatenate(ds1, axis=0)
        s0b = s0.astype(BF16)
        s1b = s1.astype(BF16)
        inter = [jnp.concatenate([_dot(qh[:GLA_CHUNK], s0b), _dot(qh[GLA_CHUNK:], s1b)], axis=0)
                 for qh in d["qh"]]
        return s2, (v, att, inter)

    def outputs(p, v, att, inter):
        rows = slice(p * GLA_PAIR, (p + 1) * GLA_PAIR)
        r = r_ref[rows, :]
        for h in range(GLA_HEADS):
            vc = slice(h * GLA_DV, (h + 1) * GLA_DV)
            o = _rms(_dot(att[h], v[:, vc]) + inter[h]) * g_ref[:, vc]
            o_ref[rows, vc] = (o * jax.nn.silu(r[:, vc])).astype(BF16)

    state = s_ref[...]
    dec, prod = {}, {}
    for i in range(n_pairs + 2):
        if i < n_pairs:
            dec[i] = decays(i)
        if 0 <= i - 1 < n_pairs:
            state, prod[i - 1] = products(i - 1, dec.pop(i - 1), state)
        if 0 <= i - 2 < n_pairs:
            outputs(i - 2, *prod.pop(i - 2))
    s_ref[...] = state


def _gla_call(gq, gk, gv, gr, la, gla_norm, layer):
    bsz, seq, _ = gq.shape
    tl = GLA_TILE

    def tok(w):
        return pl.BlockSpec((None, tl, w), lambda b, i: (b, i, 0))

    return pl.pallas_call(
        functools.partial(_gla_kernel, n_pairs=tl // GLA_PAIR),
        grid=(bsz, seq // tl),
        in_specs=[tok(GLA_QK_W), tok(GLA_QK_W), tok(GLA_V_W), tok(GLA_V_W), tok(GLA_QK_W),
                  pl.BlockSpec((None, 1, GLA_V_W), lambda b, i: (layer, 0, 0))],
        out_specs=tok(GLA_V_W),
        out_shape=jax.ShapeDtypeStruct((bsz, seq, GLA_V_W), BF16),
        scratch_shapes=[pltpu.VMEM((GLA_QK_W, GLA_DV), F32)],
        compiler_params=_params("arbitrary", "arbitrary"),
        name="gla_mixer",
    )(gq, gk, gv, gr, la, gla_norm.reshape(DEPTH, 1, GLA_V_W))


def _t5_bucket_tiles():
    t = ATT_TILE
    r = np.arange(t)[:, None]
    c = np.arange(t)[None, :]
    max_exact = NUM_BUCKETS // 2
    tiles = []
    for off in range(3):
        dist = off * t + r - c
        d = np.maximum(dist, 1).astype(np.float32)
        large = max_exact + (np.log(d / np.float32(max_exact)) / np.float32(math.log(MAX_DISTANCE / max_exact))
                             * np.float32(NUM_BUCKETS - max_exact)).astype(np.int32)
        large = np.minimum(large, NUM_BUCKETS - 1)
        bucket = np.where(dist < max_exact, dist, large)
        tiles.append(np.where(dist >= 0, bucket, -1).T)
    far = tiles[2]
    assert (far == NUM_BUCKETS - 1).all()
    return np.stack(tiles).astype(np.int32)


def _bias_kernel(rel_ref, bucket_ref, o_ref):
    m = pl.program_id(0)
    bucket = bucket_ref[0:2]
    acc = jnp.full(bucket.shape, -jnp.inf, F32)
    for kk in range(NUM_BUCKETS):
        acc = jnp.where(bucket == kk, rel_ref[kk, m], acc)
    o_ref[0:2] = acc * LOG2E
    o_ref[2] = jnp.full(o_ref.shape[1:], rel_ref[NUM_BUCKETS - 1, m], F32) * LOG2E


def _bias_call(rel_bias):
    n_maps = 2 * DIFF_HEADS
    t = ATT_TILE
    return pl.pallas_call(
        _bias_kernel,
        grid=(n_maps,),
        in_specs=[pl.BlockSpec(memory_space=pltpu.SMEM),
                  pl.BlockSpec((3, t, t), lambda m: (0, 0, 0))],
        out_specs=pl.BlockSpec((None, 3, t, t), lambda m: (m, 0, 0, 0)),
        out_shape=jax.ShapeDtypeStruct((n_maps, 3, t, t), F32),
        compiler_params=_params("arbitrary"),
        name="t5_bias_tiles",
    )(rel_bias, jnp.asarray(_t5_bucket_tiles()))


def _diff_kernel(q_ref, k_ref, v_ref, bias_ref, lam_ref, g_ref, o_ref, *, lam_init, n_tiles):
    t = ATT_TILE
    lv = lam_ref[...]
    lam = (jnp.exp(jnp.sum(lv[0:1] * lv[1:2], axis=-1, keepdims=True))
           - jnp.exp(jnp.sum(lv[2:3] * lv[3:4], axis=-1, keepdims=True)) + lam_init)
    v_t = jnp.concatenate([v_ref[...].astype(F32).T, jnp.ones((ONES_ROWS, v_ref.shape[0]), F32)],
                          axis=0).astype(BF16)
    lane = lax.broadcasted_iota(jnp.int32, (t, 2 * DIFF_DQK), 1)
    far = [bias_ref[mp, 2, 0:1, 0:1] for mp in range(2)]

    tasks = [(qi, kj, mp) for qi in range(n_tiles) for kj in range(qi + 1) for mp in range(2)]
    masked_q = {}
    state = {}
    done = {}

    def logits(qi, kj, mp):
        if qi not in masked_q:
            q = q_ref[qi * t:(qi + 1) * t, :]
            zero = jnp.zeros_like(q)
            masked_q[qi] = (jnp.where(lane < DIFF_DQK, q, zero), jnp.where(lane < DIFF_DQK, zero, q))
        s = _dot_nt(k_ref[kj * t:(kj + 1) * t, :], masked_q[qi][mp])
        return s + bias_ref[mp, qi - kj] if qi - kj < 2 else s

    def accumulate(qi, kj, mp, s):
        shift = 0.0 if qi - kj < 2 else far[mp]
        m_blk = jnp.max(s, axis=0, keepdims=True) + shift
        prev = state.get((qi, mp))
        m_new = m_blk if prev is None else jnp.maximum(prev[0], m_blk)
        p = jnp.exp2((s - (m_new - shift)).astype(BF16))
        pv = _dot(v_t[:, kj * t:(kj + 1) * t], p)
        if prev is None:
            state[(qi, mp)] = (m_new, pv)
        else:
            state[(qi, mp)] = (m_new, jnp.exp2(prev[0] - m_new) * prev[1] + pv)
        if kj == qi:
            _, acc = state.pop((qi, mp))
            done.setdefault(qi, {})[mp] = acc[:DIFF_DV] / acc[DIFF_DV:DIFF_DV + 1]
            if len(done[qi]) == 2:
                o = done[qi][0] - lam * done[qi][1]
                o = o * lax.rsqrt(jnp.mean(o * o, axis=0, keepdims=True) + EPS)
                o_ref[qi * t:(qi + 1) * t, :] = (o.T * g_ref[...] * (1.0 - lam_init)).astype(BF16)
                del done[qi]

    pending = []
    for task in tasks:
        pending.append((task, logits(*task)))
        if len(pending) > ATT_LOOKAHEAD:
            ready, s = pending.pop(0)
            accumulate(*ready, s)
    for ready, s in pending:
        accumulate(*ready, s)


def _diff_call(dq, dk, dv, bias, diff_lambda, diff_norm, layer):
    bsz, seq, _ = dq.shape
    t = ATT_TILE
    lam_init = 0.8 - 0.6 * math.exp(-0.3 * layer)
    return pl.pallas_call(
        functools.partial(_diff_kernel, lam_init=lam_init, n_tiles=seq // t),
        grid=(bsz, DIFF_HEADS),
        in_specs=[
            pl.BlockSpec((None, seq, 2 * DIFF_DQK), lambda b, h: (b, 0, h)),
            pl.BlockSpec((None, seq, 2 * DIFF_DQK), lambda b, h: (b, 0, h)),
            pl.BlockSpec((None, seq, DIFF_DV), lambda b, h: (b, 0, h)),
            pl.BlockSpec((2, 3, t, t), lambda b, h: (h, 0, 0, 0)),
            pl.BlockSpec((None, 4, DIFF_DQK), lambda b, h: (layer, 0, 0)),
            pl.BlockSpec((None, 1, DIFF_DV), lambda b, h: (layer, 0, h)),
        ],
        out_specs=pl.BlockSpec((None, seq, DIFF_DV), lambda b, h: (b, 0, h)),
        out_shape=jax.ShapeDtypeStruct((bsz, seq, DIFF_V_W), BF16),
        compiler_params=_params("arbitrary", "arbitrary"),
        name="diff_attention",
    )(dq, dk, dv, bias, diff_lambda, diff_norm.reshape(DEPTH, 1, DIFF_V_W))


def _s5_prep_kernel(lre_ref, lim_ref, lstep_ref, bre_ref, bim_ref, cre_ref, cim_ref,
                    are_ref, aim_ref, bmat_ref, cmat_ref):
    lr = lre_ref[...]
    li = lim_ref[...]
    dt = jnp.exp(lstep_ref[...])
    mag = jnp.exp(lr * dt)
    a_re = mag * jnp.cos(li * dt)
    a_im = mag * jnp.sin(li * dt)
    are_ref[...] = jnp.broadcast_to(a_re, are_ref.shape)
    aim_ref[...] = jnp.broadcast_to(a_im, aim_ref.shape)
    nr, ni = a_re - 1.0, a_im
    den = lr * lr + li * li
    f_re = (nr * lr + ni * li) / den
    f_im = (ni * lr - nr * li) / den
    ns = S5_BLOCK_STATE
    for gb in range(S5_GROUP_BLOCKS):
        fr = f_re[:, gb * ns:(gb + 1) * ns]
        fi = f_im[:, gb * ns:(gb + 1) * ns]
        br = bre_ref[gb]
        bi = bim_ref[gb]
        bmat_ref[gb, :, :ns] = (fr * br - fi * bi).astype(BF16)
        bmat_ref[gb, :, ns:] = (fr * bi + fi * br).astype(BF16)
        cmat_ref[gb, :ns, :] = cre_ref[gb].astype(BF16)
        cmat_ref[gb, ns:, :] = (-cim_ref[gb]).astype(BF16)


def _s5_prep_call(lam_re, lam_im, log_step, b_re, b_im, c_re, c_im, bsz):
    d = DEPTH
    nb, ch, ns = S5_GROUP_BLOCKS, S5_BLOCK_CH, S5_BLOCK_STATE
    gpb = S5_GROUPS // nb
    eye = jnp.eye(gpb, dtype=F32)

    def in_blocks(b):
        t = b.reshape(d, nb, gpb, S5_STATE, S5_GROUP_CH).transpose(0, 1, 2, 4, 3)
        return (t[:, :, :, :, None, :] * eye[None, None, :, None, :, None]).reshape(d, nb, ch, ns)

    def out_blocks(c):
        t = c.reshape(d, nb, gpb, S5_GROUP_CH, S5_STATE).transpose(0, 1, 4, 2, 3)
        return (t[:, :, None, :, :, :] * eye[None, None, :, None, :, None]).reshape(d, nb, ns, ch)

    def row(x):
        return x.reshape(d, 1, S5_NSTATE)

    lstep = jnp.broadcast_to(log_step[:, :, None], (d, S5_GROUPS, S5_STATE))

    def full(shape):
        nd = len(shape)
        return pl.BlockSpec((None,) + shape, lambda l: (l,) + (0,) * nd)

    return pl.pallas_call(
        _s5_prep_kernel,
        grid=(d,),
        in_specs=[full((1, S5_NSTATE))] * 3 + [full((nb, ch, ns))] * 2 + [full((nb, ns, ch))] * 2,
        out_specs=[full((bsz, S5_NSTATE)), full((bsz, S5_NSTATE)),
                   full((nb, ch, 2 * ns)), full((nb, 2 * ns, ch))],
        out_shape=[jax.ShapeDtypeStruct((d, bsz, S5_NSTATE), F32), jax.ShapeDtypeStruct((d, bsz, S5_NSTATE), F32),
                   jax.ShapeDtypeStruct((d, nb, ch, 2 * ns), BF16), jax.ShapeDtypeStruct((d, nb, 2 * ns, ch), BF16)],
        compiler_params=_params("arbitrary"),
        name="s5_discretize",
    )(row(lam_re), row(lam_im), row(lstep), in_blocks(b_re), in_blocks(b_im), out_blocks(c_re), out_blocks(c_im))


def _s5_kernel(u_ref, are_ref, aim_ref, bmat_ref, cmat_ref, d_ref, wglu_ref, bglu_ref, o_ref,
               tb_ref, xr_ref, xi_ref, sr_ref, si_ref, *, bsz, n_steps):
    @pl.when(pl.program_id(0) == 0)
    def _():
        sr_ref[...] = jnp.zeros_like(sr_ref)
        si_ref[...] = jnp.zeros_like(si_ref)

    ns, ch, nb = S5_BLOCK_STATE, S5_BLOCK_CH, S5_GROUP_BLOCKS
    for b in range(bsz):
        ub = u_ref[b].astype(F32)
        for gb in range(nb):
            tb_ref[gb, pl.ds(b, n_steps, stride=bsz), :] = ub[:, gb * ch:(gb + 1) * ch]
    u = [tb_ref[gb] for gb in range(nb)]

    def drive(gb):
        return _dot(u[gb].astype(BF16), bmat_ref[gb])

    def scan(gb, bu):
        cols = slice(gb * ns, (gb + 1) * ns)
        ar, ai = are_ref[:, cols], aim_ref[:, cols]
        pr, pi = sr_ref[:, cols], si_ref[:, cols]
        for t in range(n_steps):
            rows = slice(t * bsz, (t + 1) * bsz)
            pr, pi = ar * pr - ai * pi + bu[rows, :ns], ar * pi + ai * pr + bu[rows, ns:]
            xr_ref[rows, cols] = pr
            xi_ref[rows, cols] = pi
        sr_ref[:, cols] = pr
        si_ref[:, cols] = pi

    def readout(gb):
        cols = slice(gb * ns, (gb + 1) * ns)
        return (_dot(xr_ref[:, cols].astype(BF16), cmat_ref[gb, :ns, :])
                + _dot(xi_ref[:, cols].astype(BF16), cmat_ref[gb, ns:, :]))

    bu, ys = {0: drive(0)}, []
    for gb in range(nb):
        if gb + 1 < nb:
            bu[gb + 1] = drive(gb + 1)
        scan(gb, bu.pop(gb))
        if gb >= 1:
            ys.append(readout(gb - 1))
    ys.append(readout(nb - 1))

    y = jax.nn.gelu(jnp.concatenate(ys, axis=-1) + d_ref[...] * jnp.concatenate(u, axis=-1))
    z = _dot(y.astype(BF16), wglu_ref[...]) + bglu_ref[...]
    out = z[:, :S5_WIDTH] * jax.nn.sigmoid(z[:, S5_WIDTH:])
    for gb in range(nb):
        tb_ref[gb] = out[:, gb * ch:(gb + 1) * ch]
    for b in range(bsz):
        for gb in range(nb):
            o_ref[b, :, gb * ch:(gb + 1) * ch] = tb_ref[gb, pl.ds(b, n_steps, stride=bsz), :].astype(BF16)


def _s5_call(su, a_re, a_im, bmat, cmat, d_skip, wglu, bglu, layer):
    bsz, seq, _ = su.shape
    tl = S5_TIME_TILE
    tr = tl * bsz
    nb, ch, ns = S5_GROUP_BLOCKS, S5_BLOCK_CH, S5_BLOCK_STATE

    def res(shape):
        nd = len(shape)
        return _resident((None,) + shape, lambda i: (layer,) + (0,) * nd)

    tok = pl.BlockSpec((bsz, tl, S5_WIDTH), lambda i: (0, i, 0))
    return pl.pallas_call(
        functools.partial(_s5_kernel, bsz=bsz, n_steps=tl),
        grid=(seq // tl,),
        in_specs=[tok,
                  res((bsz, S5_NSTATE)), res((bsz, S5_NSTATE)),
                  res((nb, ch, 2 * ns)), res((nb, 2 * ns, ch)),
                  res((1, S5_WIDTH)), res((S5_WIDTH, 2 * S5_WIDTH)), res((1, 2 * S5_WIDTH))],
        out_specs=tok,
        out_shape=jax.ShapeDtypeStruct((bsz, seq, S5_WIDTH), BF16),
        scratch_shapes=[pltpu.VMEM((nb, tr, ch), F32),
                        pltpu.VMEM((tr, S5_NSTATE), F32), pltpu.VMEM((tr, S5_NSTATE), F32),
                        pltpu.VMEM((bsz, S5_NSTATE), F32), pltpu.VMEM((bsz, S5_NSTATE), F32)],
        compiler_params=_params("arbitrary"),
        name="s5_mixer",
    )(su, a_re, a_im, bmat, cmat, d_skip, wglu, bglu)


def _merge_kernel(h_ref, ada_ref, npre_ref, npost_ref, yg_ref, yd_ref, ys_ref, wgate_ref, wbr_ref, wout_ref, o_ref):
    blocks = [slice(r * ROW_BLOCK, (r + 1) * ROW_BLOCK) for r in range(h_ref.shape[0] // ROW_BLOCK)]
    us = [_prenorm_mod(h_ref[rows, :], npre_ref[1:2, :], ada_ref[3:4, :], ada_ref[4:5, :]).astype(BF16)
          for rows in blocks]
    merged = [None] * len(blocks)
    for i, y_ref in enumerate((yg_ref, yd_ref, ys_ref)):
        for r, rows in enumerate(blocks):
            gate = jax.nn.sigmoid(_dot(us[r], wgate_ref[:, i * D_MODEL:(i + 1) * D_MODEL]))
            term = gate * _dot(y_ref[rows, :], wbr_ref[i])
            merged[r] = term if merged[r] is None else merged[r] + term
    ys = [_dot(m.astype(BF16), wout_ref[...]) for m in merged]
    for rows, y in zip(blocks, ys):
        o_ref[rows, :] = h_ref[rows, :] + ada_ref[5:6, :] * (_rms(y) * npost_ref[1:2, :])


def _merge_call(h, ada4, norm_pre, norm_post, y_gla, y_diff, y_s5, w_mix, wbr, wout, layer):
    bsz, seq, _ = h.shape
    tm = TOKEN_TILE

    def tok(w):
        return pl.BlockSpec((None, tm, w), lambda b, i: (b, i, 0))

    return pl.pallas_call(
        _merge_kernel,
        grid=(bsz, seq // tm),
        in_specs=[
            tok(D_MODEL),
            pl.BlockSpec((None, None, ADA_CHUNKS, D_MODEL), lambda b, i: (layer, b, 0, 0)),
            pl.BlockSpec((None, 3, D_MODEL), lambda b, i: (layer, 0, 0)),
            pl.BlockSpec((None, 3, D_MODEL), lambda b, i: (layer, 0, 0)),
            tok(GLA_V_W), tok(DIFF_V_W), tok(S5_WIDTH),
            _resident((None, D_MODEL, MIX_COLS["gate"][0]), lambda b, i: (layer, 0, MIX_COLS["gate"][1])),
            _resident((None, N_BRANCH, S5_WIDTH, D_MODEL), lambda b, i: (layer, 0, 0, 0)),
            _resident((None, D_MODEL, D_MODEL), lambda b, i: (layer, 0, 0)),
        ],
        out_specs=tok(D_MODEL),
        out_shape=jax.ShapeDtypeStruct(h.shape, F32),
        compiler_params=_params("arbitrary", "arbitrary"),
        name="mixer_merge",
    )(h, ada4, norm_pre, norm_post, y_gla, y_diff, y_s5, w_mix, wbr, wout)


def kernel(x, c, rel_bias, w_ada, b_ada, norm_pre, norm_post, ffn1_w_up, ffn1_w_down, ffn2_w_up, ffn2_w_down,
           w_in, gla_w_alpha, gla_b_alpha, gla_norm, diff_lambda, diff_norm, s5_lam_re, s5_lam_im, s5_log_step,
           s5_b_re, s5_b_im, s5_c_re, s5_c_im, s5_d, s5_w_glu, s5_b_glu, w_branch, w_out):
    bsz, seq, _ = x.shape
    assert bsz == SUBLANES, "the S5 scan keeps the batch on the sublane axis"

    f1 = (ffn1_w_up.astype(BF16), ffn1_w_down.astype(BF16))
    f2 = (ffn2_w_up.astype(BF16), ffn2_w_down.astype(BF16))
    w_mix = _regroup_call(w_in.astype(BF16))
    walpha = jnp.pad(gla_w_alpha.astype(BF16), ((0, 0), (0, LANES - GLA_LOWRANK), (0, 0)))
    balpha = gla_b_alpha.reshape(DEPTH, 1, GLA_QK_W)
    wbr = w_branch.astype(BF16)
    wout = w_out.astype(BF16)
    wglu = s5_w_glu.astype(BF16)
    bglu = s5_b_glu.reshape(DEPTH, 1, 2 * S5_WIDTH)
    d_skip = s5_d.reshape(DEPTH, 1, S5_WIDTH)

    ada4 = _ada_call(c, w_ada, b_ada).reshape(DEPTH, bsz, ADA_CHUNKS, D_MODEL)
    bias = _bias_call(rel_bias)
    a_re, a_im, bmat, cmat = _s5_prep_call(s5_lam_re, s5_lam_im, s5_log_step, s5_b_re, s5_b_im,
                                           s5_c_re, s5_c_im, bsz)

    h = x
    for layer in range(DEPTH):
        h = _ffn_call(h, ada4, norm_pre, norm_post, *f1, layer, 0)
        gq, gk, gv, gr, la, dq, dk, dv, su = _inproj_call(h, ada4, norm_pre, w_mix, walpha, balpha, layer)
        y_gla = _gla_call(gq, gk, gv, gr, la, gla_norm, layer)
        y_diff = _diff_call(dq, dk, dv, bias, diff_lambda, diff_norm, layer)
        y_s5 = _s5_call(su, a_re, a_im, bmat, cmat, d_skip, wglu, bglu, layer)
        h = _merge_call(h, ada4, norm_pre, norm_post, y_gla, y_diff, y_s5, w_mix, wbr, wout, layer)
        h = _ffn_call(h, ada4, norm_pre, norm_post, *f2, layer, 2)
    return h
```

```python
import functools
import math

import numpy as np
import jax
import jax.numpy as jnp
from jax import lax
from jax.experimental import pallas as pl
from jax.experimental.pallas import tpu as pltpu

F32 = jnp.float32
BF16 = jnp.bfloat16

D_MODEL = 1024
DEPTH = 2
D_FF = 2816
ADA_CHUNKS = 9
EPS = 1e-6
LOG2E = math.log2(math.e)

GLA_HEADS = 4
GLA_DK = 64
GLA_DV = 128
GLA_LOWRANK = 16
GLA_TAU = 16.0
GLA_CHUNK = 64
GLA_QK_W = GLA_HEADS * GLA_DK
GLA_V_W = GLA_HEADS * GLA_DV

DIFF_HEADS = 4
DIFF_DQK = 64
DIFF_DV = 128
DIFF_QK_W = DIFF_HEADS * 2 * DIFF_DQK
DIFF_V_W = DIFF_HEADS * DIFF_DV
NUM_BUCKETS = 32
MAX_DISTANCE = 128

S5_GROUPS = 32
S5_GROUP_CH = 16
S5_STATE = 64
S5_WIDTH = S5_GROUPS * S5_GROUP_CH
S5_NSTATE = S5_GROUPS * S5_STATE
S5_GROUP_BLOCKS = 4
S5_BLOCK_CH = S5_WIDTH // S5_GROUP_BLOCKS
S5_BLOCK_STATE = S5_NSTATE // S5_GROUP_BLOCKS

N_BRANCH = 3

SUBLANES = 8
LANES = 128
ONES_ROWS = 2 * SUBLANES
VMEM_LIMIT_BYTES = 56 * 1024 * 1024

TOKEN_TILE = 1024
ROW_BLOCK = 512
FF_CHUNK = 256
FF_NCHUNK = D_FF // FF_CHUNK
WEIGHT_STAGE_COLS = 512
FFN_LOOKAHEAD = 1
GLA_PAIR = 2 * GLA_CHUNK
GLA_TILE = 1024
ATT_TILE = 256
ATT_LOOKAHEAD = 4
S5_TIME_TILE = 64
ADA_COL_TILE = 1536


def _dot(a, b):
    return jnp.dot(a, b, preferred_element_type=F32)


def _dot_nt(a, b):
    return lax.dot_general(a, b, (((1,), (1,)), ((), ())), preferred_element_type=F32)


def _rms(x):
    return x * lax.rsqrt(jnp.mean(x * x, axis=-1, keepdims=True) + EPS)


def _prenorm_mod(h, g, shift, scale):
    return (_rms(h) * g) * (1.0 + scale) + shift


def _params(*sem):
    return pltpu.CompilerParams(dimension_semantics=sem, vmem_limit_bytes=VMEM_LIMIT_BYTES)


def _resident(shape, index_map):
    return pl.BlockSpec(shape, index_map, pipeline_mode=pl.Buffered(1))


def _ada_kernel(c_ref, w_ref, b_ref, o_ref):
    cond = jax.nn.silu(c_ref[...]).astype(BF16)
    o_ref[...] = _dot(cond, w_ref[...].astype(BF16)) + b_ref[...]


def _ada_call(c, w_ada, b_ada):
    bsz = c.shape[0]
    n = ADA_CHUNKS * D_MODEL
    return pl.pallas_call(
        _ada_kernel,
        grid=(DEPTH, n // ADA_COL_TILE),
        in_specs=[
            pl.BlockSpec((bsz, D_MODEL), lambda l, j: (0, 0)),
            pl.BlockSpec((None, D_MODEL, ADA_COL_TILE), lambda l, j: (l, 0, j)),
            pl.BlockSpec((None, 1, ADA_COL_TILE), lambda l, j: (l, 0, j)),
        ],
        out_specs=pl.BlockSpec((None, bsz, ADA_COL_TILE), lambda l, j: (l, 0, j)),
        out_shape=jax.ShapeDtypeStruct((DEPTH, bsz, n), F32),
        compiler_params=_params("arbitrary", "arbitrary"),
        name="ada_proj",
    )(c, w_ada, b_ada.reshape(DEPTH, 1, n))


def _stream_cast(src, dst, stage, sem, *, axis, chunk):
    def window(c):
        part = slice(c * chunk, (c + 1) * chunk)
        return (slice(None), part) if axis == 1 else (part, slice(None))

    def copy(c):
        return pltpu.make_async_copy(src.at[window(c)], stage.at[c % 2], sem.at[c % 2])

    n = src.shape[axis] // chunk
    copy(0).start()
    for c in range(n):
        if c + 1 < n:
            copy(c + 1).start()
        copy(c).wait()
        dst[window(c)] = stage[c % 2].astype(BF16)


def _ffn_kernel(h_ref, ada_ref, npre_ref, npost_ref, wup_hbm, wdn_hbm, o_ref,
                wup_ref, wdn_ref, up_stage, dn_stage, up_sem, dn_sem, *, sub, layer):
    a0 = 3 * sub

    @pl.when((pl.program_id(0) == 0) & (pl.program_id(1) == 0))
    def _():
        _stream_cast(wup_hbm.at[layer], wup_ref, up_stage, up_sem, axis=1, chunk=WEIGHT_STAGE_COLS)
        _stream_cast(wdn_hbm.at[layer], wdn_ref, dn_stage, dn_sem, axis=0, chunk=FF_CHUNK)

    def down(j, gate, up):
        mid = (jax.nn.silu(gate) * up).astype(BF16)
        return _dot(mid, wdn_ref[j * FF_CHUNK:(j + 1) * FF_CHUNK, :])

    for r in range(h_ref.shape[0] // ROW_BLOCK):
        rows = slice(r * ROW_BLOCK, (r + 1) * ROW_BLOCK)
        h = h_ref[rows, :]
        u = _prenorm_mod(h, npre_ref[sub:sub + 1, :], ada_ref[a0:a0 + 1, :], ada_ref[a0 + 1:a0 + 2, :])
        u = u.astype(BF16)

        def gate_up(j, u=u):
            lo = j * FF_CHUNK
            return _dot(u, wup_ref[:, lo:lo + FF_CHUNK]), _dot(u, wup_ref[:, D_FF + lo:D_FF + lo + FF_CHUNK])

        acc = None
        pending = []
        for j in range(FF_NCHUNK + FFN_LOOKAHEAD):
            if j < FF_NCHUNK:
                pending.append((j,) + gate_up(j))
            if j >= FFN_LOOKAHEAD:
                d = down(*pending.pop(0))
                acc = d if acc is None else acc + d
        post = _rms(acc) * npost_ref[sub:sub + 1, :]
        o_ref[rows, :] = h + (0.5 * ada_ref[a0 + 2:a0 + 3, :]) * post


def _ffn_call(h, ada4, norm_pre, norm_post, wup, wdn, layer, sub):
    bsz, seq, _ = h.shape
    tm = TOKEN_TILE
    tok = pl.BlockSpec((None, tm, D_MODEL), lambda b, i: (b, i, 0))
    return pl.pallas_call(
        functools.partial(_ffn_kernel, sub=sub, layer=layer),
        grid=(bsz, seq // tm),
        in_specs=[
            tok,
            pl.BlockSpec((None, None, ADA_CHUNKS, D_MODEL), lambda b, i: (layer, b, 0, 0)),
            pl.BlockSpec((None, 3, D_MODEL), lambda b, i: (layer, 0, 0)),
            pl.BlockSpec((None, 3, D_MODEL), lambda b, i: (layer, 0, 0)),
            pl.BlockSpec(memory_space=pl.ANY),
            pl.BlockSpec(memory_space=pl.ANY),
        ],
        out_specs=tok,
        out_shape=jax.ShapeDtypeStruct(h.shape, F32),
        scratch_shapes=[pltpu.VMEM((D_MODEL, 2 * D_FF), BF16), pltpu.VMEM((D_FF, D_MODEL), BF16),
                        pltpu.VMEM((2, D_MODEL, WEIGHT_STAGE_COLS), F32), pltpu.VMEM((2, FF_CHUNK, D_MODEL), F32),
                        pltpu.SemaphoreType.DMA((2,)), pltpu.SemaphoreType.DMA((2,))],
        compiler_params=_params("arbitrary", "arbitrary"),
        name=f"ffn{sub // 2 + 1}",
    )(h, ada4, norm_pre, norm_post, wup, wdn)


def _log_sigmoid(z):
    return jnp.minimum(z, 0.0) - jnp.log1p(jnp.exp(-jnp.abs(z)))


def _inproj_kernel(h_ref, ada_ref, npre_ref, wgla_ref, wa_ref, walpha_ref, balpha_ref, wdiff_ref, ws5_ref,
                   gq_ref, gk_ref, gv_ref, gr_ref, la_ref, dq_ref, dk_ref, dv_ref, su_ref):
    for r in range(h_ref.shape[0] // ROW_BLOCK):
        rows = slice(r * ROW_BLOCK, (r + 1) * ROW_BLOCK)
        u = _prenorm_mod(h_ref[rows, :], npre_ref[1:2, :], ada_ref[3:4, :], ada_ref[4:5, :]).astype(BF16)
        a_low = _dot(u, wa_ref[...]).astype(BF16)
        pg = _dot(u, wgla_ref[...])
        z = _dot(a_low, walpha_ref[...]) + balpha_ref[...]
        gq_ref[rows, :] = pg[:, :GLA_QK_W]
        gk_ref[rows, :] = pg[:, GLA_QK_W:2 * GLA_QK_W]
        gv_ref[rows, :] = pg[:, 2 * GLA_QK_W:2 * GLA_QK_W + GLA_V_W].astype(BF16)
        gr_ref[rows, :] = pg[:, 2 * GLA_QK_W + GLA_V_W:]
        la_ref[rows, :] = _log_sigmoid(z) / GLA_TAU
        pd = _dot(u, wdiff_ref[...])
        dq_ref[rows, :] = (pd[:, :DIFF_QK_W] * (DIFF_DQK ** -0.5 * LOG2E)).astype(BF16)
        dk_ref[rows, :] = pd[:, DIFF_QK_W:2 * DIFF_QK_W].astype(BF16)
        dv_ref[rows, :] = pd[:, 2 * DIFF_QK_W:].astype(BF16)
        su_ref[rows, :] = _dot(u, ws5_ref[...]).astype(BF16)


def _inproj_call(h, ada4, norm_pre, wgla, wa, walpha, balpha, wdiff, ws5, layer):
    bsz, seq, _ = h.shape
    tm = TOKEN_TILE

    def tok(w):
        return pl.BlockSpec((None, tm, w), lambda b, i: (b, i, 0))

    def sds(w, dt):
        return jax.ShapeDtypeStruct((bsz, seq, w), dt)

    def res(arr):
        nd = arr.ndim - 1
        return _resident((None,) + arr.shape[1:], lambda b, i: (layer,) + (0,) * nd)

    return pl.pallas_call(
        _inproj_kernel,
        grid=(bsz, seq // tm),
        in_specs=[
            tok(D_MODEL),
            pl.BlockSpec((None, None, ADA_CHUNKS, D_MODEL), lambda b, i: (layer, b, 0, 0)),
            pl.BlockSpec((None, 3, D_MODEL), lambda b, i: (layer, 0, 0)),
            res(wgla), res(wa), res(walpha), res(balpha), res(wdiff), res(ws5),
        ],
        out_specs=[tok(GLA_QK_W), tok(GLA_QK_W), tok(GLA_V_W), tok(GLA_V_W), tok(GLA_QK_W),
                   tok(DIFF_QK_W), tok(DIFF_QK_W), tok(DIFF_V_W), tok(S5_WIDTH)],
        out_shape=[sds(GLA_QK_W, F32), sds(GLA_QK_W, F32), sds(GLA_V_W, BF16), sds(GLA_V_W, F32),
                   sds(GLA_QK_W, F32), sds(DIFF_QK_W, BF16), sds(DIFF_QK_W, BF16), sds(DIFF_V_W, BF16),
                   sds(S5_WIDTH, BF16)],
        compiler_params=_params("arbitrary", "arbitrary"),
        name="mixer_in_proj",
    )(h, ada4, norm_pre, wgla, wa, walpha, balpha, wdiff, ws5)


def _cumsum_rows(tri, x):
    hi = x.astype(BF16)
    r1 = x - hi.astype(F32)
    mid = r1.astype(BF16)
    lo = (r1 - mid.astype(F32)).astype(BF16)
    return _dot(tri, hi) + _dot(tri, mid) + _dot(tri, lo)


def _gla_kernel(q_ref, k_ref, v_ref, r_ref, la_ref, g_ref, o_ref, s_ref, *, n_pairs):
    @pl.when(pl.program_id(1) == 0)
    def _():
        s_ref[...] = jnp.zeros_like(s_ref)

    row = lax.broadcasted_iota(jnp.int32, (GLA_PAIR, GLA_PAIR), 0)
    col = lax.broadcasted_iota(jnp.int32, (GLA_PAIR, GLA_PAIR), 1)
    causal = (col <= row) & ((row // GLA_CHUNK) == (col // GLA_CHUNK))
    tri = jnp.where(causal, 1.0, 0.0).astype(BF16)
    qlane = lax.broadcasted_iota(jnp.int32, (GLA_PAIR, GLA_QK_W), 1)
    tlane = lax.broadcasted_iota(jnp.int32, (GLA_QK_W, GLA_PAIR), 1)
    first = tlane < GLA_CHUNK

    def decays(p):
        rows = slice(p * GLA_PAIR, (p + 1) * GLA_PAIR)
        b = _cumsum_rows(tri, la_ref[rows, :])
        b_t = b.T
        qd = (q_ref[rows, :] * (GLA_DK ** -0.5)) * jnp.exp(b)
        k_t = k_ref[rows, :].T
        bl0 = b_t[:, GLA_CHUNK - 1:GLA_CHUNK]
        bl1 = b_t[:, GLA_PAIR - 1:GLA_PAIR]
        kdec_t = k_t * jnp.exp(jnp.where(first, bl0, bl1) - b_t)
        return dict(
            qh=[jnp.where(qlane // GLA_DK == h, qd, 0.0).astype(BF16) for h in range(GLA_HEADS)],
            kd_t=(k_t * jnp.exp(-b_t)).astype(BF16),
            kdec0=jnp.where(first, kdec_t, 0.0).astype(BF16),
            kdec1=jnp.where(first, 0.0, kdec_t).astype(BF16),
            dec0=jnp.exp(bl0), dec1=jnp.exp(bl1))

    def products(p, d, s0):
        v = v_ref[p * GLA_PAIR:(p + 1) * GLA_PAIR, :]
        ds0, ds1 = [], []
        for h in range(GLA_HEADS):
            hr = slice(h * GLA_DK, (h + 1) * GLA_DK)
            vh = v[:, h * GLA_DV:(h + 1) * GLA_DV]
            ds0.append(_dot(d["kdec0"][hr, :], vh))
            ds1.append(_dot(d["kdec1"][hr, :], vh))
        att = [jnp.where(causal, _dot(qh, d["kd_t"]), 0.0).astype(BF16) for qh in d["qh"]]
        s1 = d["dec0"] * s0 + jnp.concatenate(ds0, axis=0)
        s2 = d["dec1"] * s1 + jnp.concatenate(ds1, axis=0)
        s0b = s0.astype(BF16)
        s1b = s1.astype(BF16)
        inter = [jnp.concatenate([_dot(qh[:GLA_CHUNK], s0b), _dot(qh[GLA_CHUNK:], s1b)], axis=0)
                 for qh in d["qh"]]
        return s2, (v, att, inter)

    def outputs(p, v, att, inter):
        rows = slice(p * GLA_PAIR, (p + 1) * GLA_PAIR)
        r = r_ref[rows, :]
        for h in range(GLA_HEADS):
            vc = slice(h * GLA_DV, (h + 1) * GLA_DV)
            o = _rms(_dot(att[h], v[:, vc]) + inter[h]) * g_ref[:, vc]
            o_ref[rows, vc] = (o * jax.nn.silu(r[:, vc])).astype(BF16)

    state = s_ref[...]
    dec, prod = {}, {}
    for i in range(n_pairs + 2):
        if i < n_pairs:
            dec[i] = decays(i)
        if 0 <= i - 1 < n_pairs:
            state, prod[i - 1] = products(i - 1, dec.pop(i - 1), state)
        if 0 <= i - 2 < n_pairs:
            outputs(i - 2, *prod.pop(i - 2))
    s_ref[...] = state


def _gla_call(gq, gk, gv, gr, la, gla_norm, layer):
    bsz, seq, _ = gq.shape
    tl = GLA_TILE

    def tok(w):
        return pl.BlockSpec((None, tl, w), lambda b, i: (b, i, 0))

    return pl.pallas_call(
        functools.partial(_gla_kernel, n_pairs=tl // GLA_PAIR),
        grid=(bsz, seq // tl),
        in_specs=[tok(GLA_QK_W), tok(GLA_QK_W), tok(GLA_V_W), tok(GLA_V_W), tok(GLA_QK_W),
                  pl.BlockSpec((None, 1, GLA_V_W), lambda b, i: (layer, 0, 0))],
        out_specs=tok(GLA_V_W),
        out_shape=jax.ShapeDtypeStruct((bsz, seq, GLA_V_W), BF16),
        scratch_shapes=[pltpu.VMEM((GLA_QK_W, GLA_DV), F32)],
        compiler_params=_params("arbitrary", "arbitrary"),
        name="gla_mixer",
    )(gq, gk, gv, gr, la, gla_norm.reshape(DEPTH, 1, GLA_V_W))


def _t5_bucket_tiles():
    t = ATT_TILE
    r = np.arange(t)[:, None]
    c = np.arange(t)[None, :]
    max_exact = NUM_BUCKETS // 2
    tiles = []
    for off in range(3):
        dist = off * t + r - c
        d = np.maximum(dist, 1).astype(np.float32)
        large = max_exact + (np.log(d / np.float32(max_exact)) / np.float32(math.log(MAX_DISTANCE / max_exact))
                             * np.float32(NUM_BUCKETS - max_exact)).astype(np.int32)
        large = np.minimum(large, NUM_BUCKETS - 1)
        bucket = np.where(dist < max_exact, dist, large)
        tiles.append(np.where(dist >= 0, bucket, -1).T)
    far = tiles[2]
    assert (far == NUM_BUCKETS - 1).all()
    return np.stack(tiles).astype(np.int32)


def _bias_kernel(rel_ref, bucket_ref, o_ref):
    m = pl.program_id(0)
    bucket = bucket_ref[0:2]
    acc = jnp.full(bucket.shape, -jnp.inf, F32)
    for kk in range(NUM_BUCKETS):
        acc = jnp.where(bucket == kk, rel_ref[kk, m], acc)
    o_ref[0:2] = acc * LOG2E
    o_ref[2] = jnp.full(o_ref.shape[1:], rel_ref[NUM_BUCKETS - 1, m], F32) * LOG2E


def _bias_call(rel_bias):
    n_maps = 2 * DIFF_HEADS
    t = ATT_TILE
    return pl.pallas_call(
        _bias_kernel,
        grid=(n_maps,),
        in_specs=[pl.BlockSpec(memory_space=pltpu.SMEM),
                  pl.BlockSpec((3, t, t), lambda m: (0, 0, 0))],
        out_specs=pl.BlockSpec((None, 3, t, t), lambda m: (m, 0, 0, 0)),
        out_shape=jax.ShapeDtypeStruct((n_maps, 3, t, t), F32),
        compiler_params=_params("arbitrary"),
        name="t5_bias_tiles",
    )(rel_bias, jnp.asarray(_t5_bucket_tiles()))


def _diff_kernel(q_ref, k_ref, v_ref, bias_ref, lam_ref, g_ref, o_ref, *, lam_init, n_tiles):
    t = ATT_TILE
    lv = lam_ref[...]
    lam = (jnp.exp(jnp.sum(lv[0:1] * lv[1:2], axis=-1, keepdims=True))
           - jnp.exp(jnp.sum(lv[2:3] * lv[3:4], axis=-1, keepdims=True)) + lam_init)
    v_t = jnp.concatenate([v_ref[...].astype(F32).T, jnp.ones((ONES_ROWS, v_ref.shape[0]), F32)],
                          axis=0).astype(BF16)
    lane = lax.broadcasted_iota(jnp.int32, (t, 2 * DIFF_DQK), 1)
    far = [bias_ref[mp, 2, 0:1, 0:1] for mp in range(2)]

    tasks = [(qi, kj, mp) for qi in range(n_tiles) for kj in range(qi + 1) for mp in range(2)]
    masked_q = {}
    state = {}
    done = {}

    def logits(qi, kj, mp):
        if qi not in masked_q:
            q = q_ref[qi * t:(qi + 1) * t, :]
            zero = jnp.zeros_like(q)
            masked_q[qi] = (jnp.where(lane < DIFF_DQK, q, zero), jnp.where(lane < DIFF_DQK, zero, q))
        s = _dot_nt(k_ref[kj * t:(kj + 1) * t, :], masked_q[qi][mp])
        return s + bias_ref[mp, qi - kj] if qi - kj < 2 else s

    def accumulate(qi, kj, mp, s):
        shift = 0.0 if qi - kj < 2 else far[mp]
        m_blk = jnp.max(s, axis=0, keepdims=True) + shift
        prev = state.get((qi, mp))
        m_new = m_blk if prev is None else jnp.maximum(prev[0], m_blk)
        p = jnp.exp2((s - (m_new - shift)).astype(BF16))
        pv = _dot(v_t[:, kj * t:(kj + 1) * t], p)
        if prev is None:
            state[(qi, mp)] = (m_new, pv)
        else:
            state[(qi, mp)] = (m_new, jnp.exp2(prev[0] - m_new) * prev[1] + pv)
        if kj == qi:
            _, acc = state.pop((qi, mp))
            done.setdefault(qi, {})[mp] = acc[:DIFF_DV] / acc[DIFF_DV:DIFF_DV + 1]
            if len(done[qi]) == 2:
                o = done[qi][0] - lam * done[qi][1]
                o = o * lax.rsqrt(jnp.mean(o * o, axis=0, keepdims=True) + EPS)
                o_ref[qi * t:(qi + 1) * t, :] = (o.T * g_ref[...] * (1.0 - lam_init)).astype(BF16)
                del done[qi]

    pending = []
    for task in tasks:
        pending.append((task, logits(*task)))
        if len(pending) > ATT_LOOKAHEAD:
            ready, s = pending.pop(0)
            accumulate(*ready, s)
    for ready, s in pending:
        accumulate(*ready, s)


def _diff_call(dq, dk, dv, bias, diff_lambda, diff_norm, layer):
    bsz, seq, _ = dq.shape
    t = ATT_TILE
    lam_init = 0.8 - 0.6 * math.exp(-0.3 * layer)
    return pl.pallas_call(
        functools.partial(_diff_kernel, lam_init=lam_init, n_tiles=seq // t),
        grid=(bsz, DIFF_HEADS),
        in_specs=[
            pl.BlockSpec((None, seq, 2 * DIFF_DQK), lambda b, h: (b, 0, h)),
            pl.BlockSpec((None, seq, 2 * DIFF_DQK), lambda b, h: (b, 0, h)),
            pl.BlockSpec((None, seq, DIFF_DV), lambda b, h: (b, 0, h)),
            pl.BlockSpec((2, 3, t, t), lambda b, h: (h, 0, 0, 0)),
            pl.BlockSpec((None, 4, DIFF_DQK), lambda b, h: (layer, 0, 0)),
            pl.BlockSpec((None, 1, DIFF_DV), lambda b, h: (layer, 0, h)),
        ],
        out_specs=pl.BlockSpec((None, seq, DIFF_DV), lambda b, h: (b, 0, h)),
        out_shape=jax.ShapeDtypeStruct((bsz, seq, DIFF_V_W), BF16),
        compiler_params=_params("arbitrary", "arbitrary"),
        name="diff_attention",
    )(dq, dk, dv, bias, diff_lambda, diff_norm.reshape(DEPTH, 1, DIFF_V_W))


def _s5_prep_kernel(lre_ref, lim_ref, lstep_ref, bre_ref, bim_ref, cre_ref, cim_ref,
                    are_ref, aim_ref, bmat_ref, cmat_ref):
    lr = lre_ref[...]
    li = lim_ref[...]
    dt = jnp.exp(lstep_ref[...])
    mag = jnp.exp(lr * dt)
    a_re = mag * jnp.cos(li * dt)
    a_im = mag * jnp.sin(li * dt)
    are_ref[...] = jnp.broadcast_to(a_re, are_ref.shape)
    aim_ref[...] = jnp.broadcast_to(a_im, aim_ref.shape)
    nr, ni = a_re - 1.0, a_im
    den = lr * lr + li * li
    f_re = (nr * lr + ni * li) / den
    f_im = (ni * lr - nr * li) / den
    ns = S5_BLOCK_STATE
    for gb in range(S5_GROUP_BLOCKS):
        fr = f_re[:, gb * ns:(gb + 1) * ns]
        fi = f_im[:, gb * ns:(gb + 1) * ns]
        br = bre_ref[gb]
        bi = bim_ref[gb]
        bmat_ref[gb, :, :ns] = (fr * br - fi * bi).astype(BF16)
        bmat_ref[gb, :, ns:] = (fr * bi + fi * br).astype(BF16)
        cmat_ref[gb, :ns, :] = cre_ref[gb].astype(BF16)
        cmat_ref[gb, ns:, :] = (-cim_ref[gb]).astype(BF16)


def _s5_prep_call(lam_re, lam_im, log_step, b_re, b_im, c_re, c_im, bsz):
    d = DEPTH
    nb, ch, ns = S5_GROUP_BLOCKS, S5_BLOCK_CH, S5_BLOCK_STATE
    gpb = S5_GROUPS // nb
    eye = jnp.eye(gpb, dtype=F32)

    def in_blocks(b):
        t = b.reshape(d, nb, gpb, S5_STATE, S5_GROUP_CH).transpose(0, 1, 2, 4, 3)
        return (t[:, :, :, :, None, :] * eye[None, None, :, None, :, None]).reshape(d, nb, ch, ns)

    def out_blocks(c):
        t = c.reshape(d, nb, gpb, S5_GROUP_CH, S5_STATE).transpose(0, 1, 4, 2, 3)
        return (t[:, :, None, :, :, :] * eye[None, None, :, None, :, None]).reshape(d, nb, ns, ch)

    def row(x):
        return x.reshape(d, 1, S5_NSTATE)

    lstep = jnp.broadcast_to(log_step[:, :, None], (d, S5_GROUPS, S5_STATE))

    def full(shape):
        nd = len(shape)
        return pl.BlockSpec((None,) + shape, lambda l: (l,) + (0,) * nd)

    return pl.pallas_call(
        _s5_prep_kernel,
        grid=(d,),
        in_specs=[full((1, S5_NSTATE))] * 3 + [full((nb, ch, ns))] * 2 + [full((nb, ns, ch))] * 2,
        out_specs=[full((bsz, S5_NSTATE)), full((bsz, S5_NSTATE)),
                   full((nb, ch, 2 * ns)), full((nb, 2 * ns, ch))],
        out_shape=[jax.ShapeDtypeStruct((d, bsz, S5_NSTATE), F32), jax.ShapeDtypeStruct((d, bsz, S5_NSTATE), F32),
                   jax.ShapeDtypeStruct((d, nb, ch, 2 * ns), BF16), jax.ShapeDtypeStruct((d, nb, 2 * ns, ch), BF16)],
        compiler_params=_params("arbitrary"),
        name="s5_discretize",
    )(row(lam_re), row(lam_im), row(lstep), in_blocks(b_re), in_blocks(b_im), out_blocks(c_re), out_blocks(c_im))


def _s5_kernel(u_ref, are_ref, aim_ref, bmat_ref, cmat_ref, d_ref, wglu_ref, bglu_ref, o_ref,
               tb_ref, xr_ref, xi_ref, sr_ref, si_ref, *, bsz, n_steps):
    @pl.when(pl.program_id(0) == 0)
    def _():
        sr_ref[...] = jnp.zeros_like(sr_ref)
        si_ref[...] = jnp.zeros_like(si_ref)

    ns, ch, nb = S5_BLOCK_STATE, S5_BLOCK_CH, S5_GROUP_BLOCKS
    for b in range(bsz):
        ub = u_ref[b].astype(F32)
        for gb in range(nb):
            tb_ref[gb, pl.ds(b, n_steps, stride=bsz), :] = ub[:, gb * ch:(gb + 1) * ch]
    u = [tb_ref[gb] for gb in range(nb)]

    def drive(gb):
        return _dot(u[gb].astype(BF16), bmat_ref[gb])

    def scan(gb, bu):
        cols = slice(gb * ns, (gb + 1) * ns)
        ar, ai = are_ref[:, cols], aim_ref[:, cols]
        pr, pi = sr_ref[:, cols], si_ref[:, cols]
        for t in range(n_steps):
            rows = slice(t * bsz, (t + 1) * bsz)
            pr, pi = ar * pr - ai * pi + bu[rows, :ns], ar * pi + ai * pr + bu[rows, ns:]
            xr_ref[rows, cols] = pr
            xi_ref[rows, cols] = pi
        sr_ref[:, cols] = pr
        si_ref[:, cols] = pi

    def readout(gb):
        cols = slice(gb * ns, (gb + 1) * ns)
        return (_dot(xr_ref[:, cols].astype(BF16), cmat_ref[gb, :ns, :])
                + _dot(xi_ref[:, cols].astype(BF16), cmat_ref[gb, ns:, :]))

    bu, ys = {0: drive(0)}, []
    for gb in range(nb):
        if gb + 1 < nb:
            bu[gb + 1] = drive(gb + 1)
        scan(gb, bu.pop(gb))
        if gb >= 1:
            ys.append(readout(gb - 1))
    ys.append(readout(nb - 1))

    y = jax.nn.gelu(jnp.concatenate(ys, axis=-1) + d_ref[...] * jnp.concatenate(u, axis=-1))
    z = _dot(y.astype(BF16), wglu_ref[...]) + bglu_ref[...]
    out = z[:, :S5_WIDTH] * jax.nn.sigmoid(z[:, S5_WIDTH:])
    for gb in range(nb):
        tb_ref[gb] = out[:, gb * ch:(gb + 1) * ch]
    for b in range(bsz):
        for gb in range(nb):
            o_ref[b, :, gb * ch:(gb + 1) * ch] = tb_ref[gb, pl.ds(b, n_steps, stride=bsz), :].astype(BF16)


def _s5_call(su, a_re, a_im, bmat, cmat, d_skip, wglu, bglu, layer):
    bsz, seq, _ = su.shape
    tl = S5_TIME_TILE
    tr = tl * bsz
    nb, ch, ns = S5_GROUP_BLOCKS, S5_BLOCK_CH, S5_BLOCK_STATE

    def res(shape):
        nd = len(shape)
        return _resident((None,) + shape, lambda i: (layer,) + (0,) * nd)

    tok = pl.BlockSpec((bsz, tl, S5_WIDTH), lambda i: (0, i, 0))
    return pl.pallas_call(
        functools.partial(_s5_kernel, bsz=bsz, n_steps=tl),
        grid=(seq // tl,),
        in_specs=[tok,
                  res((bsz, S5_NSTATE)), res((bsz, S5_NSTATE)),
                  res((nb, ch, 2 * ns)), res((nb, 2 * ns, ch)),
                  res((1, S5_WIDTH)), res((S5_WIDTH, 2 * S5_WIDTH)), res((1, 2 * S5_WIDTH))],
        out_specs=tok,
        out_shape=jax.ShapeDtypeStruct((bsz, seq, S5_WIDTH), BF16),
        scratch_shapes=[pltpu.VMEM((nb, tr, ch), F32),
                        pltpu.VMEM((tr, S5_NSTATE), F32), pltpu.VMEM((tr, S5_NSTATE), F32),
                        pltpu.VMEM((bsz, S5_NSTATE), F32), pltpu.VMEM((bsz, S5_NSTATE), F32)],
        compiler_params=_params("arbitrary"),
        name="s5_mixer",
    )(su, a_re, a_im, bmat, cmat, d_skip, wglu, bglu)


def _merge_kernel(h_ref, ada_ref, npre_ref, npost_ref, yg_ref, yd_ref, ys_ref, wgate_ref, wbr_ref, wout_ref, o_ref):
    blocks = [slice(r * ROW_BLOCK, (r + 1) * ROW_BLOCK) for r in range(h_ref.shape[0] // ROW_BLOCK)]
    us = [_prenorm_mod(h_ref[rows, :], npre_ref[1:2, :], ada_ref[3:4, :], ada_ref[4:5, :]).astype(BF16)
          for rows in blocks]
    merged = [None] * len(blocks)
    for i, y_ref in enumerate((yg_ref, yd_ref, ys_ref)):
        for r, rows in enumerate(blocks):
            gate = jax.nn.sigmoid(_dot(us[r], wgate_ref[:, i * D_MODEL:(i + 1) * D_MODEL]))
            term = gate * _dot(y_ref[rows, :], wbr_ref[i])
            merged[r] = term if merged[r] is None else merged[r] + term
    ys = [_dot(m.astype(BF16), wout_ref[...]) for m in merged]
    for rows, y in zip(blocks, ys):
        o_ref[rows, :] = h_ref[rows, :] + ada_ref[5:6, :] * (_rms(y) * npost_ref[1:2, :])


def _merge_call(h, ada4, norm_pre, norm_post, y_gla, y_diff, y_s5, wgate, wbr, wout, layer):
    bsz, seq, _ = h.shape
    tm = TOKEN_TILE

    def tok(w):
        return pl.BlockSpec((None, tm, w), lambda b, i: (b, i, 0))

    return pl.pallas_call(
        _merge_kernel,
        grid=(bsz, seq // tm),
        in_specs=[
            tok(D_MODEL),
            pl.BlockSpec((None, None, ADA_CHUNKS, D_MODEL), lambda b, i: (layer, b, 0, 0)),
            pl.BlockSpec((None, 3, D_MODEL), lambda b, i: (layer, 0, 0)),
            pl.BlockSpec((None, 3, D_MODEL), lambda b, i: (layer, 0, 0)),
            tok(GLA_V_W), tok(DIFF_V_W), tok(S5_WIDTH),
            _resident((None, D_MODEL, N_BRANCH * D_MODEL), lambda b, i: (layer, 0, 0)),
            _resident((None, N_BRANCH, S5_WIDTH, D_MODEL), lambda b, i: (layer, 0, 0, 0)),
            _resident((None, D_MODEL, D_MODEL), lambda b, i: (layer, 0, 0)),
        ],
        out_specs=tok(D_MODEL),
        out_shape=jax.ShapeDtypeStruct(h.shape, F32),
        compiler_params=_params("arbitrary", "arbitrary"),
        name="mixer_merge",
    )(h, ada4, norm_pre, norm_post, y_gla, y_diff, y_s5, wgate, wbr, wout)


def kernel(x, c, rel_bias, w_ada, b_ada, norm_pre, norm_post, ffn1_w_up, ffn1_w_down, ffn2_w_up, ffn2_w_down,
           w_in, gla_w_alpha, gla_b_alpha, gla_norm, diff_lambda, diff_norm, s5_lam_re, s5_lam_im, s5_log_step,
           s5_b_re, s5_b_im, s5_c_re, s5_c_im, s5_d, s5_w_glu, s5_b_glu, w_branch, w_out):
    bsz, seq, _ = x.shape
    assert bsz == SUBLANES, "the S5 scan keeps the batch on the sublane axis"

    f1 = (ffn1_w_up, ffn1_w_down)
    f2 = (ffn2_w_up, ffn2_w_down)
    o_alpha = 2 * GLA_QK_W + 2 * GLA_V_W
    o_diff = o_alpha + GLA_LOWRANK
    o_s5 = o_diff + 2 * DIFF_QK_W + DIFF_V_W
    o_gate = o_s5 + S5_WIDTH
    w_in_b = w_in.astype(BF16)
    wgla = w_in_b[:, :, :o_alpha]
    wa = jnp.pad(w_in_b[:, :, o_alpha:o_diff], ((0, 0), (0, 0), (0, LANES - GLA_LOWRANK)))
    wdiff = w_in_b[:, :, o_diff:o_s5]
    ws5 = w_in_b[:, :, o_s5:o_gate]
    wgate = w_in_b[:, :, o_gate:]
    walpha = jnp.pad(gla_w_alpha.astype(BF16), ((0, 0), (0, LANES - GLA_LOWRANK), (0, 0)))
    balpha = gla_b_alpha.reshape(DEPTH, 1, GLA_QK_W)
    wbr = w_branch.astype(BF16)
    wout = w_out.astype(BF16)
    wglu = s5_w_glu.astype(BF16)
    bglu = s5_b_glu.reshape(DEPTH, 1, 2 * S5_WIDTH)
    d_skip = s5_d.reshape(DEPTH, 1, S5_WIDTH)

    ada4 = _ada_call(c, w_ada, b_ada).reshape(DEPTH, bsz, ADA_CHUNKS, D_MODEL)
    bias = _bias_call(rel_bias)
    a_re, a_im, bmat, cmat = _s5_prep_call(s5_lam_re, s5_lam_im, s5_log_step, s5_b_re, s5_b_im,
                                           s5_c_re, s5_c_im, bsz)

    h = x
    for layer in range(DEPTH):
        h = _ffn_call(h, ada4, norm_pre, norm_post, *f1, layer, 0)
        gq, gk, gv, gr, la, dq, dk, dv, su = _inproj_call(h, ada4, norm_pre, wgla, wa, walpha, balpha,
                                                          wdiff, ws5, layer)
        y_gla = _gla_call(gq, gk, gv, gr, la, gla_norm, layer)
        y_diff = _diff_call(dq, dk, dv, bias, diff_lambda, diff_norm, layer)
        y_s5 = _s5_call(su, a_re, a_im, bmat, cmat, d_skip, wglu, bglu, layer)
        h = _merge_call(h, ada4, norm_pre, norm_post, y_gla, y_diff, y_s5, wgate, wbr, wout, layer)
        h = _ffn_call(h, ada4, norm_pre, norm_post, *f2, layer, 2)
    return h
```

```python
import functools
import math

import numpy as np
import jax
import jax.numpy as jnp
from jax import lax
from jax.experimental import pallas as pl
from jax.experimental.pallas import tpu as pltpu

F32 = jnp.float32
BF16 = jnp.bfloat16

D_MODEL = 1024
DEPTH = 2
D_FF = 2816
ADA_CHUNKS = 9
EPS = 1e-6
LOG2E = math.log2(math.e)

GLA_HEADS = 4
GLA_DK = 64
GLA_DV = 128
GLA_LOWRANK = 16
GLA_TAU = 16.0
GLA_CHUNK = 64
GLA_QK_W = GLA_HEADS * GLA_DK
GLA_V_W = GLA_HEADS * GLA_DV

DIFF_HEADS = 4
DIFF_DQK = 64
DIFF_DV = 128
DIFF_QK_W = DIFF_HEADS * 2 * DIFF_DQK
DIFF_V_W = DIFF_HEADS * DIFF_DV
NUM_BUCKETS = 32
MAX_DISTANCE = 128

S5_GROUPS = 32
S5_GROUP_CH = 16
S5_STATE = 64
S5_WIDTH = S5_GROUPS * S5_GROUP_CH
S5_NSTATE = S5_GROUPS * S5_STATE
S5_GROUP_BLOCKS = 4
S5_BLOCK_CH = S5_WIDTH // S5_GROUP_BLOCKS
S5_BLOCK_STATE = S5_NSTATE // S5_GROUP_BLOCKS

N_BRANCH = 3

SUBLANES = 8
LANES = 128
ONES_ROWS = 2 * SUBLANES
VMEM_LIMIT_BYTES = 56 * 1024 * 1024

TOKEN_TILE = 1024
ROW_BLOCK = 512
FF_CHUNK = 256
FF_NCHUNK = D_FF // FF_CHUNK
WEIGHT_STAGE_COLS = 512
WEIGHT_STAGE_SLOTS = 4
FFN_LOOKAHEAD = 1
GLA_PAIR = 2 * GLA_CHUNK
GLA_TILE = 1024
ATT_TILE = 256
ATT_LOOKAHEAD = 4
S5_TIME_TILE = 64
ADA_COL_TILE = 1536


def _dot(a, b):
    return jnp.dot(a, b, preferred_element_type=F32)


def _dot_nt(a, b):
    return lax.dot_general(a, b, (((1,), (1,)), ((), ())), preferred_element_type=F32)


def _rms(x):
    return x * lax.rsqrt(jnp.mean(x * x, axis=-1, keepdims=True) + EPS)


def _prenorm_mod(h, g, shift, scale):
    return (_rms(h) * g) * (1.0 + scale) + shift


def _params(*sem):
    return pltpu.CompilerParams(dimension_semantics=sem, vmem_limit_bytes=VMEM_LIMIT_BYTES)


def _resident(shape, index_map):
    return pl.BlockSpec(shape, index_map, pipeline_mode=pl.Buffered(1))


def _ada_kernel(c_ref, w_ref, b_ref, o_ref):
    cond = jax.nn.silu(c_ref[...]).astype(BF16)
    o_ref[...] = _dot(cond, w_ref[...].astype(BF16)) + b_ref[...]


def _ada_call(c, w_ada, b_ada):
    bsz = c.shape[0]
    n = ADA_CHUNKS * D_MODEL
    return pl.pallas_call(
        _ada_kernel,
        grid=(DEPTH, n // ADA_COL_TILE),
        in_specs=[
            pl.BlockSpec((bsz, D_MODEL), lambda l, j: (0, 0)),
            pl.BlockSpec((None, D_MODEL, ADA_COL_TILE), lambda l, j: (l, 0, j)),
            pl.BlockSpec((None, 1, ADA_COL_TILE), lambda l, j: (l, 0, j)),
        ],
        out_specs=pl.BlockSpec((None, bsz, ADA_COL_TILE), lambda l, j: (l, 0, j)),
        out_shape=jax.ShapeDtypeStruct((DEPTH, bsz, n), F32),
        compiler_params=_params("arbitrary", "arbitrary"),
        name="ada_proj",
    )(c, w_ada, b_ada.reshape(DEPTH, 1, n))


def _stream_cast(src, dst, stage, sem, *, axis, chunk):
    slots = stage.shape[0]

    def window(c):
        part = slice(c * chunk, (c + 1) * chunk)
        return (slice(None), part) if axis == 1 else (part, slice(None))

    def copy(c):
        return pltpu.make_async_copy(src.at[window(c)], stage.at[c % slots], sem.at[c % slots])

    n = src.shape[axis] // chunk
    for c in range(min(slots - 1, n)):
        copy(c).start()
    for c in range(n):
        if c + slots - 1 < n:
            copy(c + slots - 1).start()
        copy(c).wait()
        dst[window(c)] = stage[c % slots].astype(BF16)


def _ffn_kernel(h_ref, ada_ref, npre_ref, npost_ref, wup_hbm, wdn_hbm, o_ref,
                wup_ref, wdn_ref, up_stage, dn_stage, up_sem, dn_sem, *, sub, layer):
    a0 = 3 * sub

    @pl.when((pl.program_id(0) == 0) & (pl.program_id(1) == 0))
    def _():
        _stream_cast(wup_hbm.at[layer], wup_ref, up_stage, up_sem, axis=1, chunk=WEIGHT_STAGE_COLS)
        _stream_cast(wdn_hbm.at[layer], wdn_ref, dn_stage, dn_sem, axis=0, chunk=FF_CHUNK)

    def down(j, gate, up):
        mid = (jax.nn.silu(gate) * up).astype(BF16)
        return _dot(mid, wdn_ref[j * FF_CHUNK:(j + 1) * FF_CHUNK, :])

    for r in range(h_ref.shape[0] // ROW_BLOCK):
        rows = slice(r * ROW_BLOCK, (r + 1) * ROW_BLOCK)
        h = h_ref[rows, :]
        u = _prenorm_mod(h, npre_ref[sub:sub + 1, :], ada_ref[a0:a0 + 1, :], ada_ref[a0 + 1:a0 + 2, :])
        u = u.astype(BF16)

        def gate_up(j, u=u):
            lo = j * FF_CHUNK
            return _dot(u, wup_ref[:, lo:lo + FF_CHUNK]), _dot(u, wup_ref[:, D_FF + lo:D_FF + lo + FF_CHUNK])

        acc = None
        pending = []
        for j in range(FF_NCHUNK + FFN_LOOKAHEAD):
            if j < FF_NCHUNK:
                pending.append((j,) + gate_up(j))
            if j >= FFN_LOOKAHEAD:
                d = down(*pending.pop(0))
                acc = d if acc is None else acc + d
        post = _rms(acc) * npost_ref[sub:sub + 1, :]
        o_ref[rows, :] = h + (0.5 * ada_ref[a0 + 2:a0 + 3, :]) * post


def _ffn_call(h, ada4, norm_pre, norm_post, wup, wdn, layer, sub):
    bsz, seq, _ = h.shape
    tm = TOKEN_TILE
    tok = pl.BlockSpec((None, tm, D_MODEL), lambda b, i: (b, i, 0))
    return pl.pallas_call(
        functools.partial(_ffn_kernel, sub=sub, layer=layer),
        grid=(bsz, seq // tm),
        in_specs=[
            tok,
            pl.BlockSpec((None, None, ADA_CHUNKS, D_MODEL), lambda b, i: (layer, b, 0, 0)),
            pl.BlockSpec((None, 3, D_MODEL), lambda b, i: (layer, 0, 0)),
            pl.BlockSpec((None, 3, D_MODEL), lambda b, i: (layer, 0, 0)),
            pl.BlockSpec(memory_space=pl.ANY),
            pl.BlockSpec(memory_space=pl.ANY),
        ],
        out_specs=tok,
        out_shape=jax.ShapeDtypeStruct(h.shape, F32),
        scratch_shapes=[pltpu.VMEM((D_MODEL, 2 * D_FF), BF16), pltpu.VMEM((D_FF, D_MODEL), BF16),
                        pltpu.VMEM((WEIGHT_STAGE_SLOTS, D_MODEL, WEIGHT_STAGE_COLS), F32),
                        pltpu.VMEM((WEIGHT_STAGE_SLOTS, FF_CHUNK, D_MODEL), F32),
                        pltpu.SemaphoreType.DMA((WEIGHT_STAGE_SLOTS,)),
                        pltpu.SemaphoreType.DMA((WEIGHT_STAGE_SLOTS,))],
        compiler_params=_params("arbitrary", "arbitrary"),
        name=f"ffn{sub // 2 + 1}",
    )(h, ada4, norm_pre, norm_post, wup, wdn)


def _log_sigmoid(z):
    return jnp.minimum(z, 0.0) - jnp.log1p(jnp.exp(-jnp.abs(z)))


def _inproj_kernel(h_ref, ada_ref, npre_ref, wgla_ref, wa_ref, walpha_ref, balpha_ref, wdiff_ref, ws5_ref,
                   gq_ref, gk_ref, gv_ref, gr_ref, la_ref, dq_ref, dk_ref, dv_ref, su_ref):
    for r in range(h_ref.shape[0] // ROW_BLOCK):
        rows = slice(r * ROW_BLOCK, (r + 1) * ROW_BLOCK)
        u = _prenorm_mod(h_ref[rows, :], npre_ref[1:2, :], ada_ref[3:4, :], ada_ref[4:5, :]).astype(BF16)
        a_low = _dot(u, wa_ref[...]).astype(BF16)
        pg = _dot(u, wgla_ref[...])
        z = _dot(a_low, walpha_ref[...]) + balpha_ref[...]
        gq_ref[rows, :] = pg[:, :GLA_QK_W]
        gk_ref[rows, :] = pg[:, GLA_QK_W:2 * GLA_QK_W]
        gv_ref[rows, :] = pg[:, 2 * GLA_QK_W:2 * GLA_QK_W + GLA_V_W].astype(BF16)
        gr_ref[rows, :] = pg[:, 2 * GLA_QK_W + GLA_V_W:]
        la_ref[rows, :] = _log_sigmoid(z) / GLA_TAU
        pd = _dot(u, wdiff_ref[...])
        dq_ref[rows, :] = (pd[:, :DIFF_QK_W] * (DIFF_DQK ** -0.5 * LOG2E)).astype(BF16)
        dk_ref[rows, :] = pd[:, DIFF_QK_W:2 * DIFF_QK_W].astype(BF16)
        dv_ref[rows, :] = pd[:, 2 * DIFF_QK_W:].astype(BF16)
        su_ref[rows, :] = _dot(u, ws5_ref[...]).astype(BF16)


def _inproj_call(h, ada4, norm_pre, wgla, wa, walpha, balpha, wdiff, ws5, layer):
    bsz, seq, _ = h.shape
    tm = TOKEN_TILE

    def tok(w):
        return pl.BlockSpec((None, tm, w), lambda b, i: (b, i, 0))

    def sds(w, dt):
        return jax.ShapeDtypeStruct((bsz, seq, w), dt)

    def res(arr):
        nd = arr.ndim - 1
        return _resident((None,) + arr.shape[1:], lambda b, i: (layer,) + (0,) * nd)

    return pl.pallas_call(
        _inproj_kernel,
        grid=(bsz, seq // tm),
        in_specs=[
            tok(D_MODEL),
            pl.BlockSpec((None, None, ADA_CHUNKS, D_MODEL), lambda b, i: (layer, b, 0, 0)),
            pl.BlockSpec((None, 3, D_MODEL), lambda b, i: (layer, 0, 0)),
            res(wgla), res(wa), res(walpha), res(balpha), res(wdiff), res(ws5),
        ],
        out_specs=[tok(GLA_QK_W), tok(GLA_QK_W), tok(GLA_V_W), tok(GLA_V_W), tok(GLA_QK_W),
                   tok(DIFF_QK_W), tok(DIFF_QK_W), tok(DIFF_V_W), tok(S5_WIDTH)],
        out_shape=[sds(GLA_QK_W, F32), sds(GLA_QK_W, F32), sds(GLA_V_W, BF16), sds(GLA_V_W, F32),
                   sds(GLA_QK_W, F32), sds(DIFF_QK_W, BF16), sds(DIFF_QK_W, BF16), sds(DIFF_V_W, BF16),
                   sds(S5_WIDTH, BF16)],
        compiler_params=_params("arbitrary", "arbitrary"),
        name="mixer_in_proj",
    )(h, ada4, norm_pre, wgla, wa, walpha, balpha, wdiff, ws5)


def _cumsum_rows(tri, x):
    hi = x.astype(BF16)
    r1 = x - hi.astype(F32)
    mid = r1.astype(BF16)
    lo = (r1 - mid.astype(F32)).astype(BF16)
    return _dot(tri, hi) + _dot(tri, mid) + _dot(tri, lo)


def _gla_kernel(q_ref, k_ref, v_ref, r_ref, la_ref, g_ref, o_ref, s_ref, *, n_pairs):
    @pl.when(pl.program_id(1) == 0)
    def _():
        s_ref[...] = jnp.zeros_like(s_ref)

    row = lax.broadcasted_iota(jnp.int32, (GLA_PAIR, GLA_PAIR), 0)
    col = lax.broadcasted_iota(jnp.int32, (GLA_PAIR, GLA_PAIR), 1)
    causal = (col <= row) & ((row // GLA_CHUNK) == (col // GLA_CHUNK))
    tri = jnp.where(causal, 1.0, 0.0).astype(BF16)
    qlane = lax.broadcasted_iota(jnp.int32, (GLA_PAIR, GLA_QK_W), 1)
    tlane = lax.broadcasted_iota(jnp.int32, (GLA_QK_W, GLA_PAIR), 1)
    first = tlane < GLA_CHUNK

    def decays(p):
        rows = slice(p * GLA_PAIR, (p + 1) * GLA_PAIR)
        b = _cumsum_rows(tri, la_ref[rows, :])
        b_t = b.T
        qd = (q_ref[rows, :] * (GLA_DK ** -0.5)) * jnp.exp(b)
        k_t = k_ref[rows, :].T
        bl0 = b_t[:, GLA_CHUNK - 1:GLA_CHUNK]
        bl1 = b_t[:, GLA_PAIR - 1:GLA_PAIR]
        kdec_t = k_t * jnp.exp(jnp.where(first, bl0, bl1) - b_t)
        return dict(
            qh=[jnp.where(qlane // GLA_DK == h, qd, 0.0).astype(BF16) for h in range(GLA_HEADS)],
            kd_t=(k_t * jnp.exp(-b_t)).astype(BF16),
            kdec0=jnp.where(first, kdec_t, 0.0).astype(BF16),
            kdec1=jnp.where(first, 0.0, kdec_t).astype(BF16),
            dec0=jnp.exp(bl0), dec1=jnp.exp(bl1))

    def products(p, d, s0):
        v = v_ref[p * GLA_PAIR:(p + 1) * GLA_PAIR, :]
        ds0, ds1 = [], []
        for h in range(GLA_HEADS):
            hr = slice(h * GLA_DK, (h + 1) * GLA_DK)
            vh = v[:, h * GLA_DV:(h + 1) * GLA_DV]
            ds0.append(_dot(d["kdec0"][hr, :], vh))
            ds1.append(_dot(d["kdec1"][hr, :], vh))
        att = [jnp.where(causal, _dot(qh, d["kd_t"]), 0.0).astype(BF16) for qh in d["qh"]]
        s1 = d["dec0"] * s0 + jnp.concatenate(ds0, axis=0)
        s2 = d["dec1"] * s1 + jnp.concatenate(ds1, axis=0)
        s0b = s0.astype(BF16)
        s1b = s1.astype(BF16)
        inter = [jnp.concatenate([_dot(qh[:GLA_CHUNK], s0b), _dot(qh[GLA_CHUNK:], s1b)], axis=0)
                 for qh in d["qh"]]
        return s2, (v, att, inter)

    def outputs(p, v, att, inter):
        rows = slice(p * GLA_PAIR, (p + 1) * GLA_PAIR)
        r = r_ref[rows, :]
        for h in range(GLA_HEADS):
            vc = slice(h * GLA_DV, (h + 1) * GLA_DV)
            o = _rms(_dot(att[h], v[:, vc]) + inter[h]) * g_ref[:, vc]
            o_ref[rows, vc] = (o * jax.nn.silu(r[:, vc])).astype(BF16)

    state = s_ref[...]
    dec, prod = {}, {}
    for i in range(n_pairs + 2):
        if i < n_pairs:
            dec[i] = decays(i)
        if 0 <= i - 1 < n_pairs:
            state, prod[i - 1] = products(i - 1, dec.pop(i - 1), state)
        if 0 <= i - 2 < n_pairs:
            outputs(i - 2, *prod.pop(i - 2))
    s_ref[...] = state


def _gla_call(gq, gk, gv, gr, la, gla_norm, layer):
    bsz, seq, _ = gq.shape
    tl = GLA_TILE

    def tok(w):
        return pl.BlockSpec((None, tl, w), lambda b, i: (b, i, 0))

    return pl.pallas_call(
        functools.partial(_gla_kernel, n_pairs=tl // GLA_PAIR),
        grid=(bsz, seq // tl),
        in_specs=[tok(GLA_QK_W), tok(GLA_QK_W), tok(GLA_V_W), tok(GLA_V_W), tok(GLA_QK_W),
                  pl.BlockSpec((None, 1, GLA_V_W), lambda b, i: (layer, 0, 0))],
        out_specs=tok(GLA_V_W),
        out_shape=jax.ShapeDtypeStruct((bsz, seq, GLA_V_W), BF16),
        scratch_shapes=[pltpu.VMEM((GLA_QK_W, GLA_DV), F32)],
        compiler_params=_params("arbitrary", "arbitrary"),
        name="gla_mixer",
    )(gq, gk, gv, gr, la, gla_norm.reshape(DEPTH, 1, GLA_V_W))


def _t5_bucket_tiles():
    t = ATT_TILE
    r = np.arange(t)[:, None]
    c = np.arange(t)[None, :]
    max_exact = NUM_BUCKETS // 2
    tiles = []
    for off in range(3):
        dist = off * t + r - c
        d = np.maximum(dist, 1).astype(np.float32)
        large = max_exact + (np.log(d / np.float32(max_exact)) / np.float32(math.log(MAX_DISTANCE / max_exact))
                             * np.float32(NUM_BUCKETS - max_exact)).astype(np.int32)
        large = np.minimum(large, NUM_BUCKETS - 1)
        bucket = np.where(dist < max_exact, dist, large)
        tiles.append(np.where(dist >= 0, bucket, -1).T)
    far = tiles[2]
    assert (far == NUM_BUCKETS - 1).all()
    return np.stack(tiles).astype(np.int32)


def _bias_kernel(rel_ref, bucket_ref, o_ref):
    m = pl.program_id(0)
    bucket = bucket_ref[0:2]
    acc = jnp.full(bucket.shape, -jnp.inf, F32)
    for kk in range(NUM_BUCKETS):
        acc = jnp.where(bucket == kk, rel_ref[kk, m], acc)
    o_ref[0:2] = acc * LOG2E
    o_ref[2] = jnp.full(o_ref.shape[1:], rel_ref[NUM_BUCKETS - 1, m], F32) * LOG2E


def _bias_call(rel_bias):
    n_maps = 2 * DIFF_HEADS
    t = ATT_TILE
    return pl.pallas_call(
        _bias_kernel,
        grid=(n_maps,),
        in_specs=[pl.BlockSpec(memory_space=pltpu.SMEM),
                  pl.BlockSpec((3, t, t), lambda m: (0, 0, 0))],
        out_specs=pl.BlockSpec((None, 3, t, t), lambda m: (m, 0, 0, 0)),
        out_shape=jax.ShapeDtypeStruct((n_maps, 3, t, t), F32),
        compiler_params=_params("arbitrary"),
        name="t5_bias_tiles",
    )(rel_bias, jnp.asarray(_t5_bucket_tiles()))


def _diff_kernel(q_ref, k_ref, v_ref, bias_ref, lam_ref, g_ref, o_ref, *, lam_init, n_tiles):
    t = ATT_TILE
    lv = lam_ref[...]
    lam = (jnp.exp(jnp.sum(lv[0:1] * lv[1:2], axis=-1, keepdims=True))
           - jnp.exp(jnp.sum(lv[2:3] * lv[3:4], axis=-1, keepdims=True)) + lam_init)
    v_t = jnp.concatenate([v_ref[...].astype(F32).T, jnp.ones((ONES_ROWS, v_ref.shape[0]), F32)],
                          axis=0).astype(BF16)
    lane = lax.broadcasted_iota(jnp.int32, (t, 2 * DIFF_DQK), 1)
    far = [bias_ref[mp, 2, 0:1, 0:1] for mp in range(2)]

    tasks = [(qi, kj, mp) for qi in range(n_tiles) for kj in range(qi + 1) for mp in range(2)]
    masked_q = {}
    state = {}
    done = {}

    def logits(qi, kj, mp):
        if qi not in masked_q:
            q = q_ref[qi * t:(qi + 1) * t, :]
            zero = jnp.zeros_like(q)
            masked_q[qi] = (jnp.where(lane < DIFF_DQK, q, zero), jnp.where(lane < DIFF_DQK, zero, q))
        s = _dot_nt(k_ref[kj * t:(kj + 1) * t, :], masked_q[qi][mp])
        return s + bias_ref[mp, qi - kj] if qi - kj < 2 else s

    def accumulate(qi, kj, mp, s):
        shift = 0.0 if qi - kj < 2 else far[mp]
        m_blk = jnp.max(s, axis=0, keepdims=True) + shift
        prev = state.get((qi, mp))
        m_new = m_blk if prev is None else jnp.maximum(prev[0], m_blk)
        p = jnp.exp2((s - (m_new - shift)).astype(BF16))
        pv = _dot(v_t[:, kj * t:(kj + 1) * t], p)
        if prev is None:
            state[(qi, mp)] = (m_new, pv)
        else:
            state[(qi, mp)] = (m_new, jnp.exp2(prev[0] - m_new) * prev[1] + pv)
        if kj == qi:
            _, acc = state.pop((qi, mp))
            done.setdefault(qi, {})[mp] = acc[:DIFF_DV] / acc[DIFF_DV:DIFF_DV + 1]
            if len(done[qi]) == 2:
                o = done[qi][0] - lam * done[qi][1]
                o = o * lax.rsqrt(jnp.mean(o * o, axis=0, keepdims=True) + EPS)
                o_ref[qi * t:(qi + 1) * t, :] = (o.T * g_ref[...] * (1.0 - lam_init)).astype(BF16)
                del done[qi]

    pending = []
    for task in tasks:
        pending.append((task, logits(*task)))
        if len(pending) > ATT_LOOKAHEAD:
            ready, s = pending.pop(0)
            accumulate(*ready, s)
    for ready, s in pending:
        accumulate(*ready, s)


def _diff_call(dq, dk, dv, bias, diff_lambda, diff_norm, layer):
    bsz, seq, _ = dq.shape
    t = ATT_TILE
    lam_init = 0.8 - 0.6 * math.exp(-0.3 * layer)
    return pl.pallas_call(
        functools.partial(_diff_kernel, lam_init=lam_init, n_tiles=seq // t),
        grid=(bsz, DIFF_HEADS),
        in_specs=[
            pl.BlockSpec((None, seq, 2 * DIFF_DQK), lambda b, h: (b, 0, h)),
            pl.BlockSpec((None, seq, 2 * DIFF_DQK), lambda b, h: (b, 0, h)),
            pl.BlockSpec((None, seq, DIFF_DV), lambda b, h: (b, 0, h)),
            pl.BlockSpec((2, 3, t, t), lambda b, h: (h, 0, 0, 0)),
            pl.BlockSpec((None, 4, DIFF_DQK), lambda b, h: (layer, 0, 0)),
            pl.BlockSpec((None, 1, DIFF_DV), lambda b, h: (layer, 0, h)),
        ],
        out_specs=pl.BlockSpec((None, seq, DIFF_DV), lambda b, h: (b, 0, h)),
        out_shape=jax.ShapeDtypeStruct((bsz, seq, DIFF_V_W), BF16),
        compiler_params=_params("arbitrary", "arbitrary"),
        name="diff_attention",
    )(dq, dk, dv, bias, diff_lambda, diff_norm.reshape(DEPTH, 1, DIFF_V_W))


def _s5_prep_kernel(lre_ref, lim_ref, lstep_ref, bre_ref, bim_ref, cre_ref, cim_ref,
                    are_ref, aim_ref, bmat_ref, cmat_ref):
    lr = lre_ref[...]
    li = lim_ref[...]
    dt = jnp.exp(lstep_ref[...])
    mag = jnp.exp(lr * dt)
    a_re = mag * jnp.cos(li * dt)
    a_im = mag * jnp.sin(li * dt)
    are_ref[...] = jnp.broadcast_to(a_re, are_ref.shape)
    aim_ref[...] = jnp.broadcast_to(a_im, aim_ref.shape)
    nr, ni = a_re - 1.0, a_im
    den = lr * lr + li * li
    f_re = (nr * lr + ni * li) / den
    f_im = (ni * lr - nr * li) / den
    ns = S5_BLOCK_STATE
    for gb in range(S5_GROUP_BLOCKS):
        fr = f_re[:, gb * ns:(gb + 1) * ns]
        fi = f_im[:, gb * ns:(gb + 1) * ns]
        br = bre_ref[gb]
        bi = bim_ref[gb]
        bmat_ref[gb, :, :ns] = (fr * br - fi * bi).astype(BF16)
        bmat_ref[gb, :, ns:] = (fr * bi + fi * br).astype(BF16)
        cmat_ref[gb, :ns, :] = cre_ref[gb].astype(BF16)
        cmat_ref[gb, ns:, :] = (-cim_ref[gb]).astype(BF16)


def _s5_prep_call(lam_re, lam_im, log_step, b_re, b_im, c_re, c_im, bsz):
    d = DEPTH
    nb, ch, ns = S5_GROUP_BLOCKS, S5_BLOCK_CH, S5_BLOCK_STATE
    gpb = S5_GROUPS // nb
    eye = jnp.eye(gpb, dtype=F32)

    def in_blocks(b):
        t = b.reshape(d, nb, gpb, S5_STATE, S5_GROUP_CH).transpose(0, 1, 2, 4, 3)
        return (t[:, :, :, :, None, :] * eye[None, None, :, None, :, None]).reshape(d, nb, ch, ns)

    def out_blocks(c):
        t = c.reshape(d, nb, gpb, S5_GROUP_CH, S5_STATE).transpose(0, 1, 4, 2, 3)
        return (t[:, :, None, :, :, :] * eye[None, None, :, None, :, None]).reshape(d, nb, ns, ch)

    def row(x):
        return x.reshape(d, 1, S5_NSTATE)

    lstep = jnp.broadcast_to(log_step[:, :, None], (d, S5_GROUPS, S5_STATE))

    def full(shape):
        nd = len(shape)
        return pl.BlockSpec((None,) + shape, lambda l: (l,) + (0,) * nd)

    return pl.pallas_call(
        _s5_prep_kernel,
        grid=(d,),
        in_specs=[full((1, S5_NSTATE))] * 3 + [full((nb, ch, ns))] * 2 + [full((nb, ns, ch))] * 2,
        out_specs=[full((bsz, S5_NSTATE)), full((bsz, S5_NSTATE)),
                   full((nb, ch, 2 * ns)), full((nb, 2 * ns, ch))],
        out_shape=[jax.ShapeDtypeStruct((d, bsz, S5_NSTATE), F32), jax.ShapeDtypeStruct((d, bsz, S5_NSTATE), F32),
                   jax.ShapeDtypeStruct((d, nb, ch, 2 * ns), BF16), jax.ShapeDtypeStruct((d, nb, 2 * ns, ch), BF16)],
        compiler_params=_params("arbitrary"),
        name="s5_discretize",
    )(row(lam_re), row(lam_im), row(lstep), in_blocks(b_re), in_blocks(b_im), out_blocks(c_re), out_blocks(c_im))


def _s5_kernel(u_ref, are_ref, aim_ref, bmat_ref, cmat_ref, d_ref, wglu_ref, bglu_ref, o_ref,
               tb_ref, xr_ref, xi_ref, sr_ref, si_ref, *, bsz, n_steps):
    @pl.when(pl.program_id(0) == 0)
    def _():
        sr_ref[...] = jnp.zeros_like(sr_ref)
        si_ref[...] = jnp.zeros_like(si_ref)

    ns, ch, nb = S5_BLOCK_STATE, S5_BLOCK_CH, S5_GROUP_BLOCKS
    for b in range(bsz):
        ub = u_ref[b].astype(F32)
        for gb in range(nb):
            tb_ref[gb, pl.ds(b, n_steps, stride=bsz), :] = ub[:, gb * ch:(gb + 1) * ch]
    u = [tb_ref[gb] for gb in range(nb)]

    def drive(gb):
        return _dot(u[gb].astype(BF16), bmat_ref[gb])

    def scan(gb, bu):
        cols = slice(gb * ns, (gb + 1) * ns)
        ar, ai = are_ref[:, cols], aim_ref[:, cols]
        pr, pi = sr_ref[:, cols], si_ref[:, cols]
        for t in range(n_steps):
            rows = slice(t * bsz, (t + 1) * bsz)
            pr, pi = ar * pr - ai * pi + bu[rows, :ns], ar * pi + ai * pr + bu[rows, ns:]
            xr_ref[rows, cols] = pr
            xi_ref[rows, cols] = pi
        sr_ref[:, cols] = pr
        si_ref[:, cols] = pi

    def readout(gb):
        cols = slice(gb * ns, (gb + 1) * ns)
        return (_dot(xr_ref[:, cols].astype(BF16), cmat_ref[gb, :ns, :])
                + _dot(xi_ref[:, cols].astype(BF16), cmat_ref[gb, ns:, :]))

    bu, ys = {0: drive(0)}, []
    for gb in range(nb):
        if gb + 1 < nb:
            bu[gb + 1] = drive(gb + 1)
        scan(gb, bu.pop(gb))
        if gb >= 1:
            ys.append(readout(gb - 1))
    ys.append(readout(nb - 1))

    y = jax.nn.gelu(jnp.concatenate(ys, axis=-1) + d_ref[...] * jnp.concatenate(u, axis=-1))
    z = _dot(y.astype(BF16), wglu_ref[...]) + bglu_ref[...]
    out = z[:, :S5_WIDTH] * jax.nn.sigmoid(z[:, S5_WIDTH:])
    for gb in range(nb):
        tb_ref[gb] = out[:, gb * ch:(gb + 1) * ch]
    for b in range(bsz):
        for gb in range(nb):
            o_ref[b, :, gb * ch:(gb + 1) * ch] = tb_ref[gb, pl.ds(b, n_steps, stride=bsz), :].astype(BF16)


def _s5_call(su, a_re, a_im, bmat, cmat, d_skip, wglu, bglu, layer):
    bsz, seq, _ = su.shape
    tl = S5_TIME_TILE
    tr = tl * bsz
    nb, ch, ns = S5_GROUP_BLOCKS, S5_BLOCK_CH, S5_BLOCK_STATE

    def res(shape):
        nd = len(shape)
        return _resident((None,) + shape, lambda i: (layer,) + (0,) * nd)

    tok = pl.BlockSpec((bsz, tl, S5_WIDTH), lambda i: (0, i, 0))
    return pl.pallas_call(
        functools.partial(_s5_kernel, bsz=bsz, n_steps=tl),
        grid=(seq // tl,),
        in_specs=[tok,
                  res((bsz, S5_NSTATE)), res((bsz, S5_NSTATE)),
                  res((nb, ch, 2 * ns)), res((nb, 2 * ns, ch)),
                  res((1, S5_WIDTH)), res((S5_WIDTH, 2 * S5_WIDTH)), res((1, 2 * S5_WIDTH))],
        out_specs=tok,
        out_shape=jax.ShapeDtypeStruct((bsz, seq, S5_WIDTH), BF16),
        scratch_shapes=[pltpu.VMEM((nb, tr, ch), F32),
                        pltpu.VMEM((tr, S5_NSTATE), F32), pltpu.VMEM((tr, S5_NSTATE), F32),
                        pltpu.VMEM((bsz, S5_NSTATE), F32), pltpu.VMEM((bsz, S5_NSTATE), F32)],
        compiler_params=_params("arbitrary"),
        name="s5_mixer",
    )(su, a_re, a_im, bmat, cmat, d_skip, wglu, bglu)


def _merge_kernel(h_ref, ada_ref, npre_ref, npost_ref, yg_ref, yd_ref, ys_ref, wgate_ref, wbr_ref, wout_ref, o_ref):
    blocks = [slice(r * ROW_BLOCK, (r + 1) * ROW_BLOCK) for r in range(h_ref.shape[0] // ROW_BLOCK)]
    us = [_prenorm_mod(h_ref[rows, :], npre_ref[1:2, :], ada_ref[3:4, :], ada_ref[4:5, :]).astype(BF16)
          for rows in blocks]
    merged = [None] * len(blocks)
    for i, y_ref in enumerate((yg_ref, yd_ref, ys_ref)):
        for r, rows in enumerate(blocks):
            gate = jax.nn.sigmoid(_dot(us[r], wgate_ref[:, i * D_MODEL:(i + 1) * D_MODEL]))
            term = gate * _dot(y_ref[rows, :], wbr_ref[i])
            merged[r] = term if merged[r] is None else merged[r] + term
    ys = [_dot(m.astype(BF16), wout_ref[...]) for m in merged]
    for rows, y in zip(blocks, ys):
        o_ref[rows, :] = h_ref[rows, :] + ada_ref[5:6, :] * (_rms(y) * npost_ref[1:2, :])


def _merge_call(h, ada4, norm_pre, norm_post, y_gla, y_diff, y_s5, wgate, wbr, wout, layer):
    bsz, seq, _ = h.shape
    tm = TOKEN_TILE

    def tok(w):
        return pl.BlockSpec((None, tm, w), lambda b, i: (b, i, 0))

    return pl.pallas_call(
        _merge_kernel,
        grid=(bsz, seq // tm),
        in_specs=[
            tok(D_MODEL),
            pl.BlockSpec((None, None, ADA_CHUNKS, D_MODEL), lambda b, i: (layer, b, 0, 0)),
            pl.BlockSpec((None, 3, D_MODEL), lambda b, i: (layer, 0, 0)),
            pl.BlockSpec((None, 3, D_MODEL), lambda b, i: (layer, 0, 0)),
            tok(GLA_V_W), tok(DIFF_V_W), tok(S5_WIDTH),
            _resident((None, D_MODEL, N_BRANCH * D_MODEL), lambda b, i: (layer, 0, 0)),
            _resident((None, N_BRANCH, S5_WIDTH, D_MODEL), lambda b, i: (layer, 0, 0, 0)),
            _resident((None, D_MODEL, D_MODEL), lambda b, i: (layer, 0, 0)),
        ],
        out_specs=tok(D_MODEL),
        out_shape=jax.ShapeDtypeStruct(h.shape, F32),
        compiler_params=_params("arbitrary", "arbitrary"),
        name="mixer_merge",
    )(h, ada4, norm_pre, norm_post, y_gla, y_diff, y_s5, wgate, wbr, wout)


def kernel(x, c, rel_bias, w_ada, b_ada, norm_pre, norm_post, ffn1_w_up, ffn1_w_down, ffn2_w_up, ffn2_w_down,
           w_in, gla_w_alpha, gla_b_alpha, gla_norm, diff_lambda, diff_norm, s5_lam_re, s5_lam_im, s5_log_step,
           s5_b_re, s5_b_im, s5_c_re, s5_c_im, s5_d, s5_w_glu, s5_b_glu, w_branch, w_out):
    bsz, seq, _ = x.shape
    assert bsz == SUBLANES, "the S5 scan keeps the batch on the sublane axis"

    f1 = (ffn1_w_up, ffn1_w_down)
    f2 = (ffn2_w_up, ffn2_w_down)
    o_alpha = 2 * GLA_QK_W + 2 * GLA_V_W
    o_diff = o_alpha + GLA_LOWRANK
    o_s5 = o_diff + 2 * DIFF_QK_W + DIFF_V_W
    o_gate = o_s5 + S5_WIDTH
    wgla = w_in[:, :, :o_alpha].astype(BF16)
    wa = jnp.pad(w_in[:, :, o_alpha:o_diff].astype(BF16), ((0, 0), (0, 0), (0, LANES - GLA_LOWRANK)))
    wdiff = w_in[:, :, o_diff:o_s5].astype(BF16)
    ws5 = w_in[:, :, o_s5:o_gate].astype(BF16)
    wgate = w_in[:, :, o_gate:].astype(BF16)
    walpha = jnp.pad(gla_w_alpha.astype(BF16), ((0, 0), (0, LANES - GLA_LOWRANK), (0, 0)))
    balpha = gla_b_alpha.reshape(DEPTH, 1, GLA_QK_W)
    wbr = w_branch.astype(BF16)
    wout = w_out.astype(BF16)
    wglu = s5_w_glu.astype(BF16)
    bglu = s5_b_glu.reshape(DEPTH, 1, 2 * S5_WIDTH)
    d_skip = s5_d.reshape(DEPTH, 1, S5_WIDTH)

    ada4 = _ada_call(c, w_ada, b_ada).reshape(DEPTH, bsz, ADA_CHUNKS, D_MODEL)
    bias = _bias_call(rel_bias)
    a_re, a_im, bmat, cmat = _s5_prep_call(s5_lam_re, s5_lam_im, s5_log_step, s5_b_re, s5_b_im,
                                           s5_c_re, s5_c_im, bsz)

    h = x
    for layer in range(DEPTH):
        h = _ffn_call(h, ada4, norm_pre, norm_post, *f1, layer, 0)
        gq, gk, gv, gr, la, dq, dk, dv, su = _inproj_call(h, ada4, norm_pre, wgla, wa, walpha, balpha,
                                                          wdiff, ws5, layer)
        y_gla = _gla_call(gq, gk, gv, gr, la, gla_norm, layer)
        y_diff = _diff_call(dq, dk, dv, bias, diff_lambda, diff_norm, layer)
        y_s5 = _s5_call(su, a_re, a_im, bmat, cmat, d_skip, wglu, bglu, layer)
        h = _merge_call(h, ada4, norm_pre, norm_post, y_gla, y_diff, y_s5, wgate, wbr, wout, layer)
        h = _ffn_call(h, ada4, norm_pre, norm_post, *f2, layer, 2)
    return h
```

```python
import functools
import math

import numpy as np
import jax
import jax.numpy as jnp
from jax import lax
from jax.experimental import pallas as pl
from jax.experimental.pallas import tpu as pltpu

F32 = jnp.float32
BF16 = jnp.bfloat16

D_MODEL = 1024
DEPTH = 2
D_FF = 2816
ADA_CHUNKS = 9
EPS = 1e-6
LOG2E = math.log2(math.e)

GLA_HEADS = 4
GLA_DK = 64
GLA_DV = 128
GLA_LOWRANK = 16
GLA_TAU = 16.0
GLA_CHUNK = 64
GLA_QK_W = GLA_HEADS * GLA_DK
GLA_V_W = GLA_HEADS * GLA_DV

DIFF_HEADS = 4
DIFF_DQK = 64
DIFF_DV = 128
DIFF_QK_W = DIFF_HEADS * 2 * DIFF_DQK
DIFF_V_W = DIFF_HEADS * DIFF_DV
NUM_BUCKETS = 32
MAX_DISTANCE = 128

S5_GROUPS = 32
S5_GROUP_CH = 16
S5_STATE = 64
S5_WIDTH = S5_GROUPS * S5_GROUP_CH
S5_NSTATE = S5_GROUPS * S5_STATE
S5_GROUP_BLOCKS = 4
S5_BLOCK_CH = S5_WIDTH // S5_GROUP_BLOCKS
S5_BLOCK_STATE = S5_NSTATE // S5_GROUP_BLOCKS

N_BRANCH = 3

SUBLANES = 8
LANES = 128
ONES_ROWS = 2 * SUBLANES
VMEM_LIMIT_BYTES = 56 * 1024 * 1024

TOKEN_TILE = 1024
ROW_BLOCK = 512
FF_CHUNK = 256
FF_NCHUNK = D_FF // FF_CHUNK
WEIGHT_STAGE_COLS = 512
WEIGHT_STAGE_SLOTS = 4
MIX_GATE_ROW = 2 * GLA_QK_W + 2 * GLA_V_W + GLA_LOWRANK + 2 * DIFF_QK_W + DIFF_V_W + S5_WIDTH
MIX_STAGE_ROWS = 240
FFN_LOOKAHEAD = 1
GLA_PAIR = 2 * GLA_CHUNK
GLA_TILE = 1024
ATT_TILE = 256
ATT_LOOKAHEAD = 4
S5_TIME_TILE = 64
ADA_COL_TILE = 1536


def _dot(a, b):
    return jnp.dot(a, b, preferred_element_type=F32)


def _dot_nt(a, b):
    return lax.dot_general(a, b, (((1,), (1,)), ((), ())), preferred_element_type=F32)


def _rms(x):
    return x * lax.rsqrt(jnp.mean(x * x, axis=-1, keepdims=True) + EPS)


def _prenorm_mod(h, g, shift, scale):
    return (_rms(h) * g) * (1.0 + scale) + shift


def _params(*sem):
    return pltpu.CompilerParams(dimension_semantics=sem, vmem_limit_bytes=VMEM_LIMIT_BYTES)


def _resident(shape, index_map):
    return pl.BlockSpec(shape, index_map, pipeline_mode=pl.Buffered(1))


def _ada_kernel(c_ref, w_ref, b_ref, o_ref):
    cond = jax.nn.silu(c_ref[...]).astype(BF16)
    o_ref[...] = _dot(cond, w_ref[...].astype(BF16)) + b_ref[...]


def _ada_call(c, w_ada, b_ada):
    bsz = c.shape[0]
    n = ADA_CHUNKS * D_MODEL
    return pl.pallas_call(
        _ada_kernel,
        grid=(DEPTH, n // ADA_COL_TILE),
        in_specs=[
            pl.BlockSpec((bsz, D_MODEL), lambda l, j: (0, 0)),
            pl.BlockSpec((None, D_MODEL, ADA_COL_TILE), lambda l, j: (l, 0, j)),
            pl.BlockSpec((None, 1, ADA_COL_TILE), lambda l, j: (l, 0, j)),
        ],
        out_specs=pl.BlockSpec((None, bsz, ADA_COL_TILE), lambda l, j: (l, 0, j)),
        out_shape=jax.ShapeDtypeStruct((DEPTH, bsz, n), F32),
        compiler_params=_params("arbitrary", "arbitrary"),
        name="ada_proj",
    )(c, w_ada, b_ada.reshape(DEPTH, 1, n))


def _stream_cast(src, dst, stage, sem, *, axis, chunk):
    slots = stage.shape[0]

    def window(c):
        part = slice(c * chunk, (c + 1) * chunk)
        return (slice(None), part) if axis == 1 else (part, slice(None))

    def copy(c):
        return pltpu.make_async_copy(src.at[window(c)], stage.at[c % slots], sem.at[c % slots])

    n = src.shape[axis] // chunk
    for c in range(min(slots - 1, n)):
        copy(c).start()
    for c in range(n):
        if c + slots - 1 < n:
            copy(c + slots - 1).start()
        copy(c).wait()
        dst[window(c)] = stage[c % slots].astype(BF16)


def _ffn_kernel(h_ref, ada_ref, npre_ref, npost_ref, wup_hbm, wdn_hbm, o_ref,
                wup_ref, wdn_ref, up_stage, dn_stage, up_sem, dn_sem, *, sub, layer):
    a0 = 3 * sub

    @pl.when((pl.program_id(0) == 0) & (pl.program_id(1) == 0))
    def _():
        _stream_cast(wup_hbm.at[layer], wup_ref, up_stage, up_sem, axis=1, chunk=WEIGHT_STAGE_COLS)
        _stream_cast(wdn_hbm.at[layer], wdn_ref, dn_stage, dn_sem, axis=0, chunk=FF_CHUNK)

    def down(j, gate, up):
        mid = (jax.nn.silu(gate) * up).astype(BF16)
        return _dot(mid, wdn_ref[j * FF_CHUNK:(j + 1) * FF_CHUNK, :])

    for r in range(h_ref.shape[0] // ROW_BLOCK):
        rows = slice(r * ROW_BLOCK, (r + 1) * ROW_BLOCK)
        h = h_ref[rows, :]
        u = _prenorm_mod(h, npre_ref[sub:sub + 1, :], ada_ref[a0:a0 + 1, :], ada_ref[a0 + 1:a0 + 2, :])
        u = u.astype(BF16)

        def gate_up(j, u=u):
            lo = j * FF_CHUNK
            return _dot(u, wup_ref[:, lo:lo + FF_CHUNK]), _dot(u, wup_ref[:, D_FF + lo:D_FF + lo + FF_CHUNK])

        acc = None
        pending = []
        for j in range(FF_NCHUNK + FFN_LOOKAHEAD):
            if j < FF_NCHUNK:
                pending.append((j,) + gate_up(j))
            if j >= FFN_LOOKAHEAD:
                d = down(*pending.pop(0))
                acc = d if acc is None else acc + d
        post = _rms(acc) * npost_ref[sub:sub + 1, :]
        o_ref[rows, :] = h + (0.5 * ada_ref[a0 + 2:a0 + 3, :]) * post


def _ffn_call(h, ada4, norm_pre, norm_post, wup, wdn, layer, sub):
    bsz, seq, _ = h.shape
    tm = TOKEN_TILE
    tok = pl.BlockSpec((None, tm, D_MODEL), lambda b, i: (b, i, 0))
    return pl.pallas_call(
        functools.partial(_ffn_kernel, sub=sub, layer=layer),
        grid=(bsz, seq // tm),
        in_specs=[
            tok,
            pl.BlockSpec((None, None, ADA_CHUNKS, D_MODEL), lambda b, i: (layer, b, 0, 0)),
            pl.BlockSpec((None, 3, D_MODEL), lambda b, i: (layer, 0, 0)),
            pl.BlockSpec((None, 3, D_MODEL), lambda b, i: (layer, 0, 0)),
            pl.BlockSpec(memory_space=pl.ANY),
            pl.BlockSpec(memory_space=pl.ANY),
        ],
        out_specs=tok,
        out_shape=jax.ShapeDtypeStruct(h.shape, F32),
        scratch_shapes=[pltpu.VMEM((D_MODEL, 2 * D_FF), BF16), pltpu.VMEM((D_FF, D_MODEL), BF16),
                        pltpu.VMEM((WEIGHT_STAGE_SLOTS, D_MODEL, WEIGHT_STAGE_COLS), F32),
                        pltpu.VMEM((WEIGHT_STAGE_SLOTS, FF_CHUNK, D_MODEL), F32),
                        pltpu.SemaphoreType.DMA((WEIGHT_STAGE_SLOTS,)),
                        pltpu.SemaphoreType.DMA((WEIGHT_STAGE_SLOTS,))],
        compiler_params=_params("arbitrary", "arbitrary"),
        name=f"ffn{sub // 2 + 1}",
    )(h, ada4, norm_pre, norm_post, wup, wdn)


def _log_sigmoid(z):
    return jnp.minimum(z, 0.0) - jnp.log1p(jnp.exp(-jnp.abs(z)))


def _inproj_kernel(h_ref, ada_ref, npre_ref, wt_hbm, walpha_ref, balpha_ref,
                   gq_ref, gk_ref, gv_ref, gr_ref, la_ref, dq_ref, dk_ref, dv_ref, su_ref,
                   wt_ref, stage, sem, *, layer):
    @pl.when((pl.program_id(0) == 0) & (pl.program_id(1) == 0))
    def _():
        _stream_cast(wt_hbm.at[layer, 0:MIX_GATE_ROW], wt_ref, stage, sem, axis=0, chunk=MIX_STAGE_ROWS)

    o_alpha = 2 * GLA_QK_W + 2 * GLA_V_W
    o_diff = o_alpha + GLA_LOWRANK
    o_s5 = o_diff + 2 * DIFF_QK_W + DIFF_V_W
    for r in range(h_ref.shape[0] // ROW_BLOCK):
        rows = slice(r * ROW_BLOCK, (r + 1) * ROW_BLOCK)
        u = _prenorm_mod(h_ref[rows, :], npre_ref[1:2, :], ada_ref[3:4, :], ada_ref[4:5, :]).astype(BF16)
        a_low = _dot_nt(u, wt_ref[o_alpha:o_alpha + LANES, :]).astype(BF16)
        pg = _dot_nt(u, wt_ref[0:o_alpha, :])
        z = _dot(a_low, walpha_ref[...]) + balpha_ref[...]
        gq_ref[rows, :] = pg[:, :GLA_QK_W]
        gk_ref[rows, :] = pg[:, GLA_QK_W:2 * GLA_QK_W]
        gv_ref[rows, :] = pg[:, 2 * GLA_QK_W:2 * GLA_QK_W + GLA_V_W].astype(BF16)
        gr_ref[rows, :] = pg[:, 2 * GLA_QK_W + GLA_V_W:]
        la_ref[rows, :] = _log_sigmoid(z) / GLA_TAU
        pd = _dot_nt(u, wt_ref[o_diff:o_s5, :])
        dq_ref[rows, :] = (pd[:, :DIFF_QK_W] * (DIFF_DQK ** -0.5 * LOG2E)).astype(BF16)
        dk_ref[rows, :] = pd[:, DIFF_QK_W:2 * DIFF_QK_W].astype(BF16)
        dv_ref[rows, :] = pd[:, 2 * DIFF_QK_W:].astype(BF16)
        su_ref[rows, :] = _dot_nt(u, wt_ref[o_s5:MIX_GATE_ROW, :]).astype(BF16)


def _inproj_call(h, ada4, norm_pre, w_in_t, walpha, balpha, layer):
    bsz, seq, _ = h.shape
    tm = TOKEN_TILE

    def tok(w):
        return pl.BlockSpec((None, tm, w), lambda b, i: (b, i, 0))

    def sds(w, dt):
        return jax.ShapeDtypeStruct((bsz, seq, w), dt)

    def res(arr):
        nd = arr.ndim - 1
        return _resident((None,) + arr.shape[1:], lambda b, i: (layer,) + (0,) * nd)

    return pl.pallas_call(
        functools.partial(_inproj_kernel, layer=layer),
        grid=(bsz, seq // tm),
        in_specs=[
            tok(D_MODEL),
            pl.BlockSpec((None, None, ADA_CHUNKS, D_MODEL), lambda b, i: (layer, b, 0, 0)),
            pl.BlockSpec((None, 3, D_MODEL), lambda b, i: (layer, 0, 0)),
            pl.BlockSpec(memory_space=pl.ANY), res(walpha), res(balpha),
        ],
        out_specs=[tok(GLA_QK_W), tok(GLA_QK_W), tok(GLA_V_W), tok(GLA_V_W), tok(GLA_QK_W),
                   tok(DIFF_QK_W), tok(DIFF_QK_W), tok(DIFF_V_W), tok(S5_WIDTH)],
        out_shape=[sds(GLA_QK_W, F32), sds(GLA_QK_W, F32), sds(GLA_V_W, BF16), sds(GLA_V_W, F32),
                   sds(GLA_QK_W, F32), sds(DIFF_QK_W, BF16), sds(DIFF_QK_W, BF16), sds(DIFF_V_W, BF16),
                   sds(S5_WIDTH, BF16)],
        scratch_shapes=[pltpu.VMEM((MIX_GATE_ROW, D_MODEL), BF16),
                        pltpu.VMEM((WEIGHT_STAGE_SLOTS, MIX_STAGE_ROWS, D_MODEL), F32),
                        pltpu.SemaphoreType.DMA((WEIGHT_STAGE_SLOTS,))],
        compiler_params=_params("arbitrary", "arbitrary"),
        name="mixer_in_proj",
    )(h, ada4, norm_pre, w_in_t, walpha, balpha)


def _cumsum_rows(tri, x):
    hi = x.astype(BF16)
    r1 = x - hi.astype(F32)
    mid = r1.astype(BF16)
    lo = (r1 - mid.astype(F32)).astype(BF16)
    return _dot(tri, hi) + _dot(tri, mid) + _dot(tri, lo)


def _gla_kernel(q_ref, k_ref, v_ref, r_ref, la_ref, g_ref, o_ref, s_ref, *, n_pairs):
    @pl.when(pl.program_id(1) == 0)
    def _():
        s_ref[...] = jnp.zeros_like(s_ref)

    row = lax.broadcasted_iota(jnp.int32, (GLA_PAIR, GLA_PAIR), 0)
    col = lax.broadcasted_iota(jnp.int32, (GLA_PAIR, GLA_PAIR), 1)
    causal = (col <= row) & ((row // GLA_CHUNK) == (col // GLA_CHUNK))
    tri = jnp.where(causal, 1.0, 0.0).astype(BF16)
    qlane = lax.broadcasted_iota(jnp.int32, (GLA_PAIR, GLA_QK_W), 1)
    tlane = lax.broadcasted_iota(jnp.int32, (GLA_QK_W, GLA_PAIR), 1)
    first = tlane < GLA_CHUNK

    def decays(p):
        rows = slice(p * GLA_PAIR, (p + 1) * GLA_PAIR)
        b = _cumsum_rows(tri, la_ref[rows, :])
        b_t = b.T
        qd = (q_ref[rows, :] * (GLA_DK ** -0.5)) * jnp.exp(b)
        k_t = k_ref[rows, :].T
        bl0 = b_t[:, GLA_CHUNK - 1:GLA_CHUNK]
        bl1 = b_t[:, GLA_PAIR - 1:GLA_PAIR]
        kdec_t = k_t * jnp.exp(jnp.where(first, bl0, bl1) - b_t)
        return dict(
            qh=[jnp.where(qlane // GLA_DK == h, qd, 0.0).astype(BF16) for h in range(GLA_HEADS)],
            kd_t=(k_t * jnp.exp(-b_t)).astype(BF16),
            kdec0=jnp.where(first, kdec_t, 0.0).astype(BF16),
            kdec1=jnp.where(first, 0.0, kdec_t).astype(BF16),
            dec0=jnp.exp(bl0), dec1=jnp.exp(bl1))

    def products(p, d, s0):
        v = v_ref[p * GLA_PAIR:(p + 1) * GLA_PAIR, :]
        ds0, ds1 = [], []
        for h in range(GLA_HEADS):
            hr = slice(h * GLA_DK, (h + 1) * GLA_DK)
            vh = v[:, h * GLA_DV:(h + 1) * GLA_DV]
            ds0.append(_dot(d["kdec0"][hr, :], vh))
            ds1.append(_dot(d["kdec1"][hr, :], vh))
        att = [jnp.where(causal, _dot(qh, d["kd_t"]), 0.0).astype(BF16) for qh in d["qh"]]
        s1 = d["dec0"] * s0 + jnp.concatenate(ds0, axis=0)
        s2 = d["dec1"] * s1 + jnp.concatenate(ds1, axis=0)
        s0b = s0.astype(BF16)
        s1b = s1.astype(BF16)
        inter = [jnp.concatenate([_dot(qh[:GLA_CHUNK], s0b), _dot(qh[GLA_CHUNK:], s1b)], axis=0)
                 for qh in d["qh"]]
        return s2, (v, att, inter)

    def outputs(p, v, att, inter):
        rows = slice(p * GLA_PAIR, (p + 1) * GLA_PAIR)
        r = r_ref[rows, :]
        for h in range(GLA_HEADS):
            vc = slice(h * GLA_DV, (h + 1) * GLA_DV)
            o = _rms(_dot(att[h], v[:, vc]) + inter[h]) * g_ref[:, vc]
            o_ref[rows, vc] = (o * jax.nn.silu(r[:, vc])).astype(BF16)

    state = s_ref[...]
    dec, prod = {}, {}
    for i in range(n_pairs + 2):
        if i < n_pairs:
            dec[i] = decays(i)
        if 0 <= i - 1 < n_pairs:
            state, prod[i - 1] = products(i - 1, dec.pop(i - 1), state)
        if 0 <= i - 2 < n_pairs:
            outputs(i - 2, *prod.pop(i - 2))
    s_ref[...] = state


def _gla_call(gq, gk, gv, gr, la, gla_norm, layer):
    bsz, seq, _ = gq.shape
    tl = GLA_TILE

    def tok(w):
        return pl.BlockSpec((None, tl, w), lambda b, i: (b, i, 0))

    return pl.pallas_call(
        functools.partial(_gla_kernel, n_pairs=tl // GLA_PAIR),
        grid=(bsz, seq // tl),
        in_specs=[tok(GLA_QK_W), tok(GLA_QK_W), tok(GLA_V_W), tok(GLA_V_W), tok(GLA_QK_W),
                  pl.BlockSpec((None, 1, GLA_V_W), lambda b, i: (layer, 0, 0))],
        out_specs=tok(GLA_V_W),
        out_shape=jax.ShapeDtypeStruct((bsz, seq, GLA_V_W), BF16),
        scratch_shapes=[pltpu.VMEM((GLA_QK_W, GLA_DV), F32)],
        compiler_params=_params("arbitrary", "arbitrary"),
        name="gla_mixer",
    )(gq, gk, gv, gr, la, gla_norm.reshape(DEPTH, 1, GLA_V_W))


def _t5_bucket_tiles():
    t = ATT_TILE
    r = np.arange(t)[:, None]
    c = np.arange(t)[None, :]
    max_exact = NUM_BUCKETS // 2
    tiles = []
    for off in range(3):
        dist = off * t + r - c
        d = np.maximum(dist, 1).astype(np.float32)
        large = max_exact + (np.log(d / np.float32(max_exact)) / np.float32(math.log(MAX_DISTANCE / max_exact))
                             * np.float32(NUM_BUCKETS - max_exact)).astype(np.int32)
        large = np.minimum(large, NUM_BUCKETS - 1)
        bucket = np.where(dist < max_exact, dist, large)
        tiles.append(np.where(dist >= 0, bucket, -1).T)
    far = tiles[2]
    assert (far == NUM_BUCKETS - 1).all()
    return np.stack(tiles).astype(np.int32)


def _bias_kernel(rel_ref, bucket_ref, o_ref):
    m = pl.program_id(0)
    bucket = bucket_ref[0:2]
    acc = jnp.full(bucket.shape, -jnp.inf, F32)
    for kk in range(NUM_BUCKETS):
        acc = jnp.where(bucket == kk, rel_ref[kk, m], acc)
    o_ref[0:2] = acc * LOG2E
    o_ref[2] = jnp.full(o_ref.shape[1:], rel_ref[NUM_BUCKETS - 1, m], F32) * LOG2E


def _bias_call(rel_bias):
    n_maps = 2 * DIFF_HEADS
    t = ATT_TILE
    return pl.pallas_call(
        _bias_kernel,
        grid=(n_maps,),
        in_specs=[pl.BlockSpec(memory_space=pltpu.SMEM),
                  pl.BlockSpec((3, t, t), lambda m: (0, 0, 0))],
        out_specs=pl.BlockSpec((None, 3, t, t), lambda m: (m, 0, 0, 0)),
        out_shape=jax.ShapeDtypeStruct((n_maps, 3, t, t), F32),
        compiler_params=_params("arbitrary"),
        name="t5_bias_tiles",
    )(rel_bias, jnp.asarray(_t5_bucket_tiles()))


def _diff_kernel(q_ref, k_ref, v_ref, bias_ref, lam_ref, g_ref, o_ref, *, lam_init, n_tiles):
    t = ATT_TILE
    lv = lam_ref[...]
    lam = (jnp.exp(jnp.sum(lv[0:1] * lv[1:2], axis=-1, keepdims=True))
           - jnp.exp(jnp.sum(lv[2:3] * lv[3:4], axis=-1, keepdims=True)) + lam_init)
    v_t = jnp.concatenate([v_ref[...].astype(F32).T, jnp.ones((ONES_ROWS, v_ref.shape[0]), F32)],
                          axis=0).astype(BF16)
    lane = lax.broadcasted_iota(jnp.int32, (t, 2 * DIFF_DQK), 1)
    far = [bias_ref[mp, 2, 0:1, 0:1] for mp in range(2)]

    tasks = [(qi, kj, mp) for qi in range(n_tiles) for kj in range(qi + 1) for mp in range(2)]
    masked_q = {}
    state = {}
    done = {}

    def logits(qi, kj, mp):
        if qi not in masked_q:
            q = q_ref[qi * t:(qi + 1) * t, :]
            zero = jnp.zeros_like(q)
            masked_q[qi] = (jnp.where(lane < DIFF_DQK, q, zero), jnp.where(lane < DIFF_DQK, zero, q))
        s = _dot_nt(k_ref[kj * t:(kj + 1) * t, :], masked_q[qi][mp])
        return s + bias_ref[mp, qi - kj] if qi - kj < 2 else s

    def accumulate(qi, kj, mp, s):
        shift = 0.0 if qi - kj < 2 else far[mp]
        m_blk = jnp.max(s, axis=0, keepdims=True) + shift
        prev = state.get((qi, mp))
        m_new = m_blk if prev is None else jnp.maximum(prev[0], m_blk)
        p = jnp.exp2((s - (m_new - shift)).astype(BF16))
        pv = _dot(v_t[:, kj * t:(kj + 1) * t], p)
        if prev is None:
            state[(qi, mp)] = (m_new, pv)
        else:
            state[(qi, mp)] = (m_new, jnp.exp2(prev[0] - m_new) * prev[1] + pv)
        if kj == qi:
            _, acc = state.pop((qi, mp))
            done.setdefault(qi, {})[mp] = acc[:DIFF_DV] / acc[DIFF_DV:DIFF_DV + 1]
            if len(done[qi]) == 2:
                o = done[qi][0] - lam * done[qi][1]
                o = o * lax.rsqrt(jnp.mean(o * o, axis=0, keepdims=True) + EPS)
                o_ref[qi * t:(qi + 1) * t, :] = (o.T * g_ref[...] * (1.0 - lam_init)).astype(BF16)
                del done[qi]

    pending = []
    for task in tasks:
        pending.append((task, logits(*task)))
        if len(pending) > ATT_LOOKAHEAD:
            ready, s = pending.pop(0)
            accumulate(*ready, s)
    for ready, s in pending:
        accumulate(*ready, s)


def _diff_call(dq, dk, dv, bias, diff_lambda, diff_norm, layer):
    bsz, seq, _ = dq.shape
    t = ATT_TILE
    lam_init = 0.8 - 0.6 * math.exp(-0.3 * layer)
    return pl.pallas_call(
        functools.partial(_diff_kernel, lam_init=lam_init, n_tiles=seq // t),
        grid=(bsz, DIFF_HEADS),
        in_specs=[
            pl.BlockSpec((None, seq, 2 * DIFF_DQK), lambda b, h: (b, 0, h)),
            pl.BlockSpec((None, seq, 2 * DIFF_DQK), lambda b, h: (b, 0, h)),
            pl.BlockSpec((None, seq, DIFF_DV), lambda b, h: (b, 0, h)),
            pl.BlockSpec((2, 3, t, t), lambda b, h: (h, 0, 0, 0)),
            pl.BlockSpec((None, 4, DIFF_DQK), lambda b, h: (layer, 0, 0)),
            pl.BlockSpec((None, 1, DIFF_DV), lambda b, h: (layer, 0, h)),
        ],
        out_specs=pl.BlockSpec((None, seq, DIFF_DV), lambda b, h: (b, 0, h)),
        out_shape=jax.ShapeDtypeStruct((bsz, seq, DIFF_V_W), BF16),
        compiler_params=_params("arbitrary", "arbitrary"),
        name="diff_attention",
    )(dq, dk, dv, bias, diff_lambda, diff_norm.reshape(DEPTH, 1, DIFF_V_W))


def _s5_prep_kernel(lre_ref, lim_ref, lstep_ref, bre_ref, bim_ref, cre_ref, cim_ref,
                    are_ref, aim_ref, bmat_ref, cmat_ref):
    lr = lre_ref[...]
    li = lim_ref[...]
    dt = jnp.exp(lstep_ref[...])
    mag = jnp.exp(lr * dt)
    a_re = mag * jnp.cos(li * dt)
    a_im = mag * jnp.sin(li * dt)
    are_ref[...] = jnp.broadcast_to(a_re, are_ref.shape)
    aim_ref[...] = jnp.broadcast_to(a_im, aim_ref.shape)
    nr, ni = a_re - 1.0, a_im
    den = lr * lr + li * li
    f_re = (nr * lr + ni * li) / den
    f_im = (ni * lr - nr * li) / den
    ns = S5_BLOCK_STATE
    for gb in range(S5_GROUP_BLOCKS):
        fr = f_re[:, gb * ns:(gb + 1) * ns]
        fi = f_im[:, gb * ns:(gb + 1) * ns]
        br = bre_ref[gb]
        bi = bim_ref[gb]
        bmat_ref[gb, :, :ns] = (fr * br - fi * bi).astype(BF16)
        bmat_ref[gb, :, ns:] = (fr * bi + fi * br).astype(BF16)
        cmat_ref[gb, :ns, :] = cre_ref[gb].astype(BF16)
        cmat_ref[gb, ns:, :] = (-cim_ref[gb]).astype(BF16)


def _s5_prep_call(lam_re, lam_im, log_step, b_re, b_im, c_re, c_im, bsz):
    d = DEPTH
    nb, ch, ns = S5_GROUP_BLOCKS, S5_BLOCK_CH, S5_BLOCK_STATE
    gpb = S5_GROUPS // nb
    eye = jnp.eye(gpb, dtype=F32)

    def in_blocks(b):
        t = b.reshape(d, nb, gpb, S5_STATE, S5_GROUP_CH).transpose(0, 1, 2, 4, 3)
        return (t[:, :, :, :, None, :] * eye[None, None, :, None, :, None]).reshape(d, nb, ch, ns)

    def out_blocks(c):
        t = c.reshape(d, nb, gpb, S5_GROUP_CH, S5_STATE).transpose(0, 1, 4, 2, 3)
        return (t[:, :, None, :, :, :] * eye[None, None, :, None, :, None]).reshape(d, nb, ns, ch)

    def row(x):
        return x.reshape(d, 1, S5_NSTATE)

    lstep = jnp.broadcast_to(log_step[:, :, None], (d, S5_GROUPS, S5_STATE))

    def full(shape):
        nd = len(shape)
        return pl.BlockSpec((None,) + shape, lambda l: (l,) + (0,) * nd)

    return pl.pallas_call(
        _s5_prep_kernel,
        grid=(d,),
        in_specs=[full((1, S5_NSTATE))] * 3 + [full((nb, ch, ns))] * 2 + [full((nb, ns, ch))] * 2,
        out_specs=[full((bsz, S5_NSTATE)), full((bsz, S5_NSTATE)),
                   full((nb, ch, 2 * ns)), full((nb, 2 * ns, ch))],
        out_shape=[jax.ShapeDtypeStruct((d, bsz, S5_NSTATE), F32), jax.ShapeDtypeStruct((d, bsz, S5_NSTATE), F32),
                   jax.ShapeDtypeStruct((d, nb, ch, 2 * ns), BF16), jax.ShapeDtypeStruct((d, nb, 2 * ns, ch), BF16)],
        compiler_params=_params("arbitrary"),
        name="s5_discretize",
    )(row(lam_re), row(lam_im), row(lstep), in_blocks(b_re), in_blocks(b_im), out_blocks(c_re), out_blocks(c_im))


def _s5_kernel(u_ref, are_ref, aim_ref, bmat_ref, cmat_ref, d_ref, wglu_ref, bglu_ref, o_ref,
               tb_ref, xr_ref, xi_ref, sr_ref, si_ref, *, bsz, n_steps):
    @pl.when(pl.program_id(0) == 0)
    def _():
        sr_ref[...] = jnp.zeros_like(sr_ref)
        si_ref[...] = jnp.zeros_like(si_ref)

    ns, ch, nb = S5_BLOCK_STATE, S5_BLOCK_CH, S5_GROUP_BLOCKS
    for b in range(bsz):
        ub = u_ref[b].astype(F32)
        for gb in range(nb):
            tb_ref[gb, pl.ds(b, n_steps, stride=bsz), :] = ub[:, gb * ch:(gb + 1) * ch]
    u = [tb_ref[gb] for gb in range(nb)]

    def drive(gb):
        return _dot(u[gb].astype(BF16), bmat_ref[gb])

    def scan(gb, bu):
        cols = slice(gb * ns, (gb + 1) * ns)
        ar, ai = are_ref[:, cols], aim_ref[:, cols]
        pr, pi = sr_ref[:, cols], si_ref[:, cols]
        for t in range(n_steps):
            rows = slice(t * bsz, (t + 1) * bsz)
            pr, pi = ar * pr - ai * pi + bu[rows, :ns], ar * pi + ai * pr + bu[rows, ns:]
            xr_ref[rows, cols] = pr
            xi_ref[rows, cols] = pi
        sr_ref[:, cols] = pr
        si_ref[:, cols] = pi

    def readout(gb):
        cols = slice(gb * ns, (gb + 1) * ns)
        return (_dot(xr_ref[:, cols].astype(BF16), cmat_ref[gb, :ns, :])
                + _dot(xi_ref[:, cols].astype(BF16), cmat_ref[gb, ns:, :]))

    bu, ys = {0: drive(0)}, []
    for gb in range(nb):
        if gb + 1 < nb:
            bu[gb + 1] = drive(gb + 1)
        scan(gb, bu.pop(gb))
        if gb >= 1:
            ys.append(readout(gb - 1))
    ys.append(readout(nb - 1))

    y = jax.nn.gelu(jnp.concatenate(ys, axis=-1) + d_ref[...] * jnp.concatenate(u, axis=-1))
    z = _dot(y.astype(BF16), wglu_ref[...]) + bglu_ref[...]
    out = z[:, :S5_WIDTH] * jax.nn.sigmoid(z[:, S5_WIDTH:])
    for gb in range(nb):
        tb_ref[gb] = out[:, gb * ch:(gb + 1) * ch]
    for b in range(bsz):
        for gb in range(nb):
            o_ref[b, :, gb * ch:(gb + 1) * ch] = tb_ref[gb, pl.ds(b, n_steps, stride=bsz), :].astype(BF16)


def _s5_call(su, a_re, a_im, bmat, cmat, d_skip, wglu, bglu, layer):
    bsz, seq, _ = su.shape
    tl = S5_TIME_TILE
    tr = tl * bsz
    nb, ch, ns = S5_GROUP_BLOCKS, S5_BLOCK_CH, S5_BLOCK_STATE

    def res(shape):
        nd = len(shape)
        return _resident((None,) + shape, lambda i: (layer,) + (0,) * nd)

    tok = pl.BlockSpec((bsz, tl, S5_WIDTH), lambda i: (0, i, 0))
    return pl.pallas_call(
        functools.partial(_s5_kernel, bsz=bsz, n_steps=tl),
        grid=(seq // tl,),
        in_specs=[tok,
                  res((bsz, S5_NSTATE)), res((bsz, S5_NSTATE)),
                  res((nb, ch, 2 * ns)), res((nb, 2 * ns, ch)),
                  res((1, S5_WIDTH)), res((S5_WIDTH, 2 * S5_WIDTH)), res((1, 2 * S5_WIDTH))],
        out_specs=tok,
        out_shape=jax.ShapeDtypeStruct((bsz, seq, S5_WIDTH), BF16),
        scratch_shapes=[pltpu.VMEM((nb, tr, ch), F32),
                        pltpu.VMEM((tr, S5_NSTATE), F32), pltpu.VMEM((tr, S5_NSTATE), F32),
                        pltpu.VMEM((bsz, S5_NSTATE), F32), pltpu.VMEM((bsz, S5_NSTATE), F32)],
        compiler_params=_params("arbitrary"),
        name="s5_mixer",
    )(su, a_re, a_im, bmat, cmat, d_skip, wglu, bglu)


def _merge_kernel(h_ref, ada_ref, npre_ref, npost_ref, yg_ref, yd_ref, ys_ref, wt_hbm, wbr_ref, wout_ref, o_ref,
                  wgate_ref, stage, sem, *, layer):
    @pl.when((pl.program_id(0) == 0) & (pl.program_id(1) == 0))
    def _():
        _stream_cast(wt_hbm.at[layer, MIX_GATE_ROW:MIX_GATE_ROW + N_BRANCH * D_MODEL], wgate_ref, stage, sem,
                     axis=0, chunk=FF_CHUNK)

    blocks = [slice(r * ROW_BLOCK, (r + 1) * ROW_BLOCK) for r in range(h_ref.shape[0] // ROW_BLOCK)]
    us = [_prenorm_mod(h_ref[rows, :], npre_ref[1:2, :], ada_ref[3:4, :], ada_ref[4:5, :]).astype(BF16)
          for rows in blocks]
    merged = [None] * len(blocks)
    for i, y_ref in enumerate((yg_ref, yd_ref, ys_ref)):
        for r, rows in enumerate(blocks):
            gate = jax.nn.sigmoid(_dot_nt(us[r], wgate_ref[i * D_MODEL:(i + 1) * D_MODEL, :]))
            term = gate * _dot(y_ref[rows, :], wbr_ref[i])
            merged[r] = term if merged[r] is None else merged[r] + term
    ys = [_dot(m.astype(BF16), wout_ref[...]) for m in merged]
    for rows, y in zip(blocks, ys):
        o_ref[rows, :] = h_ref[rows, :] + ada_ref[5:6, :] * (_rms(y) * npost_ref[1:2, :])


def _merge_call(h, ada4, norm_pre, norm_post, y_gla, y_diff, y_s5, w_in_t, wbr, wout, layer):
    bsz, seq, _ = h.shape
    tm = TOKEN_TILE

    def tok(w):
        return pl.BlockSpec((None, tm, w), lambda b, i: (b, i, 0))

    return pl.pallas_call(
        functools.partial(_merge_kernel, layer=layer),
        grid=(bsz, seq // tm),
        in_specs=[
            tok(D_MODEL),
            pl.BlockSpec((None, None, ADA_CHUNKS, D_MODEL), lambda b, i: (layer, b, 0, 0)),
            pl.BlockSpec((None, 3, D_MODEL), lambda b, i: (layer, 0, 0)),
            pl.BlockSpec((None, 3, D_MODEL), lambda b, i: (layer, 0, 0)),
            tok(GLA_V_W), tok(DIFF_V_W), tok(S5_WIDTH),
            pl.BlockSpec(memory_space=pl.ANY),
            _resident((None, N_BRANCH, S5_WIDTH, D_MODEL), lambda b, i: (layer, 0, 0, 0)),
            _resident((None, D_MODEL, D_MODEL), lambda b, i: (layer, 0, 0)),
        ],
        out_specs=tok(D_MODEL),
        out_shape=jax.ShapeDtypeStruct(h.shape, F32),
        scratch_shapes=[pltpu.VMEM((N_BRANCH * D_MODEL, D_MODEL), BF16),
                        pltpu.VMEM((WEIGHT_STAGE_SLOTS, FF_CHUNK, D_MODEL), F32),
                        pltpu.SemaphoreType.DMA((WEIGHT_STAGE_SLOTS,))],
        compiler_params=_params("arbitrary", "arbitrary"),
        name="mixer_merge",
    )(h, ada4, norm_pre, norm_post, y_gla, y_diff, y_s5, w_in_t, wbr, wout)


def kernel(x, c, rel_bias, w_ada, b_ada, norm_pre, norm_post, ffn1_w_up, ffn1_w_down, ffn2_w_up, ffn2_w_down,
           w_in, gla_w_alpha, gla_b_alpha, gla_norm, diff_lambda, diff_norm, s5_lam_re, s5_lam_im, s5_log_step,
           s5_b_re, s5_b_im, s5_c_re, s5_c_im, s5_d, s5_w_glu, s5_b_glu, w_branch, w_out):
    bsz, seq, _ = x.shape
    assert bsz == SUBLANES, "the S5 scan keeps the batch on the sublane axis"

    f1 = (ffn1_w_up, ffn1_w_down)
    f2 = (ffn2_w_up, ffn2_w_down)
    w_in_t = jnp.swapaxes(w_in, 1, 2)
    walpha = jnp.pad(gla_w_alpha.astype(BF16), ((0, 0), (0, LANES - GLA_LOWRANK), (0, 0)))
    balpha = gla_b_alpha.reshape(DEPTH, 1, GLA_QK_W)
    wbr = w_branch.astype(BF16)
    wout = w_out.astype(BF16)
    wglu = s5_w_glu.astype(BF16)
    bglu = s5_b_glu.reshape(DEPTH, 1, 2 * S5_WIDTH)
    d_skip = s5_d.reshape(DEPTH, 1, S5_WIDTH)

    ada4 = _ada_call(c, w_ada, b_ada).reshape(DEPTH, bsz, ADA_CHUNKS, D_MODEL)
    bias = _bias_call(rel_bias)
    a_re, a_im, bmat, cmat = _s5_prep_call(s5_lam_re, s5_lam_im, s5_log_step, s5_b_re, s5_b_im,
                                           s5_c_re, s5_c_im, bsz)

    h = x
    for layer in range(DEPTH):
        h = _ffn_call(h, ada4, norm_pre, norm_post, *f1, layer, 0)
        gq, gk, gv, gr, la, dq, dk, dv, su = _inproj_call(h, ada4, norm_pre, w_in_t, walpha, balpha, layer)
        y_gla = _gla_call(gq, gk, gv, gr, la, gla_norm, layer)
        y_diff = _diff_call(dq, dk, dv, bias, diff_lambda, diff_norm, layer)
        y_s5 = _s5_call(su, a_re, a_im, bmat, cmat, d_skip, wglu, bglu, layer)
        h = _merge_call(h, ada4, norm_pre, norm_post, y_gla, y_diff, y_s5, w_in_t, wbr, wout, layer)
        h = _ffn_call(h, ada4, norm_pre, norm_post, *f2, layer, 2)
    return h
```

```python
import functools
import math

import numpy as np
import jax
import jax.numpy as jnp
from jax import lax
from jax.experimental import pallas as pl
from jax.experimental.pallas import tpu as pltpu

F32 = jnp.float32
BF16 = jnp.bfloat16

D_MODEL = 1024
DEPTH = 2
D_FF = 2816
ADA_CHUNKS = 9
EPS = 1e-6
LOG2E = math.log2(math.e)

GLA_HEADS = 4
GLA_DK = 64
GLA_DV = 128
GLA_LOWRANK = 16
GLA_TAU = 16.0
GLA_CHUNK = 64
GLA_QK_W = GLA_HEADS * GLA_DK
GLA_V_W = GLA_HEADS * GLA_DV

DIFF_HEADS = 4
DIFF_DQK = 64
DIFF_DV = 128
DIFF_QK_W = DIFF_HEADS * 2 * DIFF_DQK
DIFF_V_W = DIFF_HEADS * DIFF_DV
NUM_BUCKETS = 32
MAX_DISTANCE = 128

S5_GROUPS = 32
S5_GROUP_CH = 16
S5_STATE = 64
S5_WIDTH = S5_GROUPS * S5_GROUP_CH
S5_NSTATE = S5_GROUPS * S5_STATE
S5_GROUP_BLOCKS = 4
S5_BLOCK_CH = S5_WIDTH // S5_GROUP_BLOCKS
S5_BLOCK_STATE = S5_NSTATE // S5_GROUP_BLOCKS

N_BRANCH = 3

SUBLANES = 8
LANES = 128
ONES_ROWS = 2 * SUBLANES
VMEM_LIMIT_BYTES = 56 * 1024 * 1024

TOKEN_TILE = 1024
ROW_BLOCK = 512
FF_CHUNK = 256
FF_NCHUNK = D_FF // FF_CHUNK
WEIGHT_STAGE_COLS = 512
WEIGHT_STAGE_SLOTS = 4
MIX_GATE_ROW = 2 * GLA_QK_W + 2 * GLA_V_W + GLA_LOWRANK + 2 * DIFF_QK_W + DIFF_V_W + S5_WIDTH
MIX_STAGE_ROWS = 240
FFN_LOOKAHEAD = 1
GLA_PAIR = 2 * GLA_CHUNK
GLA_TILE = 1024
ATT_TILE = 256
ATT_LOOKAHEAD = 4
S5_TIME_TILE = 64
ADA_COL_TILE = 1536


def _dot(a, b):
    return jnp.dot(a, b, preferred_element_type=F32)


def _dot_nt(a, b):
    return lax.dot_general(a, b, (((1,), (1,)), ((), ())), preferred_element_type=F32)


def _rms(x):
    return x * lax.rsqrt(jnp.mean(x * x, axis=-1, keepdims=True) + EPS)


def _prenorm_mod(h, g, shift, scale):
    return (_rms(h) * g) * (1.0 + scale) + shift


def _params(*sem):
    return pltpu.CompilerParams(dimension_semantics=sem, vmem_limit_bytes=VMEM_LIMIT_BYTES)


def _resident(shape, index_map):
    return pl.BlockSpec(shape, index_map, pipeline_mode=pl.Buffered(1))


def _ada_kernel(c_ref, w_ref, b_ref, o_ref):
    cond = jax.nn.silu(c_ref[...]).astype(BF16)
    o_ref[...] = _dot(cond, w_ref[...].astype(BF16)) + b_ref[...]


def _ada_call(c, w_ada, b_ada):
    bsz = c.shape[0]
    n = ADA_CHUNKS * D_MODEL
    return pl.pallas_call(
        _ada_kernel,
        grid=(DEPTH, n // ADA_COL_TILE),
        in_specs=[
            pl.BlockSpec((bsz, D_MODEL), lambda l, j: (0, 0)),
            pl.BlockSpec((None, D_MODEL, ADA_COL_TILE), lambda l, j: (l, 0, j)),
            pl.BlockSpec((None, 1, ADA_COL_TILE), lambda l, j: (l, 0, j)),
        ],
        out_specs=pl.BlockSpec((None, bsz, ADA_COL_TILE), lambda l, j: (l, 0, j)),
        out_shape=jax.ShapeDtypeStruct((DEPTH, bsz, n), F32),
        compiler_params=_params("arbitrary", "arbitrary"),
        name="ada_proj",
    )(c, w_ada, b_ada.reshape(DEPTH, 1, n))


def _stream_cast(src, dst, stage, sem, *, axis, chunk):
    slots = stage.shape[0]

    def window(c):
        part = slice(c * chunk, (c + 1) * chunk)
        return (slice(None), part) if axis == 1 else (part, slice(None))

    def copy(c):
        return pltpu.make_async_copy(src.at[window(c)], stage.at[c % slots], sem.at[c % slots])

    n = src.shape[axis] // chunk
    for c in range(min(slots - 1, n)):
        copy(c).start()
    for c in range(n):
        if c + slots - 1 < n:
            copy(c + slots - 1).start()
        copy(c).wait()
        dst[window(c)] = stage[c % slots].astype(BF16)


def _ffn_kernel(h_ref, ada_ref, npre_ref, npost_ref, wup_hbm, wdn_hbm, o_ref,
                wup_ref, wdn_ref, up_stage, dn_stage, up_sem, dn_sem, *, sub, layer):
    a0 = 3 * sub

    @pl.when((pl.program_id(0) == 0) & (pl.program_id(1) == 0))
    def _():
        _stream_cast(wup_hbm.at[layer], wup_ref, up_stage, up_sem, axis=1, chunk=WEIGHT_STAGE_COLS)
        _stream_cast(wdn_hbm.at[layer], wdn_ref, dn_stage, dn_sem, axis=0, chunk=FF_CHUNK)

    def down(j, gate, up):
        mid = (jax.nn.silu(gate) * up).astype(BF16)
        return _dot(mid, wdn_ref[j * FF_CHUNK:(j + 1) * FF_CHUNK, :])

    for r in range(h_ref.shape[0] // ROW_BLOCK):
        rows = slice(r * ROW_BLOCK, (r + 1) * ROW_BLOCK)
        h = h_ref[rows, :]
        u = _prenorm_mod(h, npre_ref[sub:sub + 1, :], ada_ref[a0:a0 + 1, :], ada_ref[a0 + 1:a0 + 2, :])
        u = u.astype(BF16)

        def gate_up(j, u=u):
            lo = j * FF_CHUNK
            return _dot(u, wup_ref[:, lo:lo + FF_CHUNK]), _dot(u, wup_ref[:, D_FF + lo:D_FF + lo + FF_CHUNK])

        acc = None
        pending = []
        for j in range(FF_NCHUNK + FFN_LOOKAHEAD):
            if j < FF_NCHUNK:
                pending.append((j,) + gate_up(j))
            if j >= FFN_LOOKAHEAD:
                d = down(*pending.pop(0))
                acc = d if acc is None else acc + d
        post = _rms(acc) * npost_ref[sub:sub + 1, :]
        o_ref[rows, :] = h + (0.5 * ada_ref[a0 + 2:a0 + 3, :]) * post


def _ffn_call(h, ada4, norm_pre, norm_post, wup, wdn, layer, sub):
    bsz, seq, _ = h.shape
    tm = TOKEN_TILE
    tok = pl.BlockSpec((None, tm, D_MODEL), lambda b, i: (b, i, 0))
    return pl.pallas_call(
        functools.partial(_ffn_kernel, sub=sub, layer=layer),
        grid=(bsz, seq // tm),
        in_specs=[
            tok,
            pl.BlockSpec((None, None, ADA_CHUNKS, D_MODEL), lambda b, i: (layer, b, 0, 0)),
            pl.BlockSpec((None, 3, D_MODEL), lambda b, i: (layer, 0, 0)),
            pl.BlockSpec((None, 3, D_MODEL), lambda b, i: (layer, 0, 0)),
            pl.BlockSpec(memory_space=pl.ANY),
            pl.BlockSpec(memory_space=pl.ANY),
        ],
        out_specs=tok,
        out_shape=jax.ShapeDtypeStruct(h.shape, F32),
        scratch_shapes=[pltpu.VMEM((D_MODEL, 2 * D_FF), BF16), pltpu.VMEM((D_FF, D_MODEL), BF16),
                        pltpu.VMEM((WEIGHT_STAGE_SLOTS, D_MODEL, WEIGHT_STAGE_COLS), F32),
                        pltpu.VMEM((WEIGHT_STAGE_SLOTS, FF_CHUNK, D_MODEL), F32),
                        pltpu.SemaphoreType.DMA((WEIGHT_STAGE_SLOTS,)),
                        pltpu.SemaphoreType.DMA((WEIGHT_STAGE_SLOTS,))],
        compiler_params=_params("arbitrary", "arbitrary"),
        name=f"ffn{sub // 2 + 1}",
    )(h, ada4, norm_pre, norm_post, wup, wdn)


def _log_sigmoid(z):
    return jnp.minimum(z, 0.0) - jnp.log1p(jnp.exp(-jnp.abs(z)))


def _inproj_kernel(h_ref, ada_ref, npre_ref, wt_hbm, walpha_ref, balpha_ref,
                   gq_ref, gk_ref, gv_ref, gr_ref, la_ref, dq_ref, dk_ref, dv_ref, su_ref,
                   wt_ref, stage, sem, *, layer):
    @pl.when((pl.program_id(0) == 0) & (pl.program_id(1) == 0))
    def _():
        _stream_cast(wt_hbm.at[layer, 0:MIX_GATE_ROW], wt_ref, stage, sem, axis=0, chunk=MIX_STAGE_ROWS)

    o_alpha = 2 * GLA_QK_W + 2 * GLA_V_W
    o_diff = o_alpha + GLA_LOWRANK
    o_s5 = o_diff + 2 * DIFF_QK_W + DIFF_V_W
    for r in range(h_ref.shape[0] // ROW_BLOCK):
        rows = slice(r * ROW_BLOCK, (r + 1) * ROW_BLOCK)
        u = _prenorm_mod(h_ref[rows, :], npre_ref[1:2, :], ada_ref[3:4, :], ada_ref[4:5, :]).astype(BF16)
        a_low = _dot_nt(u, wt_ref[o_alpha:o_alpha + LANES, :]).astype(BF16)
        pg = _dot_nt(u, wt_ref[0:o_alpha, :])
        z = _dot(a_low, walpha_ref[...]) + balpha_ref[...]
        gq_ref[rows, :] = pg[:, :GLA_QK_W]
        gk_ref[rows, :] = pg[:, GLA_QK_W:2 * GLA_QK_W]
        gv_ref[rows, :] = pg[:, 2 * GLA_QK_W:2 * GLA_QK_W + GLA_V_W].astype(BF16)
        gr_ref[rows, :] = pg[:, 2 * GLA_QK_W + GLA_V_W:]
        la_ref[rows, :] = _log_sigmoid(z) / GLA_TAU
        pd = _dot_nt(u, wt_ref[o_diff:o_s5, :])
        dq_ref[rows, :] = (pd[:, :DIFF_QK_W] * (DIFF_DQK ** -0.5 * LOG2E)).astype(BF16)
        dk_ref[rows, :] = pd[:, DIFF_QK_W:2 * DIFF_QK_W].astype(BF16)
        dv_ref[rows, :] = pd[:, 2 * DIFF_QK_W:].astype(BF16)
        su_ref[rows, :] = _dot_nt(u, wt_ref[o_s5:MIX_GATE_ROW, :]).astype(BF16)


def _inproj_call(h, ada4, norm_pre, w_in_t, walpha, balpha, layer):
    bsz, seq, _ = h.shape
    tm = TOKEN_TILE

    def tok(w):
        return pl.BlockSpec((None, tm, w), lambda b, i: (b, i, 0))

    def sds(w, dt):
        return jax.ShapeDtypeStruct((bsz, seq, w), dt)

    def res(arr):
        nd = arr.ndim - 1
        return _resident((None,) + arr.shape[1:], lambda b, i: (layer,) + (0,) * nd)

    return pl.pallas_call(
        functools.partial(_inproj_kernel, layer=layer),
        grid=(bsz, seq // tm),
        in_specs=[
            tok(D_MODEL),
            pl.BlockSpec((None, None, ADA_CHUNKS, D_MODEL), lambda b, i: (layer, b, 0, 0)),
            pl.BlockSpec((None, 3, D_MODEL), lambda b, i: (layer, 0, 0)),
            pl.BlockSpec(memory_space=pl.ANY), res(walpha), res(balpha),
        ],
        out_specs=[tok(GLA_QK_W), tok(GLA_QK_W), tok(GLA_V_W), tok(GLA_V_W), tok(GLA_QK_W),
                   tok(DIFF_QK_W), tok(DIFF_QK_W), tok(DIFF_V_W), tok(S5_WIDTH)],
        out_shape=[sds(GLA_QK_W, F32), sds(GLA_QK_W, F32), sds(GLA_V_W, BF16), sds(GLA_V_W, F32),
                   sds(GLA_QK_W, F32), sds(DIFF_QK_W, BF16), sds(DIFF_QK_W, BF16), sds(DIFF_V_W, BF16),
                   sds(S5_WIDTH, BF16)],
        scratch_shapes=[pltpu.VMEM((MIX_GATE_ROW, D_MODEL), BF16),
                        pltpu.VMEM((WEIGHT_STAGE_SLOTS, MIX_STAGE_ROWS, D_MODEL), F32),
                        pltpu.SemaphoreType.DMA((WEIGHT_STAGE_SLOTS,))],
        compiler_params=_params("arbitrary", "arbitrary"),
        name="mixer_in_proj",
    )(h, ada4, norm_pre, w_in_t, walpha, balpha)


def _cumsum_rows(tri, x):
    hi = x.astype(BF16)
    r1 = x - hi.astype(F32)
    mid = r1.astype(BF16)
    lo = (r1 - mid.astype(F32)).astype(BF16)
    return _dot(tri, hi) + _dot(tri, mid) + _dot(tri, lo)


def _gla_kernel(q_ref, k_ref, v_ref, r_ref, la_ref, g_ref, o_ref, s_ref, *, n_pairs):
    @pl.when(pl.program_id(1) == 0)
    def _():
        s_ref[...] = jnp.zeros_like(s_ref)

    row = lax.broadcasted_iota(jnp.int32, (GLA_PAIR, GLA_PAIR), 0)
    col = lax.broadcasted_iota(jnp.int32, (GLA_PAIR, GLA_PAIR), 1)
    causal = (col <= row) & ((row // GLA_CHUNK) == (col // GLA_CHUNK))
    tri = jnp.where(causal, 1.0, 0.0).astype(BF16)
    qlane = lax.broadcasted_iota(jnp.int32, (GLA_PAIR, GLA_QK_W), 1)
    tlane = lax.broadcasted_iota(jnp.int32, (GLA_QK_W, GLA_PAIR), 1)
    first = tlane < GLA_CHUNK

    def decays(p):
        rows = slice(p * GLA_PAIR, (p + 1) * GLA_PAIR)
        b = _cumsum_rows(tri, la_ref[rows, :])
        b_t = b.T
        qd = (q_ref[rows, :] * (GLA_DK ** -0.5)) * jnp.exp(b)
        k_t = k_ref[rows, :].T
        bl0 = b_t[:, GLA_CHUNK - 1:GLA_CHUNK]
        bl1 = b_t[:, GLA_PAIR - 1:GLA_PAIR]
        kdec_t = k_t * jnp.exp(jnp.where(first, bl0, bl1) - b_t)
        return dict(
            qh=[jnp.where(qlane // GLA_DK == h, qd, 0.0).astype(BF16) for h in range(GLA_HEADS)],
            kd_t=(k_t * jnp.exp(-b_t)).astype(BF16),
            kdec0=jnp.where(first, kdec_t, 0.0).astype(BF16),
            kdec1=jnp.where(first, 0.0, kdec_t).astype(BF16),
            dec0=jnp.exp(bl0), dec1=jnp.exp(bl1))

    def products(p, d, s0):
        v = v_ref[p * GLA_PAIR:(p + 1) * GLA_PAIR, :]
        ds0, ds1 = [], []
        for h in range(GLA_HEADS):
            hr = slice(h * GLA_DK, (h + 1) * GLA_DK)
            vh = v[:, h * GLA_DV:(h + 1) * GLA_DV]
            ds0.append(_dot(d["kdec0"][hr, :], vh))
            ds1.append(_dot(d["kdec1"][hr, :], vh))
        att = [jnp.where(causal, _dot(qh, d["kd_t"]), 0.0).astype(BF16) for qh in d["qh"]]
        s1 = d["dec0"] * s0 + jnp.concatenate(ds0, axis=0)
        s2 = d["dec1"] * s1 + jnp.concatenate(ds1, axis=0)
        s0b = s0.astype(BF16)
        s1b = s1.astype(BF16)
        inter = [jnp.concatenate([_dot(qh[:GLA_CHUNK], s0b), _dot(qh[GLA_CHUNK:], s1b)], axis=0)
                 for qh in d["qh"]]
        return s2, (v, att, inter)

    def outputs(p, v, att, inter):
        rows = slice(p * GLA_PAIR, (p + 1) * GLA_PAIR)
        r = r_ref[rows, :]
        for h in range(GLA_HEADS):
            vc = slice(h * GLA_DV, (h + 1) * GLA_DV)
            o = _rms(_dot(att[h], v[:, vc]) + inter[h]) * g_ref[:, vc]
            o_ref[rows, vc] = (o * jax.nn.silu(r[:, vc])).astype(BF16)

    state = s_ref[...]
    dec, prod = {}, {}
    for i in range(n_pairs + 2):
        if i < n_pairs:
            dec[i] = decays(i)
        if 0 <= i - 1 < n_pairs:
            state, prod[i - 1] = products(i - 1, dec.pop(i - 1), state)
        if 0 <= i - 2 < n_pairs:
            outputs(i - 2, *prod.pop(i - 2))
    s_ref[...] = state


def _gla_call(gq, gk, gv, gr, la, gla_norm, layer):
    bsz, seq, _ = gq.shape
    tl = GLA_TILE

    def tok(w):
        return pl.BlockSpec((None, tl, w), lambda b, i: (b, i, 0))

    return pl.pallas_call(
        functools.partial(_gla_kernel, n_pairs=tl // GLA_PAIR),
        grid=(bsz, seq // tl),
        in_specs=[tok(GLA_QK_W), tok(GLA_QK_W), tok(GLA_V_W), tok(GLA_V_W), tok(GLA_QK_W),
                  pl.BlockSpec((None, 1, GLA_V_W), lambda b, i: (layer, 0, 0))],
        out_specs=tok(GLA_V_W),
        out_shape=jax.ShapeDtypeStruct((bsz, seq, GLA_V_W), BF16),
        scratch_shapes=[pltpu.VMEM((GLA_QK_W, GLA_DV), F32)],
        compiler_params=_params("arbitrary", "arbitrary"),
        name="gla_mixer",
    )(gq, gk, gv, gr, la, gla_norm.reshape(DEPTH, 1, GLA_V_W))


def _t5_bucket_tiles():
    t = ATT_TILE
    r = np.arange(t)[:, None]
    c = np.arange(t)[None, :]
    max_exact = NUM_BUCKETS // 2
    tiles = []
    for off in range(3):
        dist = off * t + r - c
        d = np.maximum(dist, 1).astype(np.float32)
        large = max_exact + (np.log(d / np.float32(max_exact)) / np.float32(math.log(MAX_DISTANCE / max_exact))
                             * np.float32(NUM_BUCKETS - max_exact)).astype(np.int32)
        large = np.minimum(large, NUM_BUCKETS - 1)
        bucket = np.where(dist < max_exact, dist, large)
        tiles.append(np.where(dist >= 0, bucket, -1).T)
    far = tiles[2]
    assert (far == NUM_BUCKETS - 1).all()
    return np.stack(tiles).astype(np.int32)


def _bias_kernel(rel_ref, bucket_ref, o_ref):
    m = pl.program_id(0)
    bucket = bucket_ref[0:2]
    acc = jnp.full(bucket.shape, -jnp.inf, F32)
    for kk in range(NUM_BUCKETS):
        acc = jnp.where(bucket == kk, rel_ref[kk, m], acc)
    o_ref[0:2] = acc * LOG2E
    o_ref[2] = jnp.full(o_ref.shape[1:], rel_ref[NUM_BUCKETS - 1, m], F32) * LOG2E


def _bias_call(rel_bias):
    n_maps = 2 * DIFF_HEADS
    t = ATT_TILE
    return pl.pallas_call(
        _bias_kernel,
        grid=(n_maps,),
        in_specs=[pl.BlockSpec(memory_space=pltpu.SMEM),
                  pl.BlockSpec((3, t, t), lambda m: (0, 0, 0))],
        out_specs=pl.BlockSpec((None, 3, t, t), lambda m: (m, 0, 0, 0)),
        out_shape=jax.ShapeDtypeStruct((n_maps, 3, t, t), F32),
        compiler_params=_params("arbitrary"),
        name="t5_bias_tiles",
    )(rel_bias, jnp.asarray(_t5_bucket_tiles()))


def _diff_kernel(q_ref, k_ref, v_ref, bias_ref, lam_ref, g_ref, o_ref, *, lam_init, n_tiles):
    t = ATT_TILE
    lv = lam_ref[...]
    lam = (jnp.exp(jnp.sum(lv[0:1] * lv[1:2], axis=-1, keepdims=True))
           - jnp.exp(jnp.sum(lv[2:3] * lv[3:4], axis=-1, keepdims=True)) + lam_init)
    v_t = jnp.concatenate([v_ref[...].astype(F32).T, jnp.ones((ONES_ROWS, v_ref.shape[0]), F32)],
                          axis=0).astype(BF16)
    lane = lax.broadcasted_iota(jnp.int32, (t, 2 * DIFF_DQK), 1)
    far = [bias_ref[mp, 2, 0:1, 0:1] for mp in range(2)]

    tasks = [(qi, kj, mp) for qi in range(n_tiles) for kj in range(qi + 1) for mp in range(2)]
    masked_q = {}
    state = {}
    done = {}

    def logits(qi, kj, mp):
        if qi not in masked_q:
            q = q_ref[qi * t:(qi + 1) * t, :]
            zero = jnp.zeros_like(q)
            masked_q[qi] = (jnp.where(lane < DIFF_DQK, q, zero), jnp.where(lane < DIFF_DQK, zero, q))
        s = _dot_nt(k_ref[kj * t:(kj + 1) * t, :], masked_q[qi][mp])
        return s + bias_ref[mp, qi - kj] if qi - kj < 2 else s

    def accumulate(qi, kj, mp, s):
        shift = 0.0 if qi - kj < 2 else far[mp]
        m_blk = jnp.max(s, axis=0, keepdims=True) + shift
        prev = state.get((qi, mp))
        m_new = m_blk if prev is None else jnp.maximum(prev[0], m_blk)
        p = jnp.exp2((s - (m_new - shift)).astype(BF16))
        pv = _dot(v_t[:, kj * t:(kj + 1) * t], p)
        if prev is None:
            state[(qi, mp)] = (m_new, pv)
        else:
            state[(qi, mp)] = (m_new, jnp.exp2(prev[0] - m_new) * prev[1] + pv)
        if kj == qi:
            _, acc = state.pop((qi, mp))
            done.setdefault(qi, {})[mp] = acc[:DIFF_DV] / acc[DIFF_DV:DIFF_DV + 1]
            if len(done[qi]) == 2:
                o = done[qi][0] - lam * done[qi][1]
                o = o * lax.rsqrt(jnp.mean(o * o, axis=0, keepdims=True) + EPS)
                o_ref[qi * t:(qi + 1) * t, :] = (o.T * g_ref[...] * (1.0 - lam_init)).astype(BF16)
                del done[qi]

    pending = []
    for task in tasks:
        pending.append((task, logits(*task)))
        if len(pending) > ATT_LOOKAHEAD:
            ready, s = pending.pop(0)
            accumulate(*ready, s)
    for ready, s in pending:
        accumulate(*ready, s)


def _diff_call(dq, dk, dv, bias, diff_lambda, diff_norm, layer):
    bsz, seq, _ = dq.shape
    t = ATT_TILE
    lam_init = 0.8 - 0.6 * math.exp(-0.3 * layer)
    return pl.pallas_call(
        functools.partial(_diff_kernel, lam_init=lam_init, n_tiles=seq // t),
        grid=(bsz, DIFF_HEADS),
        in_specs=[
            pl.BlockSpec((None, seq, 2 * DIFF_DQK), lambda b, h: (b, 0, h)),
            pl.BlockSpec((None, seq, 2 * DIFF_DQK), lambda b, h: (b, 0, h)),
            pl.BlockSpec((None, seq, DIFF_DV), lambda b, h: (b, 0, h)),
            pl.BlockSpec((2, 3, t, t), lambda b, h: (h, 0, 0, 0)),
            pl.BlockSpec((None, 4, DIFF_DQK), lambda b, h: (layer, 0, 0)),
            pl.BlockSpec((None, 1, DIFF_DV), lambda b, h: (layer, 0, h)),
        ],
        out_specs=pl.BlockSpec((None, seq, DIFF_DV), lambda b, h: (b, 0, h)),
        out_shape=jax.ShapeDtypeStruct((bsz, seq, DIFF_V_W), BF16),
        compiler_params=_params("arbitrary", "arbitrary"),
        name="diff_attention",
    )(dq, dk, dv, bias, diff_lambda, diff_norm.reshape(DEPTH, 1, DIFF_V_W))


def _s5_prep_kernel(lre_ref, lim_ref, lstep_ref, bre_ref, bim_ref, cre_ref, cim_ref,
                    are_ref, aim_ref, bmat_ref, cmat_ref):
    lr = lre_ref[...]
    li = lim_ref[...]
    dt = jnp.exp(lstep_ref[...])
    mag = jnp.exp(lr * dt)
    a_re = mag * jnp.cos(li * dt)
    a_im = mag * jnp.sin(li * dt)
    are_ref[...] = jnp.broadcast_to(a_re, are_ref.shape)
    aim_ref[...] = jnp.broadcast_to(a_im, aim_ref.shape)
    nr, ni = a_re - 1.0, a_im
    den = lr * lr + li * li
    f_re = (nr * lr + ni * li) / den
    f_im = (ni * lr - nr * li) / den
    ns = S5_BLOCK_STATE
    for gb in range(S5_GROUP_BLOCKS):
        fr = f_re[:, gb * ns:(gb + 1) * ns]
        fi = f_im[:, gb * ns:(gb + 1) * ns]
        br = bre_ref[gb]
        bi = bim_ref[gb]
        bmat_ref[gb, :, :ns] = (fr * br - fi * bi).astype(BF16)
        bmat_ref[gb, :, ns:] = (fr * bi + fi * br).astype(BF16)
        cmat_ref[gb, :ns, :] = cre_ref[gb].astype(BF16)
        cmat_ref[gb, ns:, :] = (-cim_ref[gb]).astype(BF16)


def _s5_prep_call(lam_re, lam_im, log_step, b_re, b_im, c_re, c_im, bsz):
    d = DEPTH
    nb, ch, ns = S5_GROUP_BLOCKS, S5_BLOCK_CH, S5_BLOCK_STATE
    gpb = S5_GROUPS // nb
    eye = jnp.eye(gpb, dtype=F32)

    def in_blocks(b):
        t = b.reshape(d, nb, gpb, S5_STATE, S5_GROUP_CH).transpose(0, 1, 2, 4, 3)
        return (t[:, :, :, :, None, :] * eye[None, None, :, None, :, None]).reshape(d, nb, ch, ns)

    def out_blocks(c):
        t = c.reshape(d, nb, gpb, S5_GROUP_CH, S5_STATE).transpose(0, 1, 4, 2, 3)
        return (t[:, :, None, :, :, :] * eye[None, None, :, None, :, None]).reshape(d, nb, ns, ch)

    def row(x):
        return x.reshape(d, 1, S5_NSTATE)

    lstep = jnp.broadcast_to(log_step[:, :, None], (d, S5_GROUPS, S5_STATE))

    def full(shape):
        nd = len(shape)
        return pl.BlockSpec((None,) + shape, lambda l: (l,) + (0,) * nd)

    return pl.pallas_call(
        _s5_prep_kernel,
        grid=(d,),
        in_specs=[full((1, S5_NSTATE))] * 3 + [full((nb, ch, ns))] * 2 + [full((nb, ns, ch))] * 2,
        out_specs=[full((bsz, S5_NSTATE)), full((bsz, S5_NSTATE)),
                   full((nb, ch, 2 * ns)), full((nb, 2 * ns, ch))],
        out_shape=[jax.ShapeDtypeStruct((d, bsz, S5_NSTATE), F32), jax.ShapeDtypeStruct((d, bsz, S5_NSTATE), F32),
                   jax.ShapeDtypeStruct((d, nb, ch, 2 * ns), BF16), jax.ShapeDtypeStruct((d, nb, 2 * ns, ch), BF16)],
        compiler_params=_params("arbitrary"),
        name="s5_discretize",
    )(row(lam_re), row(lam_im), row(lstep), in_blocks(b_re), in_blocks(b_im), out_blocks(c_re), out_blocks(c_im))


def _s5_kernel(u_ref, are_ref, aim_ref, bmat_ref, cmat_ref, d_ref, wglu_ref, bglu_ref, o_ref,
               tb_ref, xr_ref, xi_ref, sr_ref, si_ref, *, bsz, n_steps):
    @pl.when(pl.program_id(0) == 0)
    def _():
        sr_ref[...] = jnp.zeros_like(sr_ref)
        si_ref[...] = jnp.zeros_like(si_ref)

    ns, ch, nb = S5_BLOCK_STATE, S5_BLOCK_CH, S5_GROUP_BLOCKS
    for b in range(bsz):
        ub = u_ref[b].astype(F32)
        for gb in range(nb):
            tb_ref[gb, pl.ds(b, n_steps, stride=bsz), :] = ub[:, gb * ch:(gb + 1) * ch]
    u = [tb_ref[gb] for gb in range(nb)]

    def drive(gb):
        return _dot(u[gb].astype(BF16), bmat_ref[gb])

    def scan(gb, bu):
        cols = slice(gb * ns, (gb + 1) * ns)
        ar, ai = are_ref[:, cols], aim_ref[:, cols]
        pr, pi = sr_ref[:, cols], si_ref[:, cols]
        for t in range(n_steps):
            rows = slice(t * bsz, (t + 1) * bsz)
            pr, pi = ar * pr - ai * pi + bu[rows, :ns], ar * pi + ai * pr + bu[rows, ns:]
            xr_ref[rows, cols] = pr
            xi_ref[rows, cols] = pi
        sr_ref[:, cols] = pr
        si_ref[:, cols] = pi

    def readout(gb):
        cols = slice(gb * ns, (gb + 1) * ns)
        return (_dot(xr_ref[:, cols].astype(BF16), cmat_ref[gb, :ns, :])
                + _dot(xi_ref[:, cols].astype(BF16), cmat_ref[gb, ns:, :]))

    bu, ys = {0: drive(0)}, []
    for gb in range(nb):
        if gb + 1 < nb:
            bu[gb + 1] = drive(gb + 1)
        scan(gb, bu.pop(gb))
        if gb >= 1:
            ys.append(readout(gb - 1))
    ys.append(readout(nb - 1))

    y = jax.nn.gelu(jnp.concatenate(ys, axis=-1) + d_ref[...] * jnp.concatenate(u, axis=-1))
    z = _dot(y.astype(BF16), wglu_ref[...]) + bglu_ref[...]
    out = z[:, :S5_WIDTH] * jax.nn.sigmoid(z[:, S5_WIDTH:])
    for gb in range(nb):
        tb_ref[gb] = out[:, gb * ch:(gb + 1) * ch]
    for b in range(bsz):
        for gb in range(nb):
            o_ref[b, :, gb * ch:(gb + 1) * ch] = tb_ref[gb, pl.ds(b, n_steps, stride=bsz), :].astype(BF16)


def _s5_call(su, a_re, a_im, bmat, cmat, d_skip, wglu, bglu, layer):
    bsz, seq, _ = su.shape
    tl = S5_TIME_TILE
    tr = tl * bsz
    nb, ch, ns = S5_GROUP_BLOCKS, S5_BLOCK_CH, S5_BLOCK_STATE

    def res(shape):
        nd = len(shape)
        return _resident((None,) + shape, lambda i: (layer,) + (0,) * nd)

    tok = pl.BlockSpec((bsz, tl, S5_WIDTH), lambda i: (0, i, 0))
    return pl.pallas_call(
        functools.partial(_s5_kernel, bsz=bsz, n_steps=tl),
        grid=(seq // tl,),
        in_specs=[tok,
                  res((bsz, S5_NSTATE)), res((bsz, S5_NSTATE)),
                  res((nb, ch, 2 * ns)), res((nb, 2 * ns, ch)),
                  res((1, S5_WIDTH)), res((S5_WIDTH, 2 * S5_WIDTH)), res((1, 2 * S5_WIDTH))],
        out_specs=tok,
        out_shape=jax.ShapeDtypeStruct((bsz, seq, S5_WIDTH), BF16),
        scratch_shapes=[pltpu.VMEM((nb, tr, ch), F32),
                        pltpu.VMEM((tr, S5_NSTATE), F32), pltpu.VMEM((tr, S5_NSTATE), F32),
                        pltpu.VMEM((bsz, S5_NSTATE), F32), pltpu.VMEM((bsz, S5_NSTATE), F32)],
        compiler_params=_params("arbitrary"),
        name="s5_mixer",
    )(su, a_re, a_im, bmat, cmat, d_skip, wglu, bglu)


def _merge_kernel(h_ref, ada_ref, npre_ref, npost_ref, yg_ref, yd_ref, ys_ref, wt_hbm, wbr_hbm, wout_hbm, o_ref,
                  wgate_ref, wbr_ref, wout_ref, stage, sem, *, layer):
    @pl.when((pl.program_id(0) == 0) & (pl.program_id(1) == 0))
    def _():
        _stream_cast(wt_hbm.at[layer, MIX_GATE_ROW:MIX_GATE_ROW + N_BRANCH * D_MODEL], wgate_ref, stage, sem,
                     axis=0, chunk=FF_CHUNK)
        _stream_cast(wbr_hbm.at[layer], wbr_ref, stage, sem, axis=0, chunk=FF_CHUNK)
        _stream_cast(wout_hbm.at[layer], wout_ref, stage, sem, axis=0, chunk=FF_CHUNK)

    blocks = [slice(r * ROW_BLOCK, (r + 1) * ROW_BLOCK) for r in range(h_ref.shape[0] // ROW_BLOCK)]
    us = [_prenorm_mod(h_ref[rows, :], npre_ref[1:2, :], ada_ref[3:4, :], ada_ref[4:5, :]).astype(BF16)
          for rows in blocks]
    merged = [None] * len(blocks)
    for i, y_ref in enumerate((yg_ref, yd_ref, ys_ref)):
        for r, rows in enumerate(blocks):
            gate = jax.nn.sigmoid(_dot_nt(us[r], wgate_ref[i * D_MODEL:(i + 1) * D_MODEL, :]))
            term = gate * _dot(y_ref[rows, :], wbr_ref[i * S5_WIDTH:(i + 1) * S5_WIDTH, :])
            merged[r] = term if merged[r] is None else merged[r] + term
    ys = [_dot(m.astype(BF16), wout_ref[...]) for m in merged]
    for rows, y in zip(blocks, ys):
        o_ref[rows, :] = h_ref[rows, :] + ada_ref[5:6, :] * (_rms(y) * npost_ref[1:2, :])


def _merge_call(h, ada4, norm_pre, norm_post, y_gla, y_diff, y_s5, w_in_t, wbr, wout, layer):
    bsz, seq, _ = h.shape
    tm = TOKEN_TILE

    def tok(w):
        return pl.BlockSpec((None, tm, w), lambda b, i: (b, i, 0))

    return pl.pallas_call(
        functools.partial(_merge_kernel, layer=layer),
        grid=(bsz, seq // tm),
        in_specs=[
            tok(D_MODEL),
            pl.BlockSpec((None, None, ADA_CHUNKS, D_MODEL), lambda b, i: (layer, b, 0, 0)),
            pl.BlockSpec((None, 3, D_MODEL), lambda b, i: (layer, 0, 0)),
            pl.BlockSpec((None, 3, D_MODEL), lambda b, i: (layer, 0, 0)),
            tok(GLA_V_W), tok(DIFF_V_W), tok(S5_WIDTH),
            pl.BlockSpec(memory_space=pl.ANY),
            pl.BlockSpec(memory_space=pl.ANY),
            pl.BlockSpec(memory_space=pl.ANY),
        ],
        out_specs=tok(D_MODEL),
        out_shape=jax.ShapeDtypeStruct(h.shape, F32),
        scratch_shapes=[pltpu.VMEM((N_BRANCH * D_MODEL, D_MODEL), BF16),
                        pltpu.VMEM((N_BRANCH * S5_WIDTH, D_MODEL), BF16), pltpu.VMEM((D_MODEL, D_MODEL), BF16),
                        pltpu.VMEM((WEIGHT_STAGE_SLOTS, FF_CHUNK, D_MODEL), F32),
                        pltpu.SemaphoreType.DMA((WEIGHT_STAGE_SLOTS,))],
        compiler_params=_params("arbitrary", "arbitrary"),
        name="mixer_merge",
    )(h, ada4, norm_pre, norm_post, y_gla, y_diff, y_s5, w_in_t, wbr, wout)


def kernel(x, c, rel_bias, w_ada, b_ada, norm_pre, norm_post, ffn1_w_up, ffn1_w_down, ffn2_w_up, ffn2_w_down,
           w_in, gla_w_alpha, gla_b_alpha, gla_norm, diff_lambda, diff_norm, s5_lam_re, s5_lam_im, s5_log_step,
           s5_b_re, s5_b_im, s5_c_re, s5_c_im, s5_d, s5_w_glu, s5_b_glu, w_branch, w_out):
    bsz, seq, _ = x.shape
    assert bsz == SUBLANES, "the S5 scan keeps the batch on the sublane axis"

    f1 = (ffn1_w_up, ffn1_w_down)
    f2 = (ffn2_w_up, ffn2_w_down)
    w_in_t = jnp.swapaxes(w_in, 1, 2)
    walpha = jnp.pad(gla_w_alpha.astype(BF16), ((0, 0), (0, LANES - GLA_LOWRANK), (0, 0)))
    balpha = gla_b_alpha.reshape(DEPTH, 1, GLA_QK_W)
    wbr = w_branch.reshape(DEPTH, N_BRANCH * S5_WIDTH, D_MODEL)
    wout = w_out
    wglu = s5_w_glu.astype(BF16)
    bglu = s5_b_glu.reshape(DEPTH, 1, 2 * S5_WIDTH)
    d_skip = s5_d.reshape(DEPTH, 1, S5_WIDTH)

    ada4 = _ada_call(c, w_ada, b_ada).reshape(DEPTH, bsz, ADA_CHUNKS, D_MODEL)
    bias = _bias_call(rel_bias)
    a_re, a_im, bmat, cmat = _s5_prep_call(s5_lam_re, s5_lam_im, s5_log_step, s5_b_re, s5_b_im,
                                           s5_c_re, s5_c_im, bsz)

    h = x
    for layer in range(DEPTH):
        h = _ffn_call(h, ada4, norm_pre, norm_post, *f1, layer, 0)
        gq, gk, gv, gr, la, dq, dk, dv, su = _inproj_call(h, ada4, norm_pre, w_in_t, walpha, balpha, layer)
        y_gla = _gla_call(gq, gk, gv, gr, la, gla_norm, layer)
        y_diff = _diff_call(dq, dk, dv, bias, diff_lambda, diff_norm, layer)
        y_s5 = _s5_call(su, a_re, a_im, bmat, cmat, d_skip, wglu, bglu, layer)
        h = _merge_call(h, ada4, norm_pre, norm_post, y_gla, y_diff, y_s5, w_in_t, wbr, wout, layer)
        h = _ffn_call(h, ada4, norm_pre, norm_post, *f2, layer, 2)
    return h
```

```python
import functools
import math

import numpy as np
import jax
import jax.numpy as jnp
from jax import lax
from jax.experimental import pallas as pl
from jax.experimental.pallas import tpu as pltpu

F32 = jnp.float32
BF16 = jnp.bfloat16

D_MODEL = 1024
DEPTH = 2
D_FF = 2816
ADA_CHUNKS = 9
EPS = 1e-6
LOG2E = math.log2(math.e)

GLA_HEADS = 4
GLA_DK = 64
GLA_DV = 128
GLA_LOWRANK = 16
GLA_TAU = 16.0
GLA_CHUNK = 64
GLA_QK_W = GLA_HEADS * GLA_DK
GLA_V_W = GLA_HEADS * GLA_DV

DIFF_HEADS = 4
DIFF_DQK = 64
DIFF_DV = 128
DIFF_QK_W = DIFF_HEADS * 2 * DIFF_DQK
DIFF_V_W = DIFF_HEADS * DIFF_DV
NUM_BUCKETS = 32
MAX_DISTANCE = 128

S5_GROUPS = 32
S5_GROUP_CH = 16
S5_STATE = 64
S5_WIDTH = S5_GROUPS * S5_GROUP_CH
S5_NSTATE = S5_GROUPS * S5_STATE
S5_GROUP_BLOCKS = 4
S5_BLOCK_CH = S5_WIDTH // S5_GROUP_BLOCKS
S5_BLOCK_STATE = S5_NSTATE // S5_GROUP_BLOCKS

N_BRANCH = 3

SUBLANES = 8
LANES = 128
ONES_ROWS = 2 * SUBLANES
VMEM_LIMIT_BYTES = 56 * 1024 * 1024

TOKEN_TILE = 1024
ROW_BLOCK = 512
FF_CHUNK = 256
FF_NCHUNK = D_FF // FF_CHUNK
WEIGHT_STAGE_COLS = 512
WEIGHT_STAGE_SLOTS = 4
MIX_GATE_ROW = 2 * GLA_QK_W + 2 * GLA_V_W + GLA_LOWRANK + 2 * DIFF_QK_W + DIFF_V_W + S5_WIDTH
MIX_STAGE_ROWS = 240
FFN_LOOKAHEAD = 1
GLA_PAIR = 2 * GLA_CHUNK
GLA_TILE = 1024
ATT_TILE = 256
ATT_LOOKAHEAD = 4
S5_TIME_TILE = 64
ADA_COL_TILE = 1536


def _dot(a, b):
    return jnp.dot(a, b, preferred_element_type=F32)


def _dot_nt(a, b):
    return lax.dot_general(a, b, (((1,), (1,)), ((), ())), preferred_element_type=F32)


def _rms(x):
    return x * lax.rsqrt(jnp.mean(x * x, axis=-1, keepdims=True) + EPS)


def _prenorm_mod(h, g, shift, scale):
    return (_rms(h) * g) * (1.0 + scale) + shift


def _params(*sem):
    return pltpu.CompilerParams(dimension_semantics=sem, vmem_limit_bytes=VMEM_LIMIT_BYTES)


def _resident(shape, index_map):
    return pl.BlockSpec(shape, index_map, pipeline_mode=pl.Buffered(1))


def _ada_kernel(c_ref, w_ref, b_ref, o_ref):
    cond = jax.nn.silu(c_ref[...]).astype(BF16)
    o_ref[...] = _dot(cond, w_ref[...].astype(BF16)) + b_ref[...]


def _ada_call(c, w_ada, b_ada):
    bsz = c.shape[0]
    n = ADA_CHUNKS * D_MODEL
    return pl.pallas_call(
        _ada_kernel,
        grid=(DEPTH, n // ADA_COL_TILE),
        in_specs=[
            pl.BlockSpec((bsz, D_MODEL), lambda l, j: (0, 0)),
            pl.BlockSpec((None, D_MODEL, ADA_COL_TILE), lambda l, j: (l, 0, j)),
            pl.BlockSpec((None, 1, ADA_COL_TILE), lambda l, j: (l, 0, j)),
        ],
        out_specs=pl.BlockSpec((None, bsz, ADA_COL_TILE), lambda l, j: (l, 0, j)),
        out_shape=jax.ShapeDtypeStruct((DEPTH, bsz, n), F32),
        compiler_params=_params("arbitrary", "arbitrary"),
        name="ada_proj",
    )(c, w_ada, b_ada.reshape(DEPTH, 1, n))


def _stream_cast(src, dst, stage, sem, *, axis, chunk):
    slots = stage.shape[0]

    def window(c):
        part = slice(c * chunk, (c + 1) * chunk)
        return (slice(None), part) if axis == 1 else (part, slice(None))

    def copy(c):
        return pltpu.make_async_copy(src.at[window(c)], stage.at[c % slots], sem.at[c % slots])

    n = src.shape[axis] // chunk
    for c in range(min(slots - 1, n)):
        copy(c).start()
    for c in range(n):
        if c + slots - 1 < n:
            copy(c + slots - 1).start()
        copy(c).wait()
        dst[window(c)] = stage[c % slots].astype(BF16)


def _ffn_kernel(h_ref, ada_ref, npre_ref, npost_ref, wup_hbm, wdn_hbm, o_ref,
                wup_ref, wdn_ref, up_stage, dn_stage, up_sem, dn_sem, *, sub, layer):
    a0 = 3 * sub

    @pl.when((pl.program_id(0) == 0) & (pl.program_id(1) == 0))
    def _():
        _stream_cast(wup_hbm.at[layer], wup_ref, up_stage, up_sem, axis=1, chunk=WEIGHT_STAGE_COLS)
        _stream_cast(wdn_hbm.at[layer], wdn_ref, dn_stage, dn_sem, axis=0, chunk=FF_CHUNK)

    def down(j, gate, up):
        mid = (jax.nn.silu(gate) * up).astype(BF16)
        return _dot(mid, wdn_ref[j * FF_CHUNK:(j + 1) * FF_CHUNK, :])

    for r in range(h_ref.shape[0] // ROW_BLOCK):
        rows = slice(r * ROW_BLOCK, (r + 1) * ROW_BLOCK)
        h = h_ref[rows, :]
        u = _prenorm_mod(h, npre_ref[sub:sub + 1, :], ada_ref[a0:a0 + 1, :], ada_ref[a0 + 1:a0 + 2, :])
        u = u.astype(BF16)

        def gate_up(j, u=u):
            lo = j * FF_CHUNK
            return _dot(u, wup_ref[:, lo:lo + FF_CHUNK]), _dot(u, wup_ref[:, D_FF + lo:D_FF + lo + FF_CHUNK])

        acc = None
        pending = []
        for j in range(FF_NCHUNK + FFN_LOOKAHEAD):
            if j < FF_NCHUNK:
                pending.append((j,) + gate_up(j))
            if j >= FFN_LOOKAHEAD:
                d = down(*pending.pop(0))
                acc = d if acc is None else acc + d
        post = _rms(acc) * npost_ref[sub:sub + 1, :]
        o_ref[rows, :] = h + (0.5 * ada_ref[a0 + 2:a0 + 3, :]) * post


def _ffn_call(h, ada4, norm_pre, norm_post, wup, wdn, layer, sub):
    bsz, seq, _ = h.shape
    tm = TOKEN_TILE
    tok = pl.BlockSpec((None, tm, D_MODEL), lambda b, i: (b, i, 0))
    return pl.pallas_call(
        functools.partial(_ffn_kernel, sub=sub, layer=layer),
        grid=(bsz, seq // tm),
        in_specs=[
            tok,
            pl.BlockSpec((None, None, ADA_CHUNKS, D_MODEL), lambda b, i: (layer, b, 0, 0)),
            pl.BlockSpec((None, 3, D_MODEL), lambda b, i: (layer, 0, 0)),
            pl.BlockSpec((None, 3, D_MODEL), lambda b, i: (layer, 0, 0)),
            pl.BlockSpec(memory_space=pl.ANY),
            pl.BlockSpec(memory_space=pl.ANY),
        ],
        out_specs=tok,
        out_shape=jax.ShapeDtypeStruct(h.shape, F32),
        scratch_shapes=[pltpu.VMEM((D_MODEL, 2 * D_FF), BF16), pltpu.VMEM((D_FF, D_MODEL), BF16),
                        pltpu.VMEM((WEIGHT_STAGE_SLOTS, D_MODEL, WEIGHT_STAGE_COLS), F32),
                        pltpu.VMEM((WEIGHT_STAGE_SLOTS, FF_CHUNK, D_MODEL), F32),
                        pltpu.SemaphoreType.DMA((WEIGHT_STAGE_SLOTS,)),
                        pltpu.SemaphoreType.DMA((WEIGHT_STAGE_SLOTS,))],
        compiler_params=_params("arbitrary", "arbitrary"),
        name=f"ffn{sub // 2 + 1}",
    )(h, ada4, norm_pre, norm_post, wup, wdn)


def _log_sigmoid(z):
    return jnp.minimum(z, 0.0) - jnp.log1p(jnp.exp(-jnp.abs(z)))


def _inproj_kernel(h_ref, ada_ref, npre_ref, wt_hbm, walpha_ref, balpha_ref,
                   gq_ref, gk_ref, gv_ref, gr_ref, la_ref, dq_ref, dk_ref, dv_ref, su_ref,
                   wt_ref, stage, sem, *, layer):
    @pl.when((pl.program_id(0) == 0) & (pl.program_id(1) == 0))
    def _():
        _stream_cast(wt_hbm.at[layer, 0:MIX_GATE_ROW], wt_ref, stage, sem, axis=0, chunk=MIX_STAGE_ROWS)

    o_alpha = 2 * GLA_QK_W + 2 * GLA_V_W
    o_diff = o_alpha + GLA_LOWRANK
    o_s5 = o_diff + 2 * DIFF_QK_W + DIFF_V_W
    for r in range(h_ref.shape[0] // ROW_BLOCK):
        rows = slice(r * ROW_BLOCK, (r + 1) * ROW_BLOCK)
        u = _prenorm_mod(h_ref[rows, :], npre_ref[1:2, :], ada_ref[3:4, :], ada_ref[4:5, :]).astype(BF16)
        a_low = _dot_nt(u, wt_ref[o_alpha:o_alpha + LANES, :]).astype(BF16)
        pg = _dot_nt(u, wt_ref[0:o_alpha, :])
        z = _dot(a_low, walpha_ref[...]) + balpha_ref[...]
        gq_ref[rows, :] = pg[:, :GLA_QK_W]
        gk_ref[rows, :] = pg[:, GLA_QK_W:2 * GLA_QK_W]
        gv_ref[rows, :] = pg[:, 2 * GLA_QK_W:2 * GLA_QK_W + GLA_V_W].astype(BF16)
        gr_ref[rows, :] = pg[:, 2 * GLA_QK_W + GLA_V_W:]
        la_ref[rows, :] = _log_sigmoid(z) / GLA_TAU
        pd = _dot_nt(u, wt_ref[o_diff:o_s5, :])
        dq_ref[rows, :] = (pd[:, :DIFF_QK_W] * (DIFF_DQK ** -0.5 * LOG2E)).astype(BF16)
        dk_ref[rows, :] = pd[:, DIFF_QK_W:2 * DIFF_QK_W].astype(BF16)
        dv_ref[rows, :] = pd[:, 2 * DIFF_QK_W:].astype(BF16)
        su_ref[rows, :] = _dot_nt(u, wt_ref[o_s5:MIX_GATE_ROW, :]).astype(BF16)


def _inproj_call(h, ada4, norm_pre, w_in_t, walpha, balpha, layer):
    bsz, seq, _ = h.shape
    tm = TOKEN_TILE

    def tok(w):
        return pl.BlockSpec((None, tm, w), lambda b, i: (b, i, 0))

    def sds(w, dt):
        return jax.ShapeDtypeStruct((bsz, seq, w), dt)

    def res(arr):
        nd = arr.ndim - 1
        return _resident((None,) + arr.shape[1:], lambda b, i: (layer,) + (0,) * nd)

    return pl.pallas_call(
        functools.partial(_inproj_kernel, layer=layer),
        grid=(bsz, seq // tm),
        in_specs=[
            tok(D_MODEL),
            pl.BlockSpec((None, None, ADA_CHUNKS, D_MODEL), lambda b, i: (layer, b, 0, 0)),
            pl.BlockSpec((None, 3, D_MODEL), lambda b, i: (layer, 0, 0)),
            pl.BlockSpec(memory_space=pl.ANY), res(walpha), res(balpha),
        ],
        out_specs=[tok(GLA_QK_W), tok(GLA_QK_W), tok(GLA_V_W), tok(GLA_V_W), tok(GLA_QK_W),
                   tok(DIFF_QK_W), tok(DIFF_QK_W), tok(DIFF_V_W), tok(S5_WIDTH)],
        out_shape=[sds(GLA_QK_W, F32), sds(GLA_QK_W, F32), sds(GLA_V_W, BF16), sds(GLA_V_W, F32),
                   sds(GLA_QK_W, F32), sds(DIFF_QK_W, BF16), sds(DIFF_QK_W, BF16), sds(DIFF_V_W, BF16),
                   sds(S5_WIDTH, BF16)],
        scratch_shapes=[pltpu.VMEM((MIX_GATE_ROW, D_MODEL), BF16),
                        pltpu.VMEM((WEIGHT_STAGE_SLOTS, MIX_STAGE_ROWS, D_MODEL), F32),
                        pltpu.SemaphoreType.DMA((WEIGHT_STAGE_SLOTS,))],
        compiler_params=_params("arbitrary", "arbitrary"),
        name="mixer_in_proj",
    )(h, ada4, norm_pre, w_in_t, walpha, balpha)


def _cumsum_rows(tri, x):
    hi = x.astype(BF16)
    r1 = x - hi.astype(F32)
    mid = r1.astype(BF16)
    lo = (r1 - mid.astype(F32)).astype(BF16)
    return _dot(tri, hi) + _dot(tri, mid) + _dot(tri, lo)


def _gla_kernel(q_ref, k_ref, v_ref, r_ref, la_ref, g_ref, o_ref, s_ref, *, n_pairs):
    @pl.when(pl.program_id(1) == 0)
    def _():
        s_ref[...] = jnp.zeros_like(s_ref)

    row = lax.broadcasted_iota(jnp.int32, (GLA_PAIR, GLA_PAIR), 0)
    col = lax.broadcasted_iota(jnp.int32, (GLA_PAIR, GLA_PAIR), 1)
    causal = (col <= row) & ((row // GLA_CHUNK) == (col // GLA_CHUNK))
    tri = jnp.where(causal, 1.0, 0.0).astype(BF16)
    qlane = lax.broadcasted_iota(jnp.int32, (GLA_PAIR, GLA_QK_W), 1)
    tlane = lax.broadcasted_iota(jnp.int32, (GLA_QK_W, GLA_PAIR), 1)
    first = tlane < GLA_CHUNK

    def decays(p):
        rows = slice(p * GLA_PAIR, (p + 1) * GLA_PAIR)
        b = _cumsum_rows(tri, la_ref[rows, :])
        b_t = b.T
        qd = (q_ref[rows, :] * (GLA_DK ** -0.5)) * jnp.exp(b)
        k_t = k_ref[rows, :].T
        bl0 = b_t[:, GLA_CHUNK - 1:GLA_CHUNK]
        bl1 = b_t[:, GLA_PAIR - 1:GLA_PAIR]
        kdec_t = k_t * jnp.exp(jnp.where(first, bl0, bl1) - b_t)
        return dict(
            qh=[jnp.where(qlane // GLA_DK == h, qd, 0.0).astype(BF16) for h in range(GLA_HEADS)],
            kd_t=(k_t * jnp.exp(-b_t)).astype(BF16),
            kdec0=jnp.where(first, kdec_t, 0.0).astype(BF16),
            kdec1=jnp.where(first, 0.0, kdec_t).astype(BF16),
            dec0=jnp.exp(bl0), dec1=jnp.exp(bl1))

    def products(p, d, s0):
        v = v_ref[p * GLA_PAIR:(p + 1) * GLA_PAIR, :]
        ds0, ds1 = [], []
        for h in range(GLA_HEADS):
            hr = slice(h * GLA_DK, (h + 1) * GLA_DK)
            vh = v[:, h * GLA_DV:(h + 1) * GLA_DV]
            ds0.append(_dot(d["kdec0"][hr, :], vh))
            ds1.append(_dot(d["kdec1"][hr, :], vh))
        att = [jnp.where(causal, _dot(qh, d["kd_t"]), 0.0).astype(BF16) for qh in d["qh"]]
        s1 = d["dec0"] * s0 + jnp.concatenate(ds0, axis=0)
        s2 = d["dec1"] * s1 + jnp.concatenate(ds1, axis=0)
        s0b = s0.astype(BF16)
        s1b = s1.astype(BF16)
        inter = [jnp.concatenate([_dot(qh[:GLA_CHUNK], s0b), _dot(qh[GLA_CHUNK:], s1b)], axis=0)
                 for qh in d["qh"]]
        return s2, (v, att, inter)

    def outputs(p, v, att, inter):
        rows = slice(p * GLA_PAIR, (p + 1) * GLA_PAIR)
        r = r_ref[rows, :]
        for h in range(GLA_HEADS):
            vc = slice(h * GLA_DV, (h + 1) * GLA_DV)
            o = _rms(_dot(att[h], v[:, vc]) + inter[h]) * g_ref[:, vc]
            o_ref[rows, vc] = (o * jax.nn.silu(r[:, vc])).astype(BF16)

    state = s_ref[...]
    dec, prod = {}, {}
    for i in range(n_pairs + 2):
        if i < n_pairs:
            dec[i] = decays(i)
        if 0 <= i - 1 < n_pairs:
            state, prod[i - 1] = products(i - 1, dec.pop(i - 1), state)
        if 0 <= i - 2 < n_pairs:
            outputs(i - 2, *prod.pop(i - 2))
    s_ref[...] = state


def _gla_call(gq, gk, gv, gr, la, gla_norm, layer):
    bsz, seq, _ = gq.shape
    tl = GLA_TILE

    def tok(w):
        return pl.BlockSpec((None, tl, w), lambda b, i: (b, i, 0))

    return pl.pallas_call(
        functools.partial(_gla_kernel, n_pairs=tl // GLA_PAIR),
        grid=(bsz, seq // tl),
        in_specs=[tok(GLA_QK_W), tok(GLA_QK_W), tok(GLA_V_W), tok(GLA_V_W), tok(GLA_QK_W),
                  pl.BlockSpec((None, 1, GLA_V_W), lambda b, i: (layer, 0, 0))],
        out_specs=tok(GLA_V_W),
        out_shape=jax.ShapeDtypeStruct((bsz, seq, GLA_V_W), BF16),
        scratch_shapes=[pltpu.VMEM((GLA_QK_W, GLA_DV), F32)],
        compiler_params=_params("arbitrary", "arbitrary"),
        name="gla_mixer",
    )(gq, gk, gv, gr, la, gla_norm.reshape(DEPTH, 1, GLA_V_W))


def _t5_bucket_tiles():
    t = ATT_TILE
    r = np.arange(t)[:, None]
    c = np.arange(t)[None, :]
    max_exact = NUM_BUCKETS // 2
    tiles = []
    for off in range(3):
        dist = off * t + r - c
        d = np.maximum(dist, 1).astype(np.float32)
        large = max_exact + (np.log(d / np.float32(max_exact)) / np.float32(math.log(MAX_DISTANCE / max_exact))
                             * np.float32(NUM_BUCKETS - max_exact)).astype(np.int32)
        large = np.minimum(large, NUM_BUCKETS - 1)
        bucket = np.where(dist < max_exact, dist, large)
        tiles.append(np.where(dist >= 0, bucket, -1).T)
    far = tiles[2]
    assert (far == NUM_BUCKETS - 1).all()
    return np.stack(tiles).astype(np.int32)


def _bias_kernel(rel_ref, bucket_ref, o_ref):
    m = pl.program_id(0)
    bucket = bucket_ref[0:2]
    acc = jnp.full(bucket.shape, -jnp.inf, F32)
    for kk in range(NUM_BUCKETS):
        acc = jnp.where(bucket == kk, rel_ref[kk, m], acc)
    o_ref[0:2] = acc * LOG2E
    o_ref[2] = jnp.full(o_ref.shape[1:], rel_ref[NUM_BUCKETS - 1, m], F32) * LOG2E


def _bias_call(rel_bias):
    n_maps = 2 * DIFF_HEADS
    t = ATT_TILE
    return pl.pallas_call(
        _bias_kernel,
        grid=(n_maps,),
        in_specs=[pl.BlockSpec(memory_space=pltpu.SMEM),
                  pl.BlockSpec((3, t, t), lambda m: (0, 0, 0))],
        out_specs=pl.BlockSpec((None, 3, t, t), lambda m: (m, 0, 0, 0)),
        out_shape=jax.ShapeDtypeStruct((n_maps, 3, t, t), F32),
        compiler_params=_params("arbitrary"),
        name="t5_bias_tiles",
    )(rel_bias, jnp.asarray(_t5_bucket_tiles()))


def _diff_kernel(q_ref, k_ref, v_ref, bias_ref, lam_ref, g_ref, o_ref, *, lam_init, n_tiles):
    t = ATT_TILE
    lv = lam_ref[...]
    lam = (jnp.exp(jnp.sum(lv[0:1] * lv[1:2], axis=-1, keepdims=True))
           - jnp.exp(jnp.sum(lv[2:3] * lv[3:4], axis=-1, keepdims=True)) + lam_init)
    v_t = jnp.concatenate([v_ref[...].astype(F32).T, jnp.ones((ONES_ROWS, v_ref.shape[0]), F32)],
                          axis=0).astype(BF16)
    lane = lax.broadcasted_iota(jnp.int32, (t, 2 * DIFF_DQK), 1)
    far = [bias_ref[mp, 2, 0:1, 0:1] for mp in range(2)]

    tasks = [(qi, kj, mp) for qi in range(n_tiles) for kj in range(qi + 1) for mp in range(2)]
    masked_q = {}
    state = {}
    done = {}

    def logits(qi, kj, mp):
        if qi not in masked_q:
            q = q_ref[qi * t:(qi + 1) * t, :]
            zero = jnp.zeros_like(q)
            masked_q[qi] = (jnp.where(lane < DIFF_DQK, q, zero), jnp.where(lane < DIFF_DQK, zero, q))
        s = _dot_nt(k_ref[kj * t:(kj + 1) * t, :], masked_q[qi][mp])
        return s + bias_ref[mp, qi - kj] if qi - kj < 2 else s

    def accumulate(qi, kj, mp, s):
        shift = 0.0 if qi - kj < 2 else far[mp]
        m_blk = jnp.max(s, axis=0, keepdims=True) + shift
        prev = state.get((qi, mp))
        m_new = m_blk if prev is None else jnp.maximum(prev[0], m_blk)
        p = jnp.exp2((s - (m_new - shift)).astype(BF16))
        pv = _dot(v_t[:, kj * t:(kj + 1) * t], p)
        if prev is None:
            state[(qi, mp)] = (m_new, pv)
        else:
            state[(qi, mp)] = (m_new, jnp.exp2(prev[0] - m_new) * prev[1] + pv)
        if kj == qi:
            _, acc = state.pop((qi, mp))
            done.setdefault(qi, {})[mp] = acc[:DIFF_DV] / acc[DIFF_DV:DIFF_DV + 1]
            if len(done[qi]) == 2:
                o = done[qi][0] - lam * done[qi][1]
                o = o * lax.rsqrt(jnp.mean(o * o, axis=0, keepdims=True) + EPS)
                o_ref[qi * t:(qi + 1) * t, :] = (o.T * g_ref[...] * (1.0 - lam_init)).astype(BF16)
                del done[qi]

    pending = []
    for task in tasks:
        pending.append((task, logits(*task)))
        if len(pending) > ATT_LOOKAHEAD:
            ready, s = pending.pop(0)
            accumulate(*ready, s)
    for ready, s in pending:
        accumulate(*ready, s)


def _diff_call(dq, dk, dv, bias, diff_lambda, diff_norm, layer):
    bsz, seq, _ = dq.shape
    t = ATT_TILE
    lam_init = 0.8 - 0.6 * math.exp(-0.3 * layer)
    return pl.pallas_call(
        functools.partial(_diff_kernel, lam_init=lam_init, n_tiles=seq // t),
        grid=(bsz, DIFF_HEADS),
        in_specs=[
            pl.BlockSpec((None, seq, 2 * DIFF_DQK), lambda b, h: (b, 0, h)),
            pl.BlockSpec((None, seq, 2 * DIFF_DQK), lambda b, h: (b, 0, h)),
            pl.BlockSpec((None, seq, DIFF_DV), lambda b, h: (b, 0, h)),
            pl.BlockSpec((2, 3, t, t), lambda b, h: (h, 0, 0, 0)),
            pl.BlockSpec((None, 4, DIFF_DQK), lambda b, h: (layer, 0, 0)),
            pl.BlockSpec((None, 1, DIFF_DV), lambda b, h: (layer, 0, h)),
        ],
        out_specs=pl.BlockSpec((None, seq, DIFF_DV), lambda b, h: (b, 0, h)),
        out_shape=jax.ShapeDtypeStruct((bsz, seq, DIFF_V_W), BF16),
        compiler_params=_params("arbitrary", "arbitrary"),
        name="diff_attention",
    )(dq, dk, dv, bias, diff_lambda, diff_norm.reshape(DEPTH, 1, DIFF_V_W))


def _s5_prep_kernel(lre_ref, lim_ref, lstep_ref, bre_ref, bim_ref, cre_ref, cim_ref,
                    are_ref, aim_ref, bmat_ref, cmat_ref):
    lr = lre_ref[...]
    li = lim_ref[...]
    dt = jnp.exp(lstep_ref[...])
    mag = jnp.exp(lr * dt)
    a_re = mag * jnp.cos(li * dt)
    a_im = mag * jnp.sin(li * dt)
    are_ref[...] = jnp.broadcast_to(a_re, are_ref.shape)
    aim_ref[...] = jnp.broadcast_to(a_im, aim_ref.shape)
    nr, ni = a_re - 1.0, a_im
    den = lr * lr + li * li
    f_re = (nr * lr + ni * li) / den
    f_im = (ni * lr - nr * li) / den
    ns, ch = S5_BLOCK_STATE, S5_BLOCK_CH
    gpb = ns // S5_STATE
    in_own = (lax.broadcasted_iota(jnp.int32, (ch, ns), 0) // S5_GROUP_CH
              == lax.broadcasted_iota(jnp.int32, (ch, ns), 1) // S5_STATE)
    out_own = (lax.broadcasted_iota(jnp.int32, (ns, ch), 0) // S5_STATE
               == lax.broadcasted_iota(jnp.int32, (ns, ch), 1) // S5_GROUP_CH)

    def in_block(ref, gb):
        return jnp.where(in_own, jnp.concatenate([ref[gb]] * gpb, axis=1), 0.0)

    def out_block(ref, gb):
        return jnp.where(out_own, jnp.concatenate([ref[gb]] * gpb, axis=0), 0.0)

    for gb in range(S5_GROUP_BLOCKS):
        fr = f_re[:, gb * ns:(gb + 1) * ns]
        fi = f_im[:, gb * ns:(gb + 1) * ns]
        br = in_block(bre_ref, gb)
        bi = in_block(bim_ref, gb)
        bmat_ref[gb, :, :ns] = (fr * br - fi * bi).astype(BF16)
        bmat_ref[gb, :, ns:] = (fr * bi + fi * br).astype(BF16)
        cmat_ref[gb, :ns, :] = out_block(cre_ref, gb).astype(BF16)
        cmat_ref[gb, ns:, :] = (-out_block(cim_ref, gb)).astype(BF16)


def _s5_prep_call(lam_re, lam_im, log_step, b_re, b_im, c_re, c_im, bsz):
    d = DEPTH
    nb, ch, ns = S5_GROUP_BLOCKS, S5_BLOCK_CH, S5_BLOCK_STATE
    gpb = S5_GROUPS // nb

    def in_blocks(b):
        return b.reshape(d, nb, gpb, S5_STATE, S5_GROUP_CH).transpose(0, 1, 2, 4, 3).reshape(d, nb, ch, S5_STATE)

    def out_blocks(c):
        return c.reshape(d, nb, gpb, S5_GROUP_CH, S5_STATE).transpose(0, 1, 4, 2, 3).reshape(d, nb, S5_STATE, ch)

    def row(x):
        return x.reshape(d, 1, S5_NSTATE)

    lstep = jnp.broadcast_to(log_step[:, :, None], (d, S5_GROUPS, S5_STATE))

    def full(shape):
        nd = len(shape)
        return pl.BlockSpec((None,) + shape, lambda l: (l,) + (0,) * nd)

    return pl.pallas_call(
        _s5_prep_kernel,
        grid=(d,),
        in_specs=[full((1, S5_NSTATE))] * 3 + [full((nb, ch, S5_STATE))] * 2 + [full((nb, S5_STATE, ch))] * 2,
        out_specs=[full((bsz, S5_NSTATE)), full((bsz, S5_NSTATE)),
                   full((nb, ch, 2 * ns)), full((nb, 2 * ns, ch))],
        out_shape=[jax.ShapeDtypeStruct((d, bsz, S5_NSTATE), F32), jax.ShapeDtypeStruct((d, bsz, S5_NSTATE), F32),
                   jax.ShapeDtypeStruct((d, nb, ch, 2 * ns), BF16), jax.ShapeDtypeStruct((d, nb, 2 * ns, ch), BF16)],
        compiler_params=_params("arbitrary"),
        name="s5_discretize",
    )(row(lam_re), row(lam_im), row(lstep), in_blocks(b_re), in_blocks(b_im), out_blocks(c_re), out_blocks(c_im))


def _s5_kernel(u_ref, are_ref, aim_ref, bmat_ref, cmat_ref, d_ref, wglu_ref, bglu_ref, o_ref,
               tb_ref, xr_ref, xi_ref, sr_ref, si_ref, *, bsz, n_steps):
    @pl.when(pl.program_id(0) == 0)
    def _():
        sr_ref[...] = jnp.zeros_like(sr_ref)
        si_ref[...] = jnp.zeros_like(si_ref)

    ns, ch, nb = S5_BLOCK_STATE, S5_BLOCK_CH, S5_GROUP_BLOCKS
    for b in range(bsz):
        ub = u_ref[b].astype(F32)
        for gb in range(nb):
            tb_ref[gb, pl.ds(b, n_steps, stride=bsz), :] = ub[:, gb * ch:(gb + 1) * ch]
    u = [tb_ref[gb] for gb in range(nb)]

    def drive(gb):
        return _dot(u[gb].astype(BF16), bmat_ref[gb])

    def scan(gb, bu):
        cols = slice(gb * ns, (gb + 1) * ns)
        ar, ai = are_ref[:, cols], aim_ref[:, cols]
        pr, pi = sr_ref[:, cols], si_ref[:, cols]
        for t in range(n_steps):
            rows = slice(t * bsz, (t + 1) * bsz)
            pr, pi = ar * pr - ai * pi + bu[rows, :ns], ar * pi + ai * pr + bu[rows, ns:]
            xr_ref[rows, cols] = pr
            xi_ref[rows, cols] = pi
        sr_ref[:, cols] = pr
        si_ref[:, cols] = pi

    def readout(gb):
        cols = slice(gb * ns, (gb + 1) * ns)
        return (_dot(xr_ref[:, cols].astype(BF16), cmat_ref[gb, :ns, :])
                + _dot(xi_ref[:, cols].astype(BF16), cmat_ref[gb, ns:, :]))

    bu, ys = {0: drive(0)}, []
    for gb in range(nb):
        if gb + 1 < nb:
            bu[gb + 1] = drive(gb + 1)
        scan(gb, bu.pop(gb))
        if gb >= 1:
            ys.append(readout(gb - 1))
    ys.append(readout(nb - 1))

    y = jax.nn.gelu(jnp.concatenate(ys, axis=-1) + d_ref[...] * jnp.concatenate(u, axis=-1))
    z = _dot(y.astype(BF16), wglu_ref[...]) + bglu_ref[...]
    out = z[:, :S5_WIDTH] * jax.nn.sigmoid(z[:, S5_WIDTH:])
    for gb in range(nb):
        tb_ref[gb] = out[:, gb * ch:(gb + 1) * ch]
    for b in range(bsz):
        for gb in range(nb):
            o_ref[b, :, gb * ch:(gb + 1) * ch] = tb_ref[gb, pl.ds(b, n_steps, stride=bsz), :].astype(BF16)


def _s5_call(su, a_re, a_im, bmat, cmat, d_skip, wglu, bglu, layer):
    bsz, seq, _ = su.shape
    tl = S5_TIME_TILE
    tr = tl * bsz
    nb, ch, ns = S5_GROUP_BLOCKS, S5_BLOCK_CH, S5_BLOCK_STATE

    def res(shape):
        nd = len(shape)
        return _resident((None,) + shape, lambda i: (layer,) + (0,) * nd)

    tok = pl.BlockSpec((bsz, tl, S5_WIDTH), lambda i: (0, i, 0))
    return pl.pallas_call(
        functools.partial(_s5_kernel, bsz=bsz, n_steps=tl),
        grid=(seq // tl,),
        in_specs=[tok,
                  res((bsz, S5_NSTATE)), res((bsz, S5_NSTATE)),
                  res((nb, ch, 2 * ns)), res((nb, 2 * ns, ch)),
                  res((1, S5_WIDTH)), res((S5_WIDTH, 2 * S5_WIDTH)), res((1, 2 * S5_WIDTH))],
        out_specs=tok,
        out_shape=jax.ShapeDtypeStruct((bsz, seq, S5_WIDTH), BF16),
        scratch_shapes=[pltpu.VMEM((nb, tr, ch), F32),
                        pltpu.VMEM((tr, S5_NSTATE), F32), pltpu.VMEM((tr, S5_NSTATE), F32),
                        pltpu.VMEM((bsz, S5_NSTATE), F32), pltpu.VMEM((bsz, S5_NSTATE), F32)],
        compiler_params=_params("arbitrary"),
        name="s5_mixer",
    )(su, a_re, a_im, bmat, cmat, d_skip, wglu, bglu)


def _merge_kernel(h_ref, ada_ref, npre_ref, npost_ref, yg_ref, yd_ref, ys_ref, wt_hbm, wbr_hbm, wout_hbm, o_ref,
                  wgate_ref, wbr_ref, wout_ref, stage, sem, *, layer):
    @pl.when((pl.program_id(0) == 0) & (pl.program_id(1) == 0))
    def _():
        _stream_cast(wt_hbm.at[layer, MIX_GATE_ROW:MIX_GATE_ROW + N_BRANCH * D_MODEL], wgate_ref, stage, sem,
                     axis=0, chunk=FF_CHUNK)
        _stream_cast(wbr_hbm.at[layer], wbr_ref, stage, sem, axis=0, chunk=FF_CHUNK)
        _stream_cast(wout_hbm.at[layer], wout_ref, stage, sem, axis=0, chunk=FF_CHUNK)

    blocks = [slice(r * ROW_BLOCK, (r + 1) * ROW_BLOCK) for r in range(h_ref.shape[0] // ROW_BLOCK)]
    us = [_prenorm_mod(h_ref[rows, :], npre_ref[1:2, :], ada_ref[3:4, :], ada_ref[4:5, :]).astype(BF16)
          for rows in blocks]
    merged = [None] * len(blocks)
    for i, y_ref in enumerate((yg_ref, yd_ref, ys_ref)):
        for r, rows in enumerate(blocks):
            gate = jax.nn.sigmoid(_dot_nt(us[r], wgate_ref[i * D_MODEL:(i + 1) * D_MODEL, :]))
            term = gate * _dot(y_ref[rows, :], wbr_ref[i * S5_WIDTH:(i + 1) * S5_WIDTH, :])
            merged[r] = term if merged[r] is None else merged[r] + term
    ys = [_dot(m.astype(BF16), wout_ref[...]) for m in merged]
    for rows, y in zip(blocks, ys):
        o_ref[rows, :] = h_ref[rows, :] + ada_ref[5:6, :] * (_rms(y) * npost_ref[1:2, :])


def _merge_call(h, ada4, norm_pre, norm_post, y_gla, y_diff, y_s5, w_in_t, wbr, wout, layer):
    bsz, seq, _ = h.shape
    tm = TOKEN_TILE

    def tok(w):
        return pl.BlockSpec((None, tm, w), lambda b, i: (b, i, 0))

    return pl.pallas_call(
        functools.partial(_merge_kernel, layer=layer),
        grid=(bsz, seq // tm),
        in_specs=[
            tok(D_MODEL),
            pl.BlockSpec((None, None, ADA_CHUNKS, D_MODEL), lambda b, i: (layer, b, 0, 0)),
            pl.BlockSpec((None, 3, D_MODEL), lambda b, i: (layer, 0, 0)),
            pl.BlockSpec((None, 3, D_MODEL), lambda b, i: (layer, 0, 0)),
            tok(GLA_V_W), tok(DIFF_V_W), tok(S5_WIDTH),
            pl.BlockSpec(memory_space=pl.ANY),
            pl.BlockSpec(memory_space=pl.ANY),
            pl.BlockSpec(memory_space=pl.ANY),
        ],
        out_specs=tok(D_MODEL),
        out_shape=jax.ShapeDtypeStruct(h.shape, F32),
        scratch_shapes=[pltpu.VMEM((N_BRANCH * D_MODEL, D_MODEL), BF16),
                        pltpu.VMEM((N_BRANCH * S5_WIDTH, D_MODEL), BF16), pltpu.VMEM((D_MODEL, D_MODEL), BF16),
                        pltpu.VMEM((WEIGHT_STAGE_SLOTS, FF_CHUNK, D_MODEL), F32),
                        pltpu.SemaphoreType.DMA((WEIGHT_STAGE_SLOTS,))],
        compiler_params=_params("arbitrary", "arbitrary"),
        name="mixer_merge",
    )(h, ada4, norm_pre, norm_post, y_gla, y_diff, y_s5, w_in_t, wbr, wout)


def kernel(x, c, rel_bias, w_ada, b_ada, norm_pre, norm_post, ffn1_w_up, ffn1_w_down, ffn2_w_up, ffn2_w_down,
           w_in, gla_w_alpha, gla_b_alpha, gla_norm, diff_lambda, diff_norm, s5_lam_re, s5_lam_im, s5_log_step,
           s5_b_re, s5_b_im, s5_c_re, s5_c_im, s5_d, s5_w_glu, s5_b_glu, w_branch, w_out):
    bsz, seq, _ = x.shape
    assert bsz == SUBLANES, "the S5 scan keeps the batch on the sublane axis"

    f1 = (ffn1_w_up, ffn1_w_down)
    f2 = (ffn2_w_up, ffn2_w_down)
    w_in_t = jnp.swapaxes(w_in, 1, 2)
    walpha = jnp.pad(gla_w_alpha.astype(BF16), ((0, 0), (0, LANES - GLA_LOWRANK), (0, 0)))
    balpha = gla_b_alpha.reshape(DEPTH, 1, GLA_QK_W)
    wbr = w_branch.reshape(DEPTH, N_BRANCH * S5_WIDTH, D_MODEL)
    wout = w_out
    wglu = s5_w_glu.astype(BF16)
    bglu = s5_b_glu.reshape(DEPTH, 1, 2 * S5_WIDTH)
    d_skip = s5_d.reshape(DEPTH, 1, S5_WIDTH)

    ada4 = _ada_call(c, w_ada, b_ada).reshape(DEPTH, bsz, ADA_CHUNKS, D_MODEL)
    bias = _bias_call(rel_bias)
    a_re, a_im, bmat, cmat = _s5_prep_call(s5_lam_re, s5_lam_im, s5_log_step, s5_b_re, s5_b_im,
                                           s5_c_re, s5_c_im, bsz)

    h = x
    for layer in range(DEPTH):
        h = _ffn_call(h, ada4, norm_pre, norm_post, *f1, layer, 0)
        gq, gk, gv, gr, la, dq, dk, dv, su = _inproj_call(h, ada4, norm_pre, w_in_t, walpha, balpha, layer)
        y_gla = _gla_call(gq, gk, gv, gr, la, gla_norm, layer)
        y_diff = _diff_call(dq, dk, dv, bias, diff_lambda, diff_norm, layer)
        y_s5 = _s5_call(su, a_re, a_im, bmat, cmat, d_skip, wglu, bglu, layer)
        h = _merge_call(h, ada4, norm_pre, norm_post, y_gla, y_diff, y_s5, w_in_t, wbr, wout, layer)
        h = _ffn_call(h, ada4, norm_pre, norm_post, *f2, layer, 2)
    return h
```

```python
import functools
import math

import numpy as np
import jax
import jax.numpy as jnp
from jax import lax
from jax.experimental import pallas as pl
from jax.experimental.pallas import tpu as pltpu

F32 = jnp.float32
BF16 = jnp.bfloat16

D_MODEL = 1024
DEPTH = 2
D_FF = 2816
ADA_CHUNKS = 9
EPS = 1e-6
LOG2E = math.log2(math.e)

GLA_HEADS = 4
GLA_DK = 64
GLA_DV = 128
GLA_LOWRANK = 16
GLA_TAU = 16.0
GLA_CHUNK = 64
GLA_QK_W = GLA_HEADS * GLA_DK
GLA_V_W = GLA_HEADS * GLA_DV

DIFF_HEADS = 4
DIFF_DQK = 64
DIFF_DV = 128
DIFF_QK_W = DIFF_HEADS * 2 * DIFF_DQK
DIFF_V_W = DIFF_HEADS * DIFF_DV
NUM_BUCKETS = 32
MAX_DISTANCE = 128

S5_GROUPS = 32
S5_GROUP_CH = 16
S5_STATE = 64
S5_WIDTH = S5_GROUPS * S5_GROUP_CH
S5_NSTATE = S5_GROUPS * S5_STATE
S5_GROUP_BLOCKS = 4
S5_BLOCK_CH = S5_WIDTH // S5_GROUP_BLOCKS
S5_BLOCK_STATE = S5_NSTATE // S5_GROUP_BLOCKS

N_BRANCH = 3

SUBLANES = 8
LANES = 128
ONES_ROWS = 2 * SUBLANES
VMEM_LIMIT_BYTES = 56 * 1024 * 1024

TOKEN_TILE = 1024
ROW_BLOCK = 512
FF_CHUNK = 256
FF_NCHUNK = D_FF // FF_CHUNK
WEIGHT_STAGE_COLS = 512
WEIGHT_STAGE_SLOTS = 4
MIX_GATE_ROW = 2 * GLA_QK_W + 2 * GLA_V_W + GLA_LOWRANK + 2 * DIFF_QK_W + DIFF_V_W + S5_WIDTH
MIX_STAGE_ROWS = 720
MERGE_STAGE_ROWS = 512
FFN_LOOKAHEAD = 1
GLA_PAIR = 2 * GLA_CHUNK
GLA_TILE = 1024
ATT_TILE = 256
ATT_LOOKAHEAD = 4
S5_TIME_TILE = 64
ADA_COL_TILE = 1536


def _dot(a, b):
    return jnp.dot(a, b, preferred_element_type=F32)


def _dot_nt(a, b):
    return lax.dot_general(a, b, (((1,), (1,)), ((), ())), preferred_element_type=F32)


def _rms(x):
    return x * lax.rsqrt(jnp.mean(x * x, axis=-1, keepdims=True) + EPS)


def _prenorm_mod(h, g, shift, scale):
    return (_rms(h) * g) * (1.0 + scale) + shift


def _params(*sem):
    return pltpu.CompilerParams(dimension_semantics=sem, vmem_limit_bytes=VMEM_LIMIT_BYTES)


def _resident(shape, index_map):
    return pl.BlockSpec(shape, index_map, pipeline_mode=pl.Buffered(1))


def _ada_kernel(c_ref, w_ref, b_ref, o_ref):
    cond = jax.nn.silu(c_ref[...]).astype(BF16)
    o_ref[...] = _dot(cond, w_ref[...].astype(BF16)) + b_ref[...]


def _ada_call(c, w_ada, b_ada):
    bsz = c.shape[0]
    n = ADA_CHUNKS * D_MODEL
    return pl.pallas_call(
        _ada_kernel,
        grid=(DEPTH, n // ADA_COL_TILE),
        in_specs=[
            pl.BlockSpec((bsz, D_MODEL), lambda l, j: (0, 0)),
            pl.BlockSpec((None, D_MODEL, ADA_COL_TILE), lambda l, j: (l, 0, j)),
            pl.BlockSpec((None, 1, ADA_COL_TILE), lambda l, j: (l, 0, j)),
        ],
        out_specs=pl.BlockSpec((None, bsz, ADA_COL_TILE), lambda l, j: (l, 0, j)),
        out_shape=jax.ShapeDtypeStruct((DEPTH, bsz, n), F32),
        compiler_params=_params("arbitrary", "arbitrary"),
        name="ada_proj",
    )(c, w_ada, b_ada.reshape(DEPTH, 1, n))


def _stream_cast(src, dst, stage, sem, *, axis, chunk):
    slots = stage.shape[0]

    def window(c):
        part = slice(c * chunk, (c + 1) * chunk)
        return (slice(None), part) if axis == 1 else (part, slice(None))

    def copy(c):
        return pltpu.make_async_copy(src.at[window(c)], stage.at[c % slots], sem.at[c % slots])

    n = src.shape[axis] // chunk
    for c in range(min(slots - 1, n)):
        copy(c).start()
    for c in range(n):
        if c + slots - 1 < n:
            copy(c + slots - 1).start()
        copy(c).wait()
        dst[window(c)] = stage[c % slots].astype(BF16)


def _ffn_kernel(h_ref, ada_ref, npre_ref, npost_ref, wup_hbm, wdn_hbm, o_ref,
                wup_ref, wdn_ref, up_stage, dn_stage, up_sem, dn_sem, *, sub, layer):
    a0 = 3 * sub

    @pl.when((pl.program_id(0) == 0) & (pl.program_id(1) == 0))
    def _():
        _stream_cast(wup_hbm.at[layer], wup_ref, up_stage, up_sem, axis=1, chunk=WEIGHT_STAGE_COLS)
        _stream_cast(wdn_hbm.at[layer], wdn_ref, dn_stage, dn_sem, axis=0, chunk=FF_CHUNK)

    def down(j, gate, up):
        mid = (jax.nn.silu(gate) * up).astype(BF16)
        return _dot(mid, wdn_ref[j * FF_CHUNK:(j + 1) * FF_CHUNK, :])

    for r in range(h_ref.shape[0] // ROW_BLOCK):
        rows = slice(r * ROW_BLOCK, (r + 1) * ROW_BLOCK)
        h = h_ref[rows, :]
        u = _prenorm_mod(h, npre_ref[sub:sub + 1, :], ada_ref[a0:a0 + 1, :], ada_ref[a0 + 1:a0 + 2, :])
        u = u.astype(BF16)

        def gate_up(j, u=u):
            lo = j * FF_CHUNK
            return _dot(u, wup_ref[:, lo:lo + FF_CHUNK]), _dot(u, wup_ref[:, D_FF + lo:D_FF + lo + FF_CHUNK])

        acc = None
        pending = []
        for j in range(FF_NCHUNK + FFN_LOOKAHEAD):
            if j < FF_NCHUNK:
                pending.append((j,) + gate_up(j))
            if j >= FFN_LOOKAHEAD:
                d = down(*pending.pop(0))
                acc = d if acc is None else acc + d
        post = _rms(acc) * npost_ref[sub:sub + 1, :]
        o_ref[rows, :] = h + (0.5 * ada_ref[a0 + 2:a0 + 3, :]) * post


def _ffn_call(h, ada4, norm_pre, norm_post, wup, wdn, layer, sub):
    bsz, seq, _ = h.shape
    tm = TOKEN_TILE
    tok = pl.BlockSpec((None, tm, D_MODEL), lambda b, i: (b, i, 0))
    return pl.pallas_call(
        functools.partial(_ffn_kernel, sub=sub, layer=layer),
        grid=(bsz, seq // tm),
        in_specs=[
            tok,
            pl.BlockSpec((None, None, ADA_CHUNKS, D_MODEL), lambda b, i: (layer, b, 0, 0)),
            pl.BlockSpec((None, 3, D_MODEL), lambda b, i: (layer, 0, 0)),
            pl.BlockSpec((None, 3, D_MODEL), lambda b, i: (layer, 0, 0)),
            pl.BlockSpec(memory_space=pl.ANY),
            pl.BlockSpec(memory_space=pl.ANY),
        ],
        out_specs=tok,
        out_shape=jax.ShapeDtypeStruct(h.shape, F32),
        scratch_shapes=[pltpu.VMEM((D_MODEL, 2 * D_FF), BF16), pltpu.VMEM((D_FF, D_MODEL), BF16),
                        pltpu.VMEM((WEIGHT_STAGE_SLOTS, D_MODEL, WEIGHT_STAGE_COLS), F32),
                        pltpu.VMEM((WEIGHT_STAGE_SLOTS, FF_CHUNK, D_MODEL), F32),
                        pltpu.SemaphoreType.DMA((WEIGHT_STAGE_SLOTS,)),
                        pltpu.SemaphoreType.DMA((WEIGHT_STAGE_SLOTS,))],
        compiler_params=_params("arbitrary", "arbitrary"),
        name=f"ffn{sub // 2 + 1}",
    )(h, ada4, norm_pre, norm_post, wup, wdn)


def _log_sigmoid(z):
    return jnp.minimum(z, 0.0) - jnp.log1p(jnp.exp(-jnp.abs(z)))


def _inproj_kernel(h_ref, ada_ref, npre_ref, wt_hbm, walpha_ref, balpha_ref,
                   gq_ref, gk_ref, gv_ref, gr_ref, la_ref, dq_ref, dk_ref, dv_ref, su_ref,
                   wt_ref, stage, sem, *, layer):
    @pl.when((pl.program_id(0) == 0) & (pl.program_id(1) == 0))
    def _():
        _stream_cast(wt_hbm.at[layer, 0:MIX_GATE_ROW], wt_ref, stage, sem, axis=0, chunk=MIX_STAGE_ROWS)

    o_alpha = 2 * GLA_QK_W + 2 * GLA_V_W
    o_diff = o_alpha + GLA_LOWRANK
    o_s5 = o_diff + 2 * DIFF_QK_W + DIFF_V_W
    for r in range(h_ref.shape[0] // ROW_BLOCK):
        rows = slice(r * ROW_BLOCK, (r + 1) * ROW_BLOCK)
        u = _prenorm_mod(h_ref[rows, :], npre_ref[1:2, :], ada_ref[3:4, :], ada_ref[4:5, :]).astype(BF16)
        a_low = _dot_nt(u, wt_ref[o_alpha:o_alpha + LANES, :]).astype(BF16)
        pg = _dot_nt(u, wt_ref[0:o_alpha, :])
        z = _dot(a_low, walpha_ref[...]) + balpha_ref[...]
        gq_ref[rows, :] = pg[:, :GLA_QK_W]
        gk_ref[rows, :] = pg[:, GLA_QK_W:2 * GLA_QK_W]
        gv_ref[rows, :] = pg[:, 2 * GLA_QK_W:2 * GLA_QK_W + GLA_V_W].astype(BF16)
        gr_ref[rows, :] = pg[:, 2 * GLA_QK_W + GLA_V_W:]
        la_ref[rows, :] = _log_sigmoid(z) / GLA_TAU
        pd = _dot_nt(u, wt_ref[o_diff:o_s5, :])
        dq_ref[rows, :] = (pd[:, :DIFF_QK_W] * (DIFF_DQK ** -0.5 * LOG2E)).astype(BF16)
        dk_ref[rows, :] = pd[:, DIFF_QK_W:2 * DIFF_QK_W].astype(BF16)
        dv_ref[rows, :] = pd[:, 2 * DIFF_QK_W:].astype(BF16)
        su_ref[rows, :] = _dot_nt(u, wt_ref[o_s5:MIX_GATE_ROW, :]).astype(BF16)


def _inproj_call(h, ada4, norm_pre, w_in_t, walpha, balpha, layer):
    bsz, seq, _ = h.shape
    tm = TOKEN_TILE

    def tok(w):
        return pl.BlockSpec((None, tm, w), lambda b, i: (b, i, 0))

    def sds(w, dt):
        return jax.ShapeDtypeStruct((bsz, seq, w), dt)

    def res(arr):
        nd = arr.ndim - 1
        return _resident((None,) + arr.shape[1:], lambda b, i: (layer,) + (0,) * nd)

    return pl.pallas_call(
        functools.partial(_inproj_kernel, layer=layer),
        grid=(bsz, seq // tm),
        in_specs=[
            tok(D_MODEL),
            pl.BlockSpec((None, None, ADA_CHUNKS, D_MODEL), lambda b, i: (layer, b, 0, 0)),
            pl.BlockSpec((None, 3, D_MODEL), lambda b, i: (layer, 0, 0)),
            pl.BlockSpec(memory_space=pl.ANY), res(walpha), res(balpha),
        ],
        out_specs=[tok(GLA_QK_W), tok(GLA_QK_W), tok(GLA_V_W), tok(GLA_V_W), tok(GLA_QK_W),
                   tok(DIFF_QK_W), tok(DIFF_QK_W), tok(DIFF_V_W), tok(S5_WIDTH)],
        out_shape=[sds(GLA_QK_W, F32), sds(GLA_QK_W, F32), sds(GLA_V_W, BF16), sds(GLA_V_W, F32),
                   sds(GLA_QK_W, F32), sds(DIFF_QK_W, BF16), sds(DIFF_QK_W, BF16), sds(DIFF_V_W, BF16),
                   sds(S5_WIDTH, BF16)],
        scratch_shapes=[pltpu.VMEM((MIX_GATE_ROW, D_MODEL), BF16),
                        pltpu.VMEM((WEIGHT_STAGE_SLOTS, MIX_STAGE_ROWS, D_MODEL), F32),
                        pltpu.SemaphoreType.DMA((WEIGHT_STAGE_SLOTS,))],
        compiler_params=_params("arbitrary", "arbitrary"),
        name="mixer_in_proj",
    )(h, ada4, norm_pre, w_in_t, walpha, balpha)


def _cumsum_rows(tri, x):
    hi = x.astype(BF16)
    r1 = x - hi.astype(F32)
    mid = r1.astype(BF16)
    lo = (r1 - mid.astype(F32)).astype(BF16)
    return _dot(tri, hi) + _dot(tri, mid) + _dot(tri, lo)


def _gla_kernel(q_ref, k_ref, v_ref, r_ref, la_ref, g_ref, o_ref, s_ref, *, n_pairs):
    @pl.when(pl.program_id(1) == 0)
    def _():
        s_ref[...] = jnp.zeros_like(s_ref)

    row = lax.broadcasted_iota(jnp.int32, (GLA_PAIR, GLA_PAIR), 0)
    col = lax.broadcasted_iota(jnp.int32, (GLA_PAIR, GLA_PAIR), 1)
    causal = (col <= row) & ((row // GLA_CHUNK) == (col // GLA_CHUNK))
    tri = jnp.where(causal, 1.0, 0.0).astype(BF16)
    qlane = lax.broadcasted_iota(jnp.int32, (GLA_PAIR, GLA_QK_W), 1)
    tlane = lax.broadcasted_iota(jnp.int32, (GLA_QK_W, GLA_PAIR), 1)
    first = tlane < GLA_CHUNK

    def decays(p):
        rows = slice(p * GLA_PAIR, (p + 1) * GLA_PAIR)
        b = _cumsum_rows(tri, la_ref[rows, :])
        b_t = b.T
        qd = (q_ref[rows, :] * (GLA_DK ** -0.5)) * jnp.exp(b)
        k_t = k_ref[rows, :].T
        bl0 = b_t[:, GLA_CHUNK - 1:GLA_CHUNK]
        bl1 = b_t[:, GLA_PAIR - 1:GLA_PAIR]
        kdec_t = k_t * jnp.exp(jnp.where(first, bl0, bl1) - b_t)
        return dict(
            qh=[jnp.where(qlane // GLA_DK == h, qd, 0.0).astype(BF16) for h in range(GLA_HEADS)],
            kd_t=(k_t * jnp.exp(-b_t)).astype(BF16),
            kdec0=jnp.where(first, kdec_t, 0.0).astype(BF16),
            kdec1=jnp.where(first, 0.0, kdec_t).astype(BF16),
            dec0=jnp.exp(bl0), dec1=jnp.exp(bl1))

    def products(p, d, s0):
        v = v_ref[p * GLA_PAIR:(p + 1) * GLA_PAIR, :]
        ds0, ds1 = [], []
        for h in range(GLA_HEADS):
            hr = slice(h * GLA_DK, (h + 1) * GLA_DK)
            vh = v[:, h * GLA_DV:(h + 1) * GLA_DV]
            ds0.append(_dot(d["kdec0"][hr, :], vh))
            ds1.append(_dot(d["kdec1"][hr, :], vh))
        att = [jnp.where(causal, _dot(qh, d["kd_t"]), 0.0).astype(BF16) for qh in d["qh"]]
        s1 = d["dec0"] * s0 + jnp.concatenate(ds0, axis=0)
        s2 = d["dec1"] * s1 + jnp.concatenate(ds1, axis=0)
        s0b = s0.astype(BF16)
        s1b = s1.astype(BF16)
        inter = [jnp.concatenate([_dot(qh[:GLA_CHUNK], s0b), _dot(qh[GLA_CHUNK:], s1b)], axis=0)
                 for qh in d["qh"]]
        return s2, (v, att, inter)

    def outputs(p, v, att, inter):
        rows = slice(p * GLA_PAIR, (p + 1) * GLA_PAIR)
        r = r_ref[rows, :]
        for h in range(GLA_HEADS):
            vc = slice(h * GLA_DV, (h + 1) * GLA_DV)
            o = _rms(_dot(att[h], v[:, vc]) + inter[h]) * g_ref[:, vc]
            o_ref[rows, vc] = (o * jax.nn.silu(r[:, vc])).astype(BF16)

    state = s_ref[...]
    dec, prod = {}, {}
    for i in range(n_pairs + 2):
        if i < n_pairs:
            dec[i] = decays(i)
        if 0 <= i - 1 < n_pairs:
            state, prod[i - 1] = products(i - 1, dec.pop(i - 1), state)
        if 0 <= i - 2 < n_pairs:
            outputs(i - 2, *prod.pop(i - 2))
    s_ref[...] = state


def _gla_call(gq, gk, gv, gr, la, gla_norm, layer):
    bsz, seq, _ = gq.shape
    tl = GLA_TILE

    def tok(w):
        return pl.BlockSpec((None, tl, w), lambda b, i: (b, i, 0))

    return pl.pallas_call(
        functools.partial(_gla_kernel, n_pairs=tl // GLA_PAIR),
        grid=(bsz, seq // tl),
        in_specs=[tok(GLA_QK_W), tok(GLA_QK_W), tok(GLA_V_W), tok(GLA_V_W), tok(GLA_QK_W),
                  pl.BlockSpec((None, 1, GLA_V_W), lambda b, i: (layer, 0, 0))],
        out_specs=tok(GLA_V_W),
        out_shape=jax.ShapeDtypeStruct((bsz, seq, GLA_V_W), BF16),
        scratch_shapes=[pltpu.VMEM((GLA_QK_W, GLA_DV), F32)],
        compiler_params=_params("arbitrary", "arbitrary"),
        name="gla_mixer",
    )(gq, gk, gv, gr, la, gla_norm.reshape(DEPTH, 1, GLA_V_W))


def _t5_bucket_tiles():
    t = ATT_TILE
    r = np.arange(t)[:, None]
    c = np.arange(t)[None, :]
    max_exact = NUM_BUCKETS // 2
    tiles = []
    for off in range(3):
        dist = off * t + r - c
        d = np.maximum(dist, 1).astype(np.float32)
        large = max_exact + (np.log(d / np.float32(max_exact)) / np.float32(math.log(MAX_DISTANCE / max_exact))
                             * np.float32(NUM_BUCKETS - max_exact)).astype(np.int32)
        large = np.minimum(large, NUM_BUCKETS - 1)
        bucket = np.where(dist < max_exact, dist, large)
        tiles.append(np.where(dist >= 0, bucket, -1).T)
    far = tiles[2]
    assert (far == NUM_BUCKETS - 1).all()
    return np.stack(tiles).astype(np.int32)


def _bias_kernel(rel_ref, bucket_ref, o_ref):
    m = pl.program_id(0)
    bucket = bucket_ref[0:2]
    acc = jnp.full(bucket.shape, -jnp.inf, F32)
    for kk in range(NUM_BUCKETS):
        acc = jnp.where(bucket == kk, rel_ref[kk, m], acc)
    o_ref[0:2] = acc * LOG2E
    o_ref[2] = jnp.full(o_ref.shape[1:], rel_ref[NUM_BUCKETS - 1, m], F32) * LOG2E


def _bias_call(rel_bias):
    n_maps = 2 * DIFF_HEADS
    t = ATT_TILE
    return pl.pallas_call(
        _bias_kernel,
        grid=(n_maps,),
        in_specs=[pl.BlockSpec(memory_space=pltpu.SMEM),
                  pl.BlockSpec((3, t, t), lambda m: (0, 0, 0))],
        out_specs=pl.BlockSpec((None, 3, t, t), lambda m: (m, 0, 0, 0)),
        out_shape=jax.ShapeDtypeStruct((n_maps, 3, t, t), F32),
        compiler_params=_params("arbitrary"),
        name="t5_bias_tiles",
    )(rel_bias, jnp.asarray(_t5_bucket_tiles()))


def _diff_kernel(q_ref, k_ref, v_ref, bias_ref, lam_ref, g_ref, o_ref, *, lam_init, n_tiles):
    t = ATT_TILE
    lv = lam_ref[...]
    lam = (jnp.exp(jnp.sum(lv[0:1] * lv[1:2], axis=-1, keepdims=True))
           - jnp.exp(jnp.sum(lv[2:3] * lv[3:4], axis=-1, keepdims=True)) + lam_init)
    v_t = jnp.concatenate([v_ref[...].astype(F32).T, jnp.ones((ONES_ROWS, v_ref.shape[0]), F32)],
                          axis=0).astype(BF16)
    lane = lax.broadcasted_iota(jnp.int32, (t, 2 * DIFF_DQK), 1)
    far = [bias_ref[mp, 2, 0:1, 0:1] for mp in range(2)]

    tasks = [(qi, kj, mp) for qi in range(n_tiles) for kj in range(qi + 1) for mp in range(2)]
    masked_q = {}
    state = {}
    done = {}

    def logits(qi, kj, mp):
        if qi not in masked_q:
            q = q_ref[qi * t:(qi + 1) * t, :]
            zero = jnp.zeros_like(q)
            masked_q[qi] = (jnp.where(lane < DIFF_DQK, q, zero), jnp.where(lane < DIFF_DQK, zero, q))
        s = _dot_nt(k_ref[kj * t:(kj + 1) * t, :], masked_q[qi][mp])
        return s + bias_ref[mp, qi - kj] if qi - kj < 2 else s

    def accumulate(qi, kj, mp, s):
        shift = 0.0 if qi - kj < 2 else far[mp]
        m_blk = jnp.max(s, axis=0, keepdims=True) + shift
        prev = state.get((qi, mp))
        m_new = m_blk if prev is None else jnp.maximum(prev[0], m_blk)
        p = jnp.exp2((s - (m_new - shift)).astype(BF16))
        pv = _dot(v_t[:, kj * t:(kj + 1) * t], p)
        if prev is None:
            state[(qi, mp)] = (m_new, pv)
        else:
            state[(qi, mp)] = (m_new, jnp.exp2(prev[0] - m_new) * prev[1] + pv)
        if kj == qi:
            _, acc = state.pop((qi, mp))
            done.setdefault(qi, {})[mp] = acc[:DIFF_DV] / acc[DIFF_DV:DIFF_DV + 1]
            if len(done[qi]) == 2:
                o = done[qi][0] - lam * done[qi][1]
                o = o * lax.rsqrt(jnp.mean(o * o, axis=0, keepdims=True) + EPS)
                o_ref[qi * t:(qi + 1) * t, :] = (o.T * g_ref[...] * (1.0 - lam_init)).astype(BF16)
                del done[qi]

    pending = []
    for task in tasks:
        pending.append((task, logits(*task)))
        if len(pending) > ATT_LOOKAHEAD:
            ready, s = pending.pop(0)
            accumulate(*ready, s)
    for ready, s in pending:
        accumulate(*ready, s)


def _diff_call(dq, dk, dv, bias, diff_lambda, diff_norm, layer):
    bsz, seq, _ = dq.shape
    t = ATT_TILE
    lam_init = 0.8 - 0.6 * math.exp(-0.3 * layer)
    return pl.pallas_call(
        functools.partial(_diff_kernel, lam_init=lam_init, n_tiles=seq // t),
        grid=(bsz, DIFF_HEADS),
        in_specs=[
            pl.BlockSpec((None, seq, 2 * DIFF_DQK), lambda b, h: (b, 0, h)),
            pl.BlockSpec((None, seq, 2 * DIFF_DQK), lambda b, h: (b, 0, h)),
            pl.BlockSpec((None, seq, DIFF_DV), lambda b, h: (b, 0, h)),
            pl.BlockSpec((2, 3, t, t), lambda b, h: (h, 0, 0, 0)),
            pl.BlockSpec((None, 4, DIFF_DQK), lambda b, h: (layer, 0, 0)),
            pl.BlockSpec((None, 1, DIFF_DV), lambda b, h: (layer, 0, h)),
        ],
        out_specs=pl.BlockSpec((None, seq, DIFF_DV), lambda b, h: (b, 0, h)),
        out_shape=jax.ShapeDtypeStruct((bsz, seq, DIFF_V_W), BF16),
        compiler_params=_params("arbitrary", "arbitrary"),
        name="diff_attention",
    )(dq, dk, dv, bias, diff_lambda, diff_norm.reshape(DEPTH, 1, DIFF_V_W))


def _s5_prep_kernel(lre_ref, lim_ref, lstep_ref, bre_ref, bim_ref, cre_ref, cim_ref,
                    are_ref, aim_ref, bmat_ref, cmat_ref):
    lr = lre_ref[...]
    li = lim_ref[...]
    dt = jnp.exp(lstep_ref[...])
    mag = jnp.exp(lr * dt)
    a_re = mag * jnp.cos(li * dt)
    a_im = mag * jnp.sin(li * dt)
    are_ref[...] = jnp.broadcast_to(a_re, are_ref.shape)
    aim_ref[...] = jnp.broadcast_to(a_im, aim_ref.shape)
    nr, ni = a_re - 1.0, a_im
    den = lr * lr + li * li
    f_re = (nr * lr + ni * li) / den
    f_im = (ni * lr - nr * li) / den
    ns, ch = S5_BLOCK_STATE, S5_BLOCK_CH
    gpb = ns // S5_STATE
    in_own = (lax.broadcasted_iota(jnp.int32, (ch, ns), 0) // S5_GROUP_CH
              == lax.broadcasted_iota(jnp.int32, (ch, ns), 1) // S5_STATE)
    out_own = (lax.broadcasted_iota(jnp.int32, (ns, ch), 0) // S5_STATE
               == lax.broadcasted_iota(jnp.int32, (ns, ch), 1) // S5_GROUP_CH)

    def in_block(ref, gb):
        return jnp.where(in_own, jnp.concatenate([ref[gb]] * gpb, axis=1), 0.0)

    def out_block(ref, gb):
        return jnp.where(out_own, jnp.concatenate([ref[gb]] * gpb, axis=0), 0.0)

    for gb in range(S5_GROUP_BLOCKS):
        fr = f_re[:, gb * ns:(gb + 1) * ns]
        fi = f_im[:, gb * ns:(gb + 1) * ns]
        br = in_block(bre_ref, gb)
        bi = in_block(bim_ref, gb)
        bmat_ref[gb, :, :ns] = (fr * br - fi * bi).astype(BF16)
        bmat_ref[gb, :, ns:] = (fr * bi + fi * br).astype(BF16)
        cmat_ref[gb, :ns, :] = out_block(cre_ref, gb).astype(BF16)
        cmat_ref[gb, ns:, :] = (-out_block(cim_ref, gb)).astype(BF16)


def _s5_prep_call(lam_re, lam_im, log_step, b_re, b_im, c_re, c_im, bsz):
    d = DEPTH
    nb, ch, ns = S5_GROUP_BLOCKS, S5_BLOCK_CH, S5_BLOCK_STATE
    gpb = S5_GROUPS // nb

    def in_blocks(b):
        return b.reshape(d, nb, gpb, S5_STATE, S5_GROUP_CH).transpose(0, 1, 2, 4, 3).reshape(d, nb, ch, S5_STATE)

    def out_blocks(c):
        return c.reshape(d, nb, gpb, S5_GROUP_CH, S5_STATE).transpose(0, 1, 4, 2, 3).reshape(d, nb, S5_STATE, ch)

    def row(x):
        return x.reshape(d, 1, S5_NSTATE)

    lstep = jnp.broadcast_to(log_step[:, :, None], (d, S5_GROUPS, S5_STATE))

    def full(shape):
        nd = len(shape)
        return pl.BlockSpec((None,) + shape, lambda l: (l,) + (0,) * nd)

    return pl.pallas_call(
        _s5_prep_kernel,
        grid=(d,),
        in_specs=[full((1, S5_NSTATE))] * 3 + [full((nb, ch, S5_STATE))] * 2 + [full((nb, S5_STATE, ch))] * 2,
        out_specs=[full((bsz, S5_NSTATE)), full((bsz, S5_NSTATE)),
                   full((nb, ch, 2 * ns)), full((nb, 2 * ns, ch))],
        out_shape=[jax.ShapeDtypeStruct((d, bsz, S5_NSTATE), F32), jax.ShapeDtypeStruct((d, bsz, S5_NSTATE), F32),
                   jax.ShapeDtypeStruct((d, nb, ch, 2 * ns), BF16), jax.ShapeDtypeStruct((d, nb, 2 * ns, ch), BF16)],
        compiler_params=_params("arbitrary"),
        name="s5_discretize",
    )(row(lam_re), row(lam_im), row(lstep), in_blocks(b_re), in_blocks(b_im), out_blocks(c_re), out_blocks(c_im))


def _s5_kernel(u_ref, are_ref, aim_ref, bmat_ref, cmat_ref, d_ref, wglu_ref, bglu_ref, o_ref,
               tb_ref, xr_ref, xi_ref, sr_ref, si_ref, *, bsz, n_steps):
    @pl.when(pl.program_id(0) == 0)
    def _():
        sr_ref[...] = jnp.zeros_like(sr_ref)
        si_ref[...] = jnp.zeros_like(si_ref)

    ns, ch, nb = S5_BLOCK_STATE, S5_BLOCK_CH, S5_GROUP_BLOCKS
    for b in range(bsz):
        ub = u_ref[b].astype(F32)
        for gb in range(nb):
            tb_ref[gb, pl.ds(b, n_steps, stride=bsz), :] = ub[:, gb * ch:(gb + 1) * ch]
    u = [tb_ref[gb] for gb in range(nb)]

    def drive(gb):
        return _dot(u[gb].astype(BF16), bmat_ref[gb])

    def scan(gb, bu):
        cols = slice(gb * ns, (gb + 1) * ns)
        ar, ai = are_ref[:, cols], aim_ref[:, cols]
        pr, pi = sr_ref[:, cols], si_ref[:, cols]
        for t in range(n_steps):
            rows = slice(t * bsz, (t + 1) * bsz)
            pr, pi = ar * pr - ai * pi + bu[rows, :ns], ar * pi + ai * pr + bu[rows, ns:]
            xr_ref[rows, cols] = pr
            xi_ref[rows, cols] = pi
        sr_ref[:, cols] = pr
        si_ref[:, cols] = pi

    def readout(gb):
        cols = slice(gb * ns, (gb + 1) * ns)
        return (_dot(xr_ref[:, cols].astype(BF16), cmat_ref[gb, :ns, :])
                + _dot(xi_ref[:, cols].astype(BF16), cmat_ref[gb, ns:, :]))

    bu, ys = {0: drive(0)}, []
    for gb in range(nb):
        if gb + 1 < nb:
            bu[gb + 1] = drive(gb + 1)
        scan(gb, bu.pop(gb))
        if gb >= 1:
            ys.append(readout(gb - 1))
    ys.append(readout(nb - 1))

    y = jax.nn.gelu(jnp.concatenate(ys, axis=-1) + d_ref[...] * jnp.concatenate(u, axis=-1))
    z = _dot(y.astype(BF16), wglu_ref[...]) + bglu_ref[...]
    out = z[:, :S5_WIDTH] * jax.nn.sigmoid(z[:, S5_WIDTH:])
    for gb in range(nb):
        tb_ref[gb] = out[:, gb * ch:(gb + 1) * ch]
    for b in range(bsz):
        for gb in range(nb):
            o_ref[b, :, gb * ch:(gb + 1) * ch] = tb_ref[gb, pl.ds(b, n_steps, stride=bsz), :].astype(BF16)


def _s5_call(su, a_re, a_im, bmat, cmat, d_skip, wglu, bglu, layer):
    bsz, seq, _ = su.shape
    tl = S5_TIME_TILE
    tr = tl * bsz
    nb, ch, ns = S5_GROUP_BLOCKS, S5_BLOCK_CH, S5_BLOCK_STATE

    def res(shape):
        nd = len(shape)
        return _resident((None,) + shape, lambda i: (layer,) + (0,) * nd)

    tok = pl.BlockSpec((bsz, tl, S5_WIDTH), lambda i: (0, i, 0))
    return pl.pallas_call(
        functools.partial(_s5_kernel, bsz=bsz, n_steps=tl),
        grid=(seq // tl,),
        in_specs=[tok,
                  res((bsz, S5_NSTATE)), res((bsz, S5_NSTATE)),
                  res((nb, ch, 2 * ns)), res((nb, 2 * ns, ch)),
                  res((1, S5_WIDTH)), res((S5_WIDTH, 2 * S5_WIDTH)), res((1, 2 * S5_WIDTH))],
        out_specs=tok,
        out_shape=jax.ShapeDtypeStruct((bsz, seq, S5_WIDTH), BF16),
        scratch_shapes=[pltpu.VMEM((nb, tr, ch), F32),
                        pltpu.VMEM((tr, S5_NSTATE), F32), pltpu.VMEM((tr, S5_NSTATE), F32),
                        pltpu.VMEM((bsz, S5_NSTATE), F32), pltpu.VMEM((bsz, S5_NSTATE), F32)],
        compiler_params=_params("arbitrary"),
        name="s5_mixer",
    )(su, a_re, a_im, bmat, cmat, d_skip, wglu, bglu)


def _merge_kernel(h_ref, ada_ref, npre_ref, npost_ref, yg_ref, yd_ref, ys_ref, wt_hbm, wbr_hbm, wout_hbm, o_ref,
                  wgate_ref, wbr_ref, wout_ref, stage, sem, *, layer):
    @pl.when((pl.program_id(0) == 0) & (pl.program_id(1) == 0))
    def _():
        _stream_cast(wt_hbm.at[layer, MIX_GATE_ROW:MIX_GATE_ROW + N_BRANCH * D_MODEL], wgate_ref, stage, sem,
                     axis=0, chunk=MERGE_STAGE_ROWS)
        _stream_cast(wbr_hbm.at[layer], wbr_ref, stage, sem, axis=0, chunk=MERGE_STAGE_ROWS)
        _stream_cast(wout_hbm.at[layer], wout_ref, stage, sem, axis=0, chunk=MERGE_STAGE_ROWS)

    blocks = [slice(r * ROW_BLOCK, (r + 1) * ROW_BLOCK) for r in range(h_ref.shape[0] // ROW_BLOCK)]
    us = [_prenorm_mod(h_ref[rows, :], npre_ref[1:2, :], ada_ref[3:4, :], ada_ref[4:5, :]).astype(BF16)
          for rows in blocks]
    merged = [None] * len(blocks)
    for i, y_ref in enumerate((yg_ref, yd_ref, ys_ref)):
        for r, rows in enumerate(blocks):
            gate = jax.nn.sigmoid(_dot_nt(us[r], wgate_ref[i * D_MODEL:(i + 1) * D_MODEL, :]))
            term = gate * _dot(y_ref[rows, :], wbr_ref[i * S5_WIDTH:(i + 1) * S5_WIDTH, :])
            merged[r] = term if merged[r] is None else merged[r] + term
    ys = [_dot(m.astype(BF16), wout_ref[...]) for m in merged]
    for rows, y in zip(blocks, ys):
        o_ref[rows, :] = h_ref[rows, :] + ada_ref[5:6, :] * (_rms(y) * npost_ref[1:2, :])


def _merge_call(h, ada4, norm_pre, norm_post, y_gla, y_diff, y_s5, w_in_t, wbr, wout, layer):
    bsz, seq, _ = h.shape
    tm = TOKEN_TILE

    def tok(w):
        return pl.BlockSpec((None, tm, w), lambda b, i: (b, i, 0))

    return pl.pallas_call(
        functools.partial(_merge_kernel, layer=layer),
        grid=(bsz, seq // tm),
        in_specs=[
            tok(D_MODEL),
            pl.BlockSpec((None, None, ADA_CHUNKS, D_MODEL), lambda b, i: (layer, b, 0, 0)),
            pl.BlockSpec((None, 3, D_MODEL), lambda b, i: (layer, 0, 0)),
            pl.BlockSpec((None, 3, D_MODEL), lambda b, i: (layer, 0, 0)),
            tok(GLA_V_W), tok(DIFF_V_W), tok(S5_WIDTH),
            pl.BlockSpec(memory_space=pl.ANY),
            pl.BlockSpec(memory_space=pl.ANY),
            pl.BlockSpec(memory_space=pl.ANY),
        ],
        out_specs=tok(D_MODEL),
        out_shape=jax.ShapeDtypeStruct(h.shape, F32),
        scratch_shapes=[pltpu.VMEM((N_BRANCH * D_MODEL, D_MODEL), BF16),
                        pltpu.VMEM((N_BRANCH * S5_WIDTH, D_MODEL), BF16), pltpu.VMEM((D_MODEL, D_MODEL), BF16),
                        pltpu.VMEM((WEIGHT_STAGE_SLOTS, MERGE_STAGE_ROWS, D_MODEL), F32),
                        pltpu.SemaphoreType.DMA((WEIGHT_STAGE_SLOTS,))],
        compiler_params=_params("arbitrary", "arbitrary"),
        name="mixer_merge",
    )(h, ada4, norm_pre, norm_post, y_gla, y_diff, y_s5, w_in_t, wbr, wout)


def kernel(x, c, rel_bias, w_ada, b_ada, norm_pre, norm_post, ffn1_w_up, ffn1_w_down, ffn2_w_up, ffn2_w_down,
           w_in, gla_w_alpha, gla_b_alpha, gla_norm, diff_lambda, diff_norm, s5_lam_re, s5_lam_im, s5_log_step,
           s5_b_re, s5_b_im, s5_c_re, s5_c_im, s5_d, s5_w_glu, s5_b_glu, w_branch, w_out):
    bsz, seq, _ = x.shape
    assert bsz == SUBLANES, "the S5 scan keeps the batch on the sublane axis"

    f1 = (ffn1_w_up, ffn1_w_down)
    f2 = (ffn2_w_up, ffn2_w_down)
    w_in_t = jnp.swapaxes(w_in, 1, 2)
    walpha = jnp.pad(gla_w_alpha.astype(BF16), ((0, 0), (0, LANES - GLA_LOWRANK), (0, 0)))
    balpha = gla_b_alpha.reshape(DEPTH, 1, GLA_QK_W)
    wbr = w_branch.reshape(DEPTH, N_BRANCH * S5_WIDTH, D_MODEL)
    wout = w_out
    wglu = s5_w_glu.astype(BF16)
    bglu = s5_b_glu.reshape(DEPTH, 1, 2 * S5_WIDTH)
    d_skip = s5_d.reshape(DEPTH, 1, S5_WIDTH)

    ada4 = _ada_call(c, w_ada, b_ada).reshape(DEPTH, bsz, ADA_CHUNKS, D_MODEL)
    bias = _bias_call(rel_bias)
    a_re, a_im, bmat, cmat = _s5_prep_call(s5_lam_re, s5_lam_im, s5_log_step, s5_b_re, s5_b_im,
                                           s5_c_re, s5_c_im, bsz)

    h = x
    for layer in range(DEPTH):
        h = _ffn_call(h, ada4, norm_pre, norm_post, *f1, layer, 0)
        gq, gk, gv, gr, la, dq, dk, dv, su = _inproj_call(h, ada4, norm_pre, w_in_t, walpha, balpha, layer)
        y_gla = _gla_call(gq, gk, gv, gr, la, gla_norm, layer)
        y_diff = _diff_call(dq, dk, dv, bias, diff_lambda, diff_norm, layer)
        y_s5 = _s5_call(su, a_re, a_im, bmat, cmat, d_skip, wglu, bglu, layer)
        h = _merge_call(h, ada4, norm_pre, norm_post, y_gla, y_diff, y_s5, w_in_t, wbr, wout, layer)
        h = _ffn_call(h, ada4, norm_pre, norm_post, *f2, layer, 2)
    return h
```

```python
import functools
import math

import numpy as np
import jax
import jax.numpy as jnp
from jax import lax
from jax.experimental import pallas as pl
from jax.experimental.pallas import tpu as pltpu

F32 = jnp.float32
BF16 = jnp.bfloat16

D_MODEL = 1024
DEPTH = 2
D_FF = 2816
ADA_CHUNKS = 9
EPS = 1e-6
LOG2E = math.log2(math.e)

GLA_HEADS = 4
GLA_DK = 64
GLA_DV = 128
GLA_LOWRANK = 16
GLA_TAU = 16.0
GLA_CHUNK = 64
GLA_QK_W = GLA_HEADS * GLA_DK
GLA_V_W = GLA_HEADS * GLA_DV

DIFF_HEADS = 4
DIFF_DQK = 64
DIFF_DV = 128
DIFF_QK_W = DIFF_HEADS * 2 * DIFF_DQK
DIFF_V_W = DIFF_HEADS * DIFF_DV
NUM_BUCKETS = 32
MAX_DISTANCE = 128

S5_GROUPS = 32
S5_GROUP_CH = 16
S5_STATE = 64
S5_WIDTH = S5_GROUPS * S5_GROUP_CH
S5_NSTATE = S5_GROUPS * S5_STATE
S5_GROUP_BLOCKS = 4
S5_BLOCK_CH = S5_WIDTH // S5_GROUP_BLOCKS
S5_BLOCK_STATE = S5_NSTATE // S5_GROUP_BLOCKS

N_BRANCH = 3

SUBLANES = 8
LANES = 128
ONES_ROWS = 2 * SUBLANES
VMEM_LIMIT_BYTES = 56 * 1024 * 1024

TOKEN_TILE = 1024
ROW_BLOCK = 512
FF_CHUNK = 256
FF_NCHUNK = D_FF // FF_CHUNK
WEIGHT_STAGE_COLS = 512
WEIGHT_STAGE_SLOTS = 4
MIX_GATE_ROW = 2 * GLA_QK_W + 2 * GLA_V_W + GLA_LOWRANK + 2 * DIFF_QK_W + DIFF_V_W + S5_WIDTH
MIX_STAGE_ROWS = 240
FFN_LOOKAHEAD = 1
GLA_PAIR = 2 * GLA_CHUNK
GLA_TILE = 1024
ATT_TILE = 256
ATT_LOOKAHEAD = 4
S5_TIME_TILE = 64
ADA_COL_TILE = 1536


def _dot(a, b):
    return jnp.dot(a, b, preferred_element_type=F32)


def _dot_nt(a, b):
    return lax.dot_general(a, b, (((1,), (1,)), ((), ())), preferred_element_type=F32)


def _rms(x):
    return x * lax.rsqrt(jnp.mean(x * x, axis=-1, keepdims=True) + EPS)


def _prenorm_mod(h, g, shift, scale):
    return (_rms(h) * g) * (1.0 + scale) + shift


def _params(*sem):
    return pltpu.CompilerParams(dimension_semantics=sem, vmem_limit_bytes=VMEM_LIMIT_BYTES)


def _resident(shape, index_map):
    return pl.BlockSpec(shape, index_map, pipeline_mode=pl.Buffered(1))


def _ada_kernel(c_ref, w_ref, b_ref, o_ref):
    cond = jax.nn.silu(c_ref[...]).astype(BF16)
    o_ref[...] = _dot(cond, w_ref[...].astype(BF16)) + b_ref[...]


def _ada_call(c, w_ada, b_ada):
    bsz = c.shape[0]
    n = ADA_CHUNKS * D_MODEL
    return pl.pallas_call(
        _ada_kernel,
        grid=(DEPTH, n // ADA_COL_TILE),
        in_specs=[
            pl.BlockSpec((bsz, D_MODEL), lambda l, j: (0, 0)),
            pl.BlockSpec((None, D_MODEL, ADA_COL_TILE), lambda l, j: (l, 0, j)),
            pl.BlockSpec((None, 1, ADA_COL_TILE), lambda l, j: (l, 0, j)),
        ],
        out_specs=pl.BlockSpec((None, bsz, ADA_COL_TILE), lambda l, j: (l, 0, j)),
        out_shape=jax.ShapeDtypeStruct((DEPTH, bsz, n), F32),
        compiler_params=_params("arbitrary", "arbitrary"),
        name="ada_proj",
    )(c, w_ada, b_ada.reshape(DEPTH, 1, n))


class _WeightStream:
    def __init__(self, src, dst, stage, sem, *, axis, chunk):
        self.src, self.dst, self.stage, self.sem, self.axis, self.chunk = src, dst, stage, sem, axis, chunk
        self.slots = stage.shape[0]
        self.n = src.shape[axis] // chunk
        self.started = self.done = 0

    def _window(self, c):
        part = slice(c * self.chunk, (c + 1) * self.chunk)
        return (slice(None), part) if self.axis == 1 else (part, slice(None))

    def _copy(self, c):
        slot = c % self.slots
        return pltpu.make_async_copy(self.src.at[self._window(c)], self.stage.at[slot], self.sem.at[slot])

    def ensure(self, upto):
        while self.done < min(self.n, -(-upto // self.chunk)):
            c = self.done
            while self.started < min(self.n, c + self.slots):
                self._copy(self.started).start()
                self.started += 1
            self._copy(c).wait()
            self.dst[self._window(c)] = self.stage[c % self.slots].astype(BF16)
            self.done += 1


def _stream_cast(src, dst, stage, sem, *, axis, chunk):
    _WeightStream(src, dst, stage, sem, axis=axis, chunk=chunk).ensure(src.shape[axis])


def _ffn_kernel(h_ref, ada_ref, npre_ref, npost_ref, wup_hbm, wdn_hbm, o_ref,
                wup_ref, wdn_ref, up_stage, dn_stage, up_sem, dn_sem, *, sub, layer):
    a0 = 3 * sub

    @pl.when((pl.program_id(0) == 0) & (pl.program_id(1) == 0))
    def _():
        _stream_cast(wup_hbm.at[layer], wup_ref, up_stage, up_sem, axis=1, chunk=WEIGHT_STAGE_COLS)
        _stream_cast(wdn_hbm.at[layer], wdn_ref, dn_stage, dn_sem, axis=0, chunk=FF_CHUNK)

    def down(j, gate, up):
        mid = (jax.nn.silu(gate) * up).astype(BF16)
        return _dot(mid, wdn_ref[j * FF_CHUNK:(j + 1) * FF_CHUNK, :])

    for r in range(h_ref.shape[0] // ROW_BLOCK):
        rows = slice(r * ROW_BLOCK, (r + 1) * ROW_BLOCK)
        h = h_ref[rows, :]
        u = _prenorm_mod(h, npre_ref[sub:sub + 1, :], ada_ref[a0:a0 + 1, :], ada_ref[a0 + 1:a0 + 2, :])
        u = u.astype(BF16)

        def gate_up(j, u=u):
            lo = j * FF_CHUNK
            return _dot(u, wup_ref[:, lo:lo + FF_CHUNK]), _dot(u, wup_ref[:, D_FF + lo:D_FF + lo + FF_CHUNK])

        acc = None
        pending = []
        for j in range(FF_NCHUNK + FFN_LOOKAHEAD):
            if j < FF_NCHUNK:
                pending.append((j,) + gate_up(j))
            if j >= FFN_LOOKAHEAD:
                d = down(*pending.pop(0))
                acc = d if acc is None else acc + d
        post = _rms(acc) * npost_ref[sub:sub + 1, :]
        o_ref[rows, :] = h + (0.5 * ada_ref[a0 + 2:a0 + 3, :]) * post


def _ffn_call(h, ada4, norm_pre, norm_post, wup, wdn, layer, sub):
    bsz, seq, _ = h.shape
    tm = TOKEN_TILE
    tok = pl.BlockSpec((None, tm, D_MODEL), lambda b, i: (b, i, 0))
    return pl.pallas_call(
        functools.partial(_ffn_kernel, sub=sub, layer=layer),
        grid=(bsz, seq // tm),
        in_specs=[
            tok,
            pl.BlockSpec((None, None, ADA_CHUNKS, D_MODEL), lambda b, i: (layer, b, 0, 0)),
            pl.BlockSpec((None, 3, D_MODEL), lambda b, i: (layer, 0, 0)),
            pl.BlockSpec((None, 3, D_MODEL), lambda b, i: (layer, 0, 0)),
            pl.BlockSpec(memory_space=pl.ANY),
            pl.BlockSpec(memory_space=pl.ANY),
        ],
        out_specs=tok,
        out_shape=jax.ShapeDtypeStruct(h.shape, F32),
        scratch_shapes=[pltpu.VMEM((D_MODEL, 2 * D_FF), BF16), pltpu.VMEM((D_FF, D_MODEL), BF16),
                        pltpu.VMEM((WEIGHT_STAGE_SLOTS, D_MODEL, WEIGHT_STAGE_COLS), F32),
                        pltpu.VMEM((WEIGHT_STAGE_SLOTS, FF_CHUNK, D_MODEL), F32),
                        pltpu.SemaphoreType.DMA((WEIGHT_STAGE_SLOTS,)),
                        pltpu.SemaphoreType.DMA((WEIGHT_STAGE_SLOTS,))],
        compiler_params=_params("arbitrary", "arbitrary"),
        name=f"ffn{sub // 2 + 1}",
    )(h, ada4, norm_pre, norm_post, wup, wdn)


def _log_sigmoid(z):
    return jnp.minimum(z, 0.0) - jnp.log1p(jnp.exp(-jnp.abs(z)))


def _inproj_kernel(h_ref, ada_ref, npre_ref, wt_hbm, walpha_ref, balpha_ref,
                   gq_ref, gk_ref, gv_ref, gr_ref, la_ref, dq_ref, dk_ref, dv_ref, su_ref,
                   wt_ref, stage, sem, *, layer):
    o_alpha = 2 * GLA_QK_W + 2 * GLA_V_W
    o_diff = o_alpha + GLA_LOWRANK
    o_s5 = o_diff + 2 * DIFF_QK_W + DIFF_V_W

    def project(need):
        for r in range(h_ref.shape[0] // ROW_BLOCK):
            rows = slice(r * ROW_BLOCK, (r + 1) * ROW_BLOCK)
            u = _prenorm_mod(h_ref[rows, :], npre_ref[1:2, :], ada_ref[3:4, :], ada_ref[4:5, :]).astype(BF16)
            need(o_alpha + LANES)
            a_low = _dot_nt(u, wt_ref[o_alpha:o_alpha + LANES, :]).astype(BF16)
            pg = _dot_nt(u, wt_ref[0:o_alpha, :])
            z = _dot(a_low, walpha_ref[...]) + balpha_ref[...]
            gq_ref[rows, :] = pg[:, :GLA_QK_W]
            gk_ref[rows, :] = pg[:, GLA_QK_W:2 * GLA_QK_W]
            gv_ref[rows, :] = pg[:, 2 * GLA_QK_W:2 * GLA_QK_W + GLA_V_W].astype(BF16)
            gr_ref[rows, :] = pg[:, 2 * GLA_QK_W + GLA_V_W:]
            la_ref[rows, :] = _log_sigmoid(z) / GLA_TAU
            need(o_s5)
            pd = _dot_nt(u, wt_ref[o_diff:o_s5, :])
            dq_ref[rows, :] = (pd[:, :DIFF_QK_W] * (DIFF_DQK ** -0.5 * LOG2E)).astype(BF16)
            dk_ref[rows, :] = pd[:, DIFF_QK_W:2 * DIFF_QK_W].astype(BF16)
            dv_ref[rows, :] = pd[:, 2 * DIFF_QK_W:].astype(BF16)
            need(MIX_GATE_ROW)
            su_ref[rows, :] = _dot_nt(u, wt_ref[o_s5:MIX_GATE_ROW, :]).astype(BF16)

    first = (pl.program_id(0) == 0) & (pl.program_id(1) == 0)

    @pl.when(first)
    def _():
        stream = _WeightStream(wt_hbm.at[layer, 0:MIX_GATE_ROW], wt_ref, stage, sem, axis=0, chunk=MIX_STAGE_ROWS)
        project(stream.ensure)

    @pl.when(jnp.logical_not(first))
    def _():
        project(lambda n: None)


def _inproj_call(h, ada4, norm_pre, w_in_t, walpha, balpha, layer):
    bsz, seq, _ = h.shape
    tm = TOKEN_TILE

    def tok(w):
        return pl.BlockSpec((None, tm, w), lambda b, i: (b, i, 0))

    def sds(w, dt):
        return jax.ShapeDtypeStruct((bsz, seq, w), dt)

    def res(arr):
        nd = arr.ndim - 1
        return _resident((None,) + arr.shape[1:], lambda b, i: (layer,) + (0,) * nd)

    return pl.pallas_call(
        functools.partial(_inproj_kernel, layer=layer),
        grid=(bsz, seq // tm),
        in_specs=[
            tok(D_MODEL),
            pl.BlockSpec((None, None, ADA_CHUNKS, D_MODEL), lambda b, i: (layer, b, 0, 0)),
            pl.BlockSpec((None, 3, D_MODEL), lambda b, i: (layer, 0, 0)),
            pl.BlockSpec(memory_space=pl.ANY), res(walpha), res(balpha),
        ],
        out_specs=[tok(GLA_QK_W), tok(GLA_QK_W), tok(GLA_V_W), tok(GLA_V_W), tok(GLA_QK_W),
                   tok(DIFF_QK_W), tok(DIFF_QK_W), tok(DIFF_V_W), tok(S5_WIDTH)],
        out_shape=[sds(GLA_QK_W, F32), sds(GLA_QK_W, F32), sds(GLA_V_W, BF16), sds(GLA_V_W, F32),
                   sds(GLA_QK_W, F32), sds(DIFF_QK_W, BF16), sds(DIFF_QK_W, BF16), sds(DIFF_V_W, BF16),
                   sds(S5_WIDTH, BF16)],
        scratch_shapes=[pltpu.VMEM((MIX_GATE_ROW, D_MODEL), BF16),
                        pltpu.VMEM((WEIGHT_STAGE_SLOTS, MIX_STAGE_ROWS, D_MODEL), F32),
                        pltpu.SemaphoreType.DMA((WEIGHT_STAGE_SLOTS,))],
        compiler_params=_params("arbitrary", "arbitrary"),
        name="mixer_in_proj",
    )(h, ada4, norm_pre, w_in_t, walpha, balpha)


def _cumsum_rows(tri, x):
    hi = x.astype(BF16)
    r1 = x - hi.astype(F32)
    mid = r1.astype(BF16)
    lo = (r1 - mid.astype(F32)).astype(BF16)
    return _dot(tri, hi) + _dot(tri, mid) + _dot(tri, lo)


def _gla_kernel(q_ref, k_ref, v_ref, r_ref, la_ref, g_ref, o_ref, s_ref, *, n_pairs):
    @pl.when(pl.program_id(1) == 0)
    def _():
        s_ref[...] = jnp.zeros_like(s_ref)

    row = lax.broadcasted_iota(jnp.int32, (GLA_PAIR, GLA_PAIR), 0)
    col = lax.broadcasted_iota(jnp.int32, (GLA_PAIR, GLA_PAIR), 1)
    causal = (col <= row) & ((row // GLA_CHUNK) == (col // GLA_CHUNK))
    tri = jnp.where(causal, 1.0, 0.0).astype(BF16)
    qlane = lax.broadcasted_iota(jnp.int32, (GLA_PAIR, GLA_QK_W), 1)
    tlane = lax.broadcasted_iota(jnp.int32, (GLA_QK_W, GLA_PAIR), 1)
    first = tlane < GLA_CHUNK

    def decays(p):
        rows = slice(p * GLA_PAIR, (p + 1) * GLA_PAIR)
        b = _cumsum_rows(tri, la_ref[rows, :])
        b_t = b.T
        qd = (q_ref[rows, :] * (GLA_DK ** -0.5)) * jnp.exp(b)
        k_t = k_ref[rows, :].T
        bl0 = b_t[:, GLA_CHUNK - 1:GLA_CHUNK]
        bl1 = b_t[:, GLA_PAIR - 1:GLA_PAIR]
        kdec_t = k_t * jnp.exp(jnp.where(first, bl0, bl1) - b_t)
        return dict(
            qh=[jnp.where(qlane // GLA_DK == h, qd, 0.0).astype(BF16) for h in range(GLA_HEADS)],
            kd_t=(k_t * jnp.exp(-b_t)).astype(BF16),
            kdec0=jnp.where(first, kdec_t, 0.0).astype(BF16),
            kdec1=jnp.where(first, 0.0, kdec_t).astype(BF16),
            dec0=jnp.exp(bl0), dec1=jnp.exp(bl1))

    def products(p, d, s0):
        v = v_ref[p * GLA_PAIR:(p + 1) * GLA_PAIR, :]
        ds0, ds1 = [], []
        for h in range(GLA_HEADS):
            hr = slice(h * GLA_DK, (h + 1) * GLA_DK)
            vh = v[:, h * GLA_DV:(h + 1) * GLA_DV]
            ds0.append(_dot(d["kdec0"][hr, :], vh))
            ds1.append(_dot(d["kdec1"][hr, :], vh))
        att = [jnp.where(causal, _dot(qh, d["kd_t"]), 0.0).astype(BF16) for qh in d["qh"]]
        s1 = d["dec0"] * s0 + jnp.concatenate(ds0, axis=0)
        s2 = d["dec1"] * s1 + jnp.concatenate(ds1, axis=0)
        s0b = s0.astype(BF16)
        s1b = s1.astype(BF16)
        inter = [jnp.concatenate([_dot(qh[:GLA_CHUNK], s0b), _dot(qh[GLA_CHUNK:], s1b)], axis=0)
                 for qh in d["qh"]]
        return s2, (v, att, inter)

    def outputs(p, v, att, inter):
        rows = slice(p * GLA_PAIR, (p + 1) * GLA_PAIR)
        r = r_ref[rows, :]
        for h in range(GLA_HEADS):
            vc = slice(h * GLA_DV, (h + 1) * GLA_DV)
            o = _rms(_dot(att[h], v[:, vc]) + inter[h]) * g_ref[:, vc]
            o_ref[rows, vc] = (o * jax.nn.silu(r[:, vc])).astype(BF16)

    state = s_ref[...]
    dec, prod = {}, {}
    for i in range(n_pairs + 2):
        if i < n_pairs:
            dec[i] = decays(i)
        if 0 <= i - 1 < n_pairs:
            state, prod[i - 1] = products(i - 1, dec.pop(i - 1), state)
        if 0 <= i - 2 < n_pairs:
            outputs(i - 2, *prod.pop(i - 2))
    s_ref[...] = state


def _gla_call(gq, gk, gv, gr, la, gla_norm, layer):
    bsz, seq, _ = gq.shape
    tl = GLA_TILE

    def tok(w):
        return pl.BlockSpec((None, tl, w), lambda b, i: (b, i, 0))

    return pl.pallas_call(
        functools.partial(_gla_kernel, n_pairs=tl // GLA_PAIR),
        grid=(bsz, seq // tl),
        in_specs=[tok(GLA_QK_W), tok(GLA_QK_W), tok(GLA_V_W), tok(GLA_V_W), tok(GLA_QK_W),
                  pl.BlockSpec((None, 1, GLA_V_W), lambda b, i: (layer, 0, 0))],
        out_specs=tok(GLA_V_W),
        out_shape=jax.ShapeDtypeStruct((bsz, seq, GLA_V_W), BF16),
        scratch_shapes=[pltpu.VMEM((GLA_QK_W, GLA_DV), F32)],
        compiler_params=_params("arbitrary", "arbitrary"),
        name="gla_mixer",
    )(gq, gk, gv, gr, la, gla_norm.reshape(DEPTH, 1, GLA_V_W))


def _t5_bucket_tiles():
    t = ATT_TILE
    r = np.arange(t)[:, None]
    c = np.arange(t)[None, :]
    max_exact = NUM_BUCKETS // 2
    tiles = []
    for off in range(3):
        dist = off * t + r - c
        d = np.maximum(dist, 1).astype(np.float32)
        large = max_exact + (np.log(d / np.float32(max_exact)) / np.float32(math.log(MAX_DISTANCE / max_exact))
                             * np.float32(NUM_BUCKETS - max_exact)).astype(np.int32)
        large = np.minimum(large, NUM_BUCKETS - 1)
        bucket = np.where(dist < max_exact, dist, large)
        tiles.append(np.where(dist >= 0, bucket, -1).T)
    far = tiles[2]
    assert (far == NUM_BUCKETS - 1).all()
    return np.stack(tiles).astype(np.int32)


def _bias_kernel(rel_ref, bucket_ref, o_ref):
    m = pl.program_id(0)
    bucket = bucket_ref[0:2]
    acc = jnp.full(bucket.shape, -jnp.inf, F32)
    for kk in range(NUM_BUCKETS):
        acc = jnp.where(bucket == kk, rel_ref[kk, m], acc)
    o_ref[0:2] = acc * LOG2E
    o_ref[2] = jnp.full(o_ref.shape[1:], rel_ref[NUM_BUCKETS - 1, m], F32) * LOG2E


def _bias_call(rel_bias):
    n_maps = 2 * DIFF_HEADS
    t = ATT_TILE
    return pl.pallas_call(
        _bias_kernel,
        grid=(n_maps,),
        in_specs=[pl.BlockSpec(memory_space=pltpu.SMEM),
                  pl.BlockSpec((3, t, t), lambda m: (0, 0, 0))],
        out_specs=pl.BlockSpec((None, 3, t, t), lambda m: (m, 0, 0, 0)),
        out_shape=jax.ShapeDtypeStruct((n_maps, 3, t, t), F32),
        compiler_params=_params("arbitrary"),
        name="t5_bias_tiles",
    )(rel_bias, jnp.asarray(_t5_bucket_tiles()))


def _diff_kernel(q_ref, k_ref, v_ref, bias_ref, lam_ref, g_ref, o_ref, *, lam_init, n_tiles):
    t = ATT_TILE
    lv = lam_ref[...]
    lam = (jnp.exp(jnp.sum(lv[0:1] * lv[1:2], axis=-1, keepdims=True))
           - jnp.exp(jnp.sum(lv[2:3] * lv[3:4], axis=-1, keepdims=True)) + lam_init)
    v_t = jnp.concatenate([v_ref[...].astype(F32).T, jnp.ones((ONES_ROWS, v_ref.shape[0]), F32)],
                          axis=0).astype(BF16)
    lane = lax.broadcasted_iota(jnp.int32, (t, 2 * DIFF_DQK), 1)
    far = [bias_ref[mp, 2, 0:1, 0:1] for mp in range(2)]

    tasks = [(qi, kj, mp) for qi in range(n_tiles) for kj in range(qi + 1) for mp in range(2)]
    masked_q = {}
    state = {}
    done = {}

    def logits(qi, kj, mp):
        if qi not in masked_q:
            q = q_ref[qi * t:(qi + 1) * t, :]
            zero = jnp.zeros_like(q)
            masked_q[qi] = (jnp.where(lane < DIFF_DQK, q, zero), jnp.where(lane < DIFF_DQK, zero, q))
        s = _dot_nt(k_ref[kj * t:(kj + 1) * t, :], masked_q[qi][mp])
        return s + bias_ref[mp, qi - kj] if qi - kj < 2 else s

    def accumulate(qi, kj, mp, s):
        shift = 0.0 if qi - kj < 2 else far[mp]
        m_blk = jnp.max(s, axis=0, keepdims=True) + shift
        prev = state.get((qi, mp))
        m_new = m_blk if prev is None else jnp.maximum(prev[0], m_blk)
        p = jnp.exp2((s - (m_new - shift)).astype(BF16))
        pv = _dot(v_t[:, kj * t:(kj + 1) * t], p)
        if prev is None:
            state[(qi, mp)] = (m_new, pv)
        else:
            state[(qi, mp)] = (m_new, jnp.exp2(prev[0] - m_new) * prev[1] + pv)
        if kj == qi:
            _, acc = state.pop((qi, mp))
            done.setdefault(qi, {})[mp] = acc[:DIFF_DV] / acc[DIFF_DV:DIFF_DV + 1]
            if len(done[qi]) == 2:
                o = done[qi][0] - lam * done[qi][1]
                o = o * lax.rsqrt(jnp.mean(o * o, axis=0, keepdims=True) + EPS)
                o_ref[qi * t:(qi + 1) * t, :] = (o.T * g_ref[...] * (1.0 - lam_init)).astype(BF16)
                del done[qi]

    pending = []
    for task in tasks:
        pending.append((task, logits(*task)))
        if len(pending) > ATT_LOOKAHEAD:
            ready, s = pending.pop(0)
            accumulate(*ready, s)
    for ready, s in pending:
        accumulate(*ready, s)


def _diff_call(dq, dk, dv, bias, diff_lambda, diff_norm, layer):
    bsz, seq, _ = dq.shape
    t = ATT_TILE
    lam_init = 0.8 - 0.6 * math.exp(-0.3 * layer)
    return pl.pallas_call(
        functools.partial(_diff_kernel, lam_init=lam_init, n_tiles=seq // t),
        grid=(bsz, DIFF_HEADS),
        in_specs=[
            pl.BlockSpec((None, seq, 2 * DIFF_DQK), lambda b, h: (b, 0, h)),
            pl.BlockSpec((None, seq, 2 * DIFF_DQK), lambda b, h: (b, 0, h)),
            pl.BlockSpec((None, seq, DIFF_DV), lambda b, h: (b, 0, h)),
            pl.BlockSpec((2, 3, t, t), lambda b, h: (h, 0, 0, 0)),
            pl.BlockSpec((None, 4, DIFF_DQK), lambda b, h: (layer, 0, 0)),
            pl.BlockSpec((None, 1, DIFF_DV), lambda b, h: (layer, 0, h)),
        ],
        out_specs=pl.BlockSpec((None, seq, DIFF_DV), lambda b, h: (b, 0, h)),
        out_shape=jax.ShapeDtypeStruct((bsz, seq, DIFF_V_W), BF16),
        compiler_params=_params("arbitrary", "arbitrary"),
        name="diff_attention",
    )(dq, dk, dv, bias, diff_lambda, diff_norm.reshape(DEPTH, 1, DIFF_V_W))


def _s5_prep_kernel(lre_ref, lim_ref, lstep_ref, bre_ref, bim_ref, cre_ref, cim_ref,
                    are_ref, aim_ref, bmat_ref, cmat_ref):
    lr = lre_ref[...]
    li = lim_ref[...]
    dt = jnp.exp(lstep_ref[...])
    mag = jnp.exp(lr * dt)
    a_re = mag * jnp.cos(li * dt)
    a_im = mag * jnp.sin(li * dt)
    are_ref[...] = jnp.broadcast_to(a_re, are_ref.shape)
    aim_ref[...] = jnp.broadcast_to(a_im, aim_ref.shape)
    nr, ni = a_re - 1.0, a_im
    den = lr * lr + li * li
    f_re = (nr * lr + ni * li) / den
    f_im = (ni * lr - nr * li) / den
    ns, ch = S5_BLOCK_STATE, S5_BLOCK_CH
    gpb = ns // S5_STATE
    in_own = (lax.broadcasted_iota(jnp.int32, (ch, ns), 0) // S5_GROUP_CH
              == lax.broadcasted_iota(jnp.int32, (ch, ns), 1) // S5_STATE)
    out_own = (lax.broadcasted_iota(jnp.int32, (ns, ch), 0) // S5_STATE
               == lax.broadcasted_iota(jnp.int32, (ns, ch), 1) // S5_GROUP_CH)

    def in_block(ref, gb):
        return jnp.where(in_own, jnp.concatenate([ref[gb]] * gpb, axis=1), 0.0)

    def out_block(ref, gb):
        return jnp.where(out_own, jnp.concatenate([ref[gb]] * gpb, axis=0), 0.0)

    for gb in range(S5_GROUP_BLOCKS):
        fr = f_re[:, gb * ns:(gb + 1) * ns]
        fi = f_im[:, gb * ns:(gb + 1) * ns]
        br = in_block(bre_ref, gb)
        bi = in_block(bim_ref, gb)
        bmat_ref[gb, :, :ns] = (fr * br - fi * bi).astype(BF16)
        bmat_ref[gb, :, ns:] = (fr * bi + fi * br).astype(BF16)
        cmat_ref[gb, :ns, :] = out_block(cre_ref, gb).astype(BF16)
        cmat_ref[gb, ns:, :] = (-out_block(cim_ref, gb)).astype(BF16)


def _s5_prep_call(lam_re, lam_im, log_step, b_re, b_im, c_re, c_im, bsz):
    d = DEPTH
    nb, ch, ns = S5_GROUP_BLOCKS, S5_BLOCK_CH, S5_BLOCK_STATE
    gpb = S5_GROUPS // nb

    def in_blocks(b):
        return b.reshape(d, nb, gpb, S5_STATE, S5_GROUP_CH).transpose(0, 1, 2, 4, 3).reshape(d, nb, ch, S5_STATE)

    def out_blocks(c):
        return c.reshape(d, nb, gpb, S5_GROUP_CH, S5_STATE).transpose(0, 1, 4, 2, 3).reshape(d, nb, S5_STATE, ch)

    def row(x):
        return x.reshape(d, 1, S5_NSTATE)

    lstep = jnp.broadcast_to(log_step[:, :, None], (d, S5_GROUPS, S5_STATE))

    def full(shape):
        nd = len(shape)
        return pl.BlockSpec((None,) + shape, lambda l: (l,) + (0,) * nd)

    return pl.pallas_call(
        _s5_prep_kernel,
        grid=(d,),
        in_specs=[full((1, S5_NSTATE))] * 3 + [full((nb, ch, S5_STATE))] * 2 + [full((nb, S5_STATE, ch))] * 2,
        out_specs=[full((bsz, S5_NSTATE)), full((bsz, S5_NSTATE)),
                   full((nb, ch, 2 * ns)), full((nb, 2 * ns, ch))],
        out_shape=[jax.ShapeDtypeStruct((d, bsz, S5_NSTATE), F32), jax.ShapeDtypeStruct((d, bsz, S5_NSTATE), F32),
                   jax.ShapeDtypeStruct((d, nb, ch, 2 * ns), BF16), jax.ShapeDtypeStruct((d, nb, 2 * ns, ch), BF16)],
        compiler_params=_params("arbitrary"),
        name="s5_discretize",
    )(row(lam_re), row(lam_im), row(lstep), in_blocks(b_re), in_blocks(b_im), out_blocks(c_re), out_blocks(c_im))


def _s5_kernel(u_ref, are_ref, aim_ref, bmat_ref, cmat_ref, d_ref, wglu_ref, bglu_ref, o_ref,
               tb_ref, xr_ref, xi_ref, sr_ref, si_ref, *, bsz, n_steps):
    @pl.when(pl.program_id(0) == 0)
    def _():
        sr_ref[...] = jnp.zeros_like(sr_ref)
        si_ref[...] = jnp.zeros_like(si_ref)

    ns, ch, nb = S5_BLOCK_STATE, S5_BLOCK_CH, S5_GROUP_BLOCKS
    for b in range(bsz):
        ub = u_ref[b].astype(F32)
        for gb in range(nb):
            tb_ref[gb, pl.ds(b, n_steps, stride=bsz), :] = ub[:, gb * ch:(gb + 1) * ch]
    u = [tb_ref[gb] for gb in range(nb)]

    def drive(gb):
        return _dot(u[gb].astype(BF16), bmat_ref[gb])

    def scan(gb, bu):
        cols = slice(gb * ns, (gb + 1) * ns)
        ar, ai = are_ref[:, cols], aim_ref[:, cols]
        pr, pi = sr_ref[:, cols], si_ref[:, cols]
        for t in range(n_steps):
            rows = slice(t * bsz, (t + 1) * bsz)
            pr, pi = ar * pr - ai * pi + bu[rows, :ns], ar * pi + ai * pr + bu[rows, ns:]
            xr_ref[rows, cols] = pr
            xi_ref[rows, cols] = pi
        sr_ref[:, cols] = pr
        si_ref[:, cols] = pi

    def readout(gb):
        cols = slice(gb * ns, (gb + 1) * ns)
        return (_dot(xr_ref[:, cols].astype(BF16), cmat_ref[gb, :ns, :])
                + _dot(xi_ref[:, cols].astype(BF16), cmat_ref[gb, ns:, :]))

    bu, ys = {0: drive(0)}, []
    for gb in range(nb):
        if gb + 1 < nb:
            bu[gb + 1] = drive(gb + 1)
        scan(gb, bu.pop(gb))
        if gb >= 1:
            ys.append(readout(gb - 1))
    ys.append(readout(nb - 1))

    y = jax.nn.gelu(jnp.concatenate(ys, axis=-1) + d_ref[...] * jnp.concatenate(u, axis=-1))
    z = _dot(y.astype(BF16), wglu_ref[...]) + bglu_ref[...]
    out = z[:, :S5_WIDTH] * jax.nn.sigmoid(z[:, S5_WIDTH:])
    for gb in range(nb):
        tb_ref[gb] = out[:, gb * ch:(gb + 1) * ch]
    for b in range(bsz):
        for gb in range(nb):
            o_ref[b, :, gb * ch:(gb + 1) * ch] = tb_ref[gb, pl.ds(b, n_steps, stride=bsz), :].astype(BF16)


def _s5_call(su, a_re, a_im, bmat, cmat, d_skip, wglu, bglu, layer):
    bsz, seq, _ = su.shape
    tl = S5_TIME_TILE
    tr = tl * bsz
    nb, ch, ns = S5_GROUP_BLOCKS, S5_BLOCK_CH, S5_BLOCK_STATE

    def res(shape):
        nd = len(shape)
        return _resident((None,) + shape, lambda i: (layer,) + (0,) * nd)

    tok = pl.BlockSpec((bsz, tl, S5_WIDTH), lambda i: (0, i, 0))
    return pl.pallas_call(
        functools.partial(_s5_kernel, bsz=bsz, n_steps=tl),
        grid=(seq // tl,),
        in_specs=[tok,
                  res((bsz, S5_NSTATE)), res((bsz, S5_NSTATE)),
                  res((nb, ch, 2 * ns)), res((nb, 2 * ns, ch)),
                  res((1, S5_WIDTH)), res((S5_WIDTH, 2 * S5_WIDTH)), res((1, 2 * S5_WIDTH))],
        out_specs=tok,
        out_shape=jax.ShapeDtypeStruct((bsz, seq, S5_WIDTH), BF16),
        scratch_shapes=[pltpu.VMEM((nb, tr, ch), F32),
                        pltpu.VMEM((tr, S5_NSTATE), F32), pltpu.VMEM((tr, S5_NSTATE), F32),
                        pltpu.VMEM((bsz, S5_NSTATE), F32), pltpu.VMEM((bsz, S5_NSTATE), F32)],
        compiler_params=_params("arbitrary"),
        name="s5_mixer",
    )(su, a_re, a_im, bmat, cmat, d_skip, wglu, bglu)


def _merge_kernel(h_ref, ada_ref, npre_ref, npost_ref, yg_ref, yd_ref, ys_ref, wt_hbm, wbr_hbm, wout_hbm, o_ref,
                  wgate_ref, wbr_ref, wout_ref, stage, sem, *, layer):
    @pl.when((pl.program_id(0) == 0) & (pl.program_id(1) == 0))
    def _():
        _stream_cast(wt_hbm.at[layer, MIX_GATE_ROW:MIX_GATE_ROW + N_BRANCH * D_MODEL], wgate_ref, stage, sem,
                     axis=0, chunk=FF_CHUNK)
        _stream_cast(wbr_hbm.at[layer], wbr_ref, stage, sem, axis=0, chunk=FF_CHUNK)
        _stream_cast(wout_hbm.at[layer], wout_ref, stage, sem, axis=0, chunk=FF_CHUNK)

    blocks = [slice(r * ROW_BLOCK, (r + 1) * ROW_BLOCK) for r in range(h_ref.shape[0] // ROW_BLOCK)]
    us = [_prenorm_mod(h_ref[rows, :], npre_ref[1:2, :], ada_ref[3:4, :], ada_ref[4:5, :]).astype(BF16)
          for rows in blocks]
    merged = [None] * len(blocks)
    for i, y_ref in enumerate((yg_ref, yd_ref, ys_ref)):
        for r, rows in enumerate(blocks):
            gate = jax.nn.sigmoid(_dot_nt(us[r], wgate_ref[i * D_MODEL:(i + 1) * D_MODEL, :]))
            term = gate * _dot(y_ref[rows, :], wbr_ref[i * S5_WIDTH:(i + 1) * S5_WIDTH, :])
            merged[r] = term if merged[r] is None else merged[r] + term
    ys = [_dot(m.astype(BF16), wout_ref[...]) for m in merged]
    for rows, y in zip(blocks, ys):
        o_ref[rows, :] = h_ref[rows, :] + ada_ref[5:6, :] * (_rms(y) * npost_ref[1:2, :])


def _merge_call(h, ada4, norm_pre, norm_post, y_gla, y_diff, y_s5, w_in_t, wbr, wout, layer):
    bsz, seq, _ = h.shape
    tm = TOKEN_TILE

    def tok(w):
        return pl.BlockSpec((None, tm, w), lambda b, i: (b, i, 0))

    return pl.pallas_call(
        functools.partial(_merge_kernel, layer=layer),
        grid=(bsz, seq // tm),
        in_specs=[
            tok(D_MODEL),
            pl.BlockSpec((None, None, ADA_CHUNKS, D_MODEL), lambda b, i: (layer, b, 0, 0)),
            pl.BlockSpec((None, 3, D_MODEL), lambda b, i: (layer, 0, 0)),
            pl.BlockSpec((None, 3, D_MODEL), lambda b, i: (layer, 0, 0)),
            tok(GLA_V_W), tok(DIFF_V_W), tok(S5_WIDTH),
            pl.BlockSpec(memory_space=pl.ANY),
            pl.BlockSpec(memory_space=pl.ANY),
            pl.BlockSpec(memory_space=pl.ANY),
        ],
        out_specs=tok(D_MODEL),
        out_shape=jax.ShapeDtypeStruct(h.shape, F32),
        scratch_shapes=[pltpu.VMEM((N_BRANCH * D_MODEL, D_MODEL), BF16),
                        pltpu.VMEM((N_BRANCH * S5_WIDTH, D_MODEL), BF16), pltpu.VMEM((D_MODEL, D_MODEL), BF16),
                        pltpu.VMEM((WEIGHT_STAGE_SLOTS, FF_CHUNK, D_MODEL), F32),
                        pltpu.SemaphoreType.DMA((WEIGHT_STAGE_SLOTS,))],
        compiler_params=_params("arbitrary", "arbitrary"),
        name="mixer_merge",
    )(h, ada4, norm_pre, norm_post, y_gla, y_diff, y_s5, w_in_t, wbr, wout)


def kernel(x, c, rel_bias, w_ada, b_ada, norm_pre, norm_post, ffn1_w_up, ffn1_w_down, ffn2_w_up, ffn2_w_down,
           w_in, gla_w_alpha, gla_b_alpha, gla_norm, diff_lambda, diff_norm, s5_lam_re, s5_lam_im, s5_log_step,
           s5_b_re, s5_b_im, s5_c_re, s5_c_im, s5_d, s5_w_glu, s5_b_glu, w_branch, w_out):
    bsz, seq, _ = x.shape
    assert bsz == SUBLANES, "the S5 scan keeps the batch on the sublane axis"

    f1 = (ffn1_w_up, ffn1_w_down)
    f2 = (ffn2_w_up, ffn2_w_down)
    w_in_t = jnp.swapaxes(w_in, 1, 2)
    walpha = jnp.pad(gla_w_alpha.astype(BF16), ((0, 0), (0, LANES - GLA_LOWRANK), (0, 0)))
    balpha = gla_b_alpha.reshape(DEPTH, 1, GLA_QK_W)
    wbr = w_branch.reshape(DEPTH, N_BRANCH * S5_WIDTH, D_MODEL)
    wout = w_out
    wglu = s5_w_glu.astype(BF16)
    bglu = s5_b_glu.reshape(DEPTH, 1, 2 * S5_WIDTH)
    d_skip = s5_d.reshape(DEPTH, 1, S5_WIDTH)

    ada4 = _ada_call(c, w_ada, b_ada).reshape(DEPTH, bsz, ADA_CHUNKS, D_MODEL)
    bias = _bias_call(rel_bias)
    a_re, a_im, bmat, cmat = _s5_prep_call(s5_lam_re, s5_lam_im, s5_log_step, s5_b_re, s5_b_im,
                                           s5_c_re, s5_c_im, bsz)

    h = x
    for layer in range(DEPTH):
        h = _ffn_call(h, ada4, norm_pre, norm_post, *f1, layer, 0)
        gq, gk, gv, gr, la, dq, dk, dv, su = _inproj_call(h, ada4, norm_pre, w_in_t, walpha, balpha, layer)
        y_gla = _gla_call(gq, gk, gv, gr, la, gla_norm, layer)
        y_diff = _diff_call(dq, dk, dv, bias, diff_lambda, diff_norm, layer)
        y_s5 = _s5_call(su, a_re, a_im, bmat, cmat, d_skip, wglu, bglu, layer)
        h = _merge_call(h, ada4, norm_pre, norm_post, y_gla, y_diff, y_s5, w_in_t, wbr, wout, layer)
        h = _ffn_call(h, ada4, norm_pre, norm_post, *f2, layer, 2)
    return h
```
